```python
import jax
import jax.numpy as jnp
from jax import lax
import numpy as np

D_MODEL = 2048
BATCH = 1
SEQ = 16384
DEPTH = 4

GRID_W = 64
CTX_LEN = 256
N_MOD = 9
D_FF = 5504
DN_ALPHA = (2 * DEPTH) ** 0.25
DN_BETA = (8 * DEPTH) ** -0.25
LN_EPS = 1e-5
RMS_EPS = 1e-6
CHUNK = 64
ROPE_BASE = 10000.0

GLA_HEADS = 4
GLA_DK = 128
GLA_DV = 256
GLA_GATE_RANK = 16
GLA_GATE_NORM = 16.0
RET_HEADS = 4
RET_DK = 128
RET_DV = 256
GLA_QK = GLA_HEADS * GLA_DK
GLA_V = GLA_HEADS * GLA_DV
RET_QK = RET_HEADS * RET_DK
RET_V = RET_HEADS * RET_DV
EVEN_SPLITS = (GLA_QK, GLA_QK, GLA_V, GLA_V, GLA_GATE_RANK, GLA_GATE_RANK, RET_QK, RET_QK, RET_V, RET_V)
EVEN_IN = 2 * GLA_QK + 2 * GLA_V + 2 * GLA_GATE_RANK + 2 * RET_QK + 2 * RET_V
EVEN_MIX = GLA_V + RET_V

MLA_HEADS = 16
MLA_Q_RANK = 512
MLA_KV_RANK = 512
MLA_NOPE = 128
MLA_ROPE = 64
MLA_DV = 128
MLA_IN = MLA_Q_RANK + MLA_KV_RANK + MLA_ROPE
MLA_SCALE = (MLA_NOPE + MLA_ROPE) ** -0.5
MLA_MIX = MLA_HEADS * MLA_DV
Q_BLOCK = 128

kernel_name = 'hybrid_gla_retnet_mla_flow_trunk'


def layer_norm(x, g, b):
    xf = x.astype(jnp.float32)
    mu = jnp.mean(xf, axis=-1, keepdims=True)
    var = jnp.mean(jnp.square(xf - mu), axis=-1, keepdims=True)
    y = (xf - mu) * lax.rsqrt(var + LN_EPS)
    return (y * g + b).astype(x.dtype)


def rms_norm(x, w=None):
    xf = x.astype(jnp.float32)
    y = xf * lax.rsqrt(jnp.mean(jnp.square(xf), axis=-1, keepdims=True) + RMS_EPS)
    if w is not None:
        y = y * w
    return y.astype(x.dtype)


def modulate(x, shift, scale):
    return x * (1.0 + scale) + shift


def swiglu(h, w_in, w_out):
    gate, up = jnp.split(h @ w_in, 2, axis=-1)
    return (jax.nn.silu(gate) * up) @ w_out


def ffn_sublayer(x, shift, scale, gate, w_in, w_out, g, b):
    y = swiglu(modulate(x, shift, scale), w_in, w_out)
    return layer_norm(DN_ALPHA * x + 0.5 * gate * y, g, b)


def to_heads(a, n_heads):
    B, T, _ = a.shape
    return a.reshape(B, T, n_heads, -1).transpose(0, 2, 1, 3)


def from_heads(a):
    B, H, T, d = a.shape
    return a.transpose(0, 2, 1, 3).reshape(B, T, H * d)


def rope_tables(pos, dim):
    inv = ROPE_BASE ** (-jnp.arange(0, dim, 2, dtype=jnp.float32) / dim)
    ang = pos.astype(jnp.float32)[:, None] * inv[None, :]
    ang = jnp.concatenate([ang, ang], axis=-1)
    return jnp.cos(ang), jnp.sin(ang)


def apply_rope(x, cos, sin):
    x1, x2 = jnp.split(x, 2, axis=-1)
    rot = jnp.concatenate([-x2, x1], axis=-1)
    return x * cos.astype(x.dtype) + rot * sin.astype(x.dtype)


def axial_tables(n_tokens, dim):
    rows = n_tokens // GRID_W
    row = jnp.repeat(jnp.arange(rows), GRID_W)
    col = jnp.tile(jnp.arange(GRID_W), rows)
    return rope_tables(row, dim // 2), rope_tables(col, dim // 2)


def apply_axial(x, tables):
    (cos_r, sin_r), (cos_c, sin_c) = tables
    x_r, x_c = jnp.split(x, 2, axis=-1)
    return jnp.concatenate([apply_rope(x_r, cos_r, sin_r), apply_rope(x_c, cos_c, sin_c)], axis=-1)


def chunks(a):
    B, H, T, d = a.shape
    return jnp.moveaxis(a.reshape(B, H, T // CHUNK, CHUNK, d), 2, 0)


def unchunk(a):
    n, B, H, C, d = a.shape
    return jnp.moveaxis(a, 0, 2).reshape(B, H, n * C, d)


def gla_scan(q, k, v, log_g, s0):
    causal = jnp.tril(jnp.ones((CHUNK, CHUNK), dtype=bool))[:, :, None]

    def step(S, inp):
        qc, kc, vc, gc = (a.astype(jnp.float32) for a in inp)
        b = jnp.cumsum(gc, axis=2)
        rel = jnp.exp(jnp.where(causal, b[:, :, :, None, :] - b[:, :, None, :, :], -jnp.inf))
        A = jnp.einsum('bhid,bhjd,bhijd->bhij', qc, kc, rel)
        o = jnp.einsum('bhij,bhje->bhie', A, vc) + jnp.einsum('bhid,bhde->bhie', qc * jnp.exp(b), S)
        b_end = b[:, :, -1:, :]
        S = jnp.exp(b_end)[:, :, 0, :, None] * S + jnp.einsum('bhjd,bhje->bhde', kc * jnp.exp(b_end - b), vc)
        return S, o

    S, o = lax.scan(step, s0, (chunks(q), chunks(k), chunks(v), chunks(log_g)))
    return unchunk(o), S


def retention_scan(q, k, v, s0, log_gamma):
    idx = jnp.arange(CHUNK, dtype=jnp.float32)
    causal = jnp.tril(jnp.ones((CHUNK, CHUNK), dtype=bool))
    lg = log_gamma[:, None, None]
    dmat = jnp.exp(jnp.where(causal, lg * (idx[:, None] - idx[None, :]), -jnp.inf))
    q_dec = jnp.exp(log_gamma[:, None] * (idx + 1.0))[:, :, None]
    k_dec = jnp.exp(log_gamma[:, None] * (CHUNK - 1.0 - idx))[:, :, None]
    s_dec = jnp.exp(log_gamma * CHUNK)[:, None, None]

    def step(S, inp):
        qc, kc, vc = (a.astype(jnp.float32) for a in inp)
        A = jnp.einsum('bhid,bhjd->bhij', qc, kc) * dmat
        o = jnp.einsum('bhij,bhje->bhie', A, vc) + jnp.einsum('bhid,bhde->bhie', qc * q_dec, S)
        S = s_dec * S + jnp.einsum('bhjd,bhje->bhde', kc * k_dec, vc)
        return S, o

    S, o = lax.scan(step, s0, (chunks(q), chunks(k), chunks(v)))
    return unchunk(o), S


def flip_t(a):
    return jnp.flip(a, axis=2)


def bidir(run_f, run_b, ctx_f, lat_f, ctx_b, lat_b, s0):
    o_cf, s_f = run_f(ctx_f, s0)
    o_lf, _ = run_f(lat_f, s_f)
    o_cb, s_b = run_b(tuple(flip_t(a) for a in ctx_b), s0)
    o_lb, _ = run_b(tuple(flip_t(a) for a in lat_b), s_b)
    return o_cf + flip_t(o_cb), o_lf + flip_t(o_lb)


def even_project(h, w_in, wg_f, bg_f, wg_b, bg_b, ret_rope):
    offs = [int(o) for o in np.cumsum(EVEN_SPLITS)[:-1]]
    gq, gk, gv, gr, gdf, gdb, rq, rk, rv, rg = jnp.split(h @ w_in, offs, axis=-1)
    gla_q = to_heads(gq, GLA_HEADS) * (GLA_DK ** -0.5)
    gla_k = to_heads(gk, GLA_HEADS)
    gla_v = to_heads(gv, GLA_HEADS)
    lg_f = to_heads(jax.nn.log_sigmoid((gdf @ wg_f + bg_f).astype(jnp.float32)) / GLA_GATE_NORM, GLA_HEADS)
    lg_b = to_heads(jax.nn.log_sigmoid((gdb @ wg_b + bg_b).astype(jnp.float32)) / GLA_GATE_NORM, GLA_HEADS)
    ret_q = to_heads(rq, RET_HEADS)
    ret_k = to_heads(rk, RET_HEADS) * (RET_DK ** -0.5)
    if ret_rope is not None:
        ret_q = apply_rope(ret_q, *ret_rope)
        ret_k = apply_rope(ret_k, *ret_rope)
    ret_v = to_heads(rv, RET_HEADS)
    return gla_q, gla_k, gla_v, lg_f, lg_b, gr, ret_q, ret_k, ret_v, rg


def even_output(gla_o, gla_r, ret_o, ret_g, norm_w, w_out):
    dt = gla_r.dtype
    ga = from_heads(rms_norm(gla_o, norm_w)).astype(dt) * jax.nn.silu(gla_r)
    ra = from_heads(rms_norm(ret_o)).astype(dt) * jax.nn.silu(ret_g)
    return jnp.concatenate([ga, ra], axis=-1) @ w_out


def even_mixer(h_ctx, h_lat, ret_rope, w_in, wg_f, bg_f, wg_b, bg_b, norm_w, w_out, need_ctx):
    c_gq, c_gk, c_gv, c_lf, c_lb, c_gr, c_rq, c_rk, c_rv, c_rg = even_project(h_ctx, w_in, wg_f, bg_f, wg_b, bg_b, None)
    l_gq, l_gk, l_gv, l_lf, l_lb, l_gr, l_rq, l_rk, l_rv, l_rg = even_project(h_lat, w_in, wg_f, bg_f, wg_b, bg_b, ret_rope)
    B = h_lat.shape[0]
    gla = lambda t, s: gla_scan(*t, s)
    g_ctx, g_lat = bidir(gla, gla,
                         (c_gq, c_gk, c_gv, c_lf), (l_gq, l_gk, l_gv, l_lf),
                         (c_gq, c_gk, c_gv, c_lb), (l_gq, l_gk, l_gv, l_lb),
                         jnp.zeros((B, GLA_HEADS, GLA_DK, GLA_DV), jnp.float32))
    scales = jnp.arange(RET_HEADS, dtype=jnp.float32)
    lg_fwd = jnp.log1p(-jnp.exp2(-5.0 - scales))
    lg_bwd = jnp.log1p(-jnp.exp2(-5.5 - scales))
    ret_f = lambda t, s: retention_scan(*t, s, lg_fwd)
    ret_b = lambda t, s: retention_scan(*t, s, lg_bwd)
    r_ctx_in = (c_rq, c_rk, c_rv)
    r_lat_in = (l_rq, l_rk, l_rv)
    r_ctx, r_lat = bidir(ret_f, ret_b, r_ctx_in, r_lat_in, r_ctx_in, r_lat_in,
                         jnp.zeros((B, RET_HEADS, RET_DK, RET_DV), jnp.float32))
    y_lat = even_output(g_lat, l_gr, r_lat, l_rg, norm_w, w_out)
    y_ctx = even_output(g_ctx, c_gr, r_ctx, c_rg, norm_w, w_out) if need_ctx else None
    return y_ctx, y_lat


def mla_project(h, w_in, q_norm_w, kv_norm_w, w_uq, w_ukv, axial):
    cq, ckv, k_rope = jnp.split(h @ w_in, [MLA_Q_RANK, MLA_Q_RANK + MLA_KV_RANK], axis=-1)
    q = to_heads(rms_norm(cq, q_norm_w) @ w_uq, MLA_HEADS)
    q_nope, q_rope = q[..., :MLA_NOPE], q[..., MLA_NOPE:]
    kv = to_heads(rms_norm(ckv, kv_norm_w) @ w_ukv, MLA_HEADS)
    k_nope, v = kv[..., :MLA_NOPE], kv[..., MLA_NOPE:]
    if axial is not None:
        q_rope = apply_axial(q_rope, axial)
        k_rope = apply_axial(k_rope, axial)
    return q_nope, q_rope, k_nope, k_rope, v


def mla_attend(qn, qr, kn, kr, v):
    s = jnp.einsum('bhqd,bhkd->bhqk', qn, kn) + jnp.einsum('bhqr,bkr->bhqk', qr, kr)
    p = jax.nn.softmax(s.astype(jnp.float32) * MLA_SCALE, axis=-1).astype(v.dtype)
    return jnp.einsum('bhqk,bhkd->bhqd', p, v)


def mla_blocked(qn, qr, kn, kr, v):
    B, H, T, _ = qn.shape
    nb = T // Q_BLOCK
    blk = lambda a: jnp.moveaxis(a.reshape(B, H, nb, Q_BLOCK, a.shape[-1]), 2, 0)
    o = lax.map(lambda qs: mla_attend(qs[0], qs[1], kn, kr, v), (blk(qn), blk(qr)))
    return jnp.moveaxis(o, 0, 2).reshape(B, H, T, MLA_DV)


def odd_mixer(h_ctx, h_lat, axial, w_in, q_norm_w, kv_norm_w, w_uq, w_ukv, w_out, need_ctx):
    qn_c, qr_c, kn_c, kr_c, v_c = mla_project(h_ctx, w_in, q_norm_w, kv_norm_w, w_uq, w_ukv, None)
    qn_l, qr_l, kn_l, kr_l, v_l = mla_project(h_lat, w_in, q_norm_w, kv_norm_w, w_uq, w_ukv, axial)
    kn = jnp.concatenate([kn_c, kn_l], axis=2)
    kr = jnp.concatenate([kr_c, kr_l], axis=1)
    v = jnp.concatenate([v_c, v_l], axis=2)
    y_lat = from_heads(mla_blocked(qn_l, qr_l, kn, kr, v)) @ w_out
    y_ctx = from_heads(mla_attend(qn_c, qr_c, kn_c, kr_c, v_c)) @ w_out if need_ctx else None
    return y_ctx, y_lat


def setup_inputs(seed: int = 0) -> dict:
    key = jax.random.key(seed)
    k = jax.random.split(key, 23)
    d = D_MODEL
    n_even = (DEPTH + 1) // 2
    n_odd = DEPTH // 2

    def nrm(i, shape, scale):
        return jax.random.normal(k[i], shape, jnp.float32) * scale

    return {
        'x': nrm(0, (BATCH, SEQ, d), 1.0),
        'c': nrm(1, (BATCH, d), 1.0),
        'ctx': nrm(2, (BATCH, CTX_LEN, d), 1.0),
        'c_ctx': nrm(3, (d,), 1.0),
        'w_ada': nrm(4, (DEPTH, d, N_MOD * d), d ** -0.5),
        'b_ada': nrm(5, (DEPTH, N_MOD * d), 0.02),
        'ln_g': 1.0 + nrm(6, (DEPTH, 3, d), 0.02),
        'ln_b': nrm(7, (DEPTH, 3, d), 0.02),
        'w_ffn_in': nrm(8, (DEPTH, 2, d, 2 * D_FF), d ** -0.5),
        'w_ffn_out': nrm(9, (DEPTH, 2, D_FF, d), D_FF ** -0.5 * DN_BETA),
        'ev_w_in': nrm(10, (n_even, d, EVEN_IN), d ** -0.5),
        'ev_gla_wg_f': nrm(11, (n_even, GLA_GATE_RANK, GLA_QK), GLA_GATE_RANK ** -0.5),
        'ev_gla_bg_f': nrm(12, (n_even, GLA_QK), 0.02),
        'ev_gla_wg_b': nrm(13, (n_even, GLA_GATE_RANK, GLA_QK), GLA_GATE_RANK ** -0.5),
        'ev_gla_bg_b': nrm(14, (n_even, GLA_QK), 0.02),
        'ev_gla_norm': 1.0 + nrm(15, (n_even, GLA_DV), 0.02),
        'ev_w_out': nrm(16, (n_even, EVEN_MIX, d), EVEN_MIX ** -0.5 * DN_BETA),
        'od_w_in': nrm(17, (n_odd, d, MLA_IN), d ** -0.5),
        'od_q_norm': 1.0 + nrm(18, (n_odd, MLA_Q_RANK), 0.02),
        'od_kv_norm': 1.0 + nrm(19, (n_odd, MLA_KV_RANK), 0.02),
        'od_w_uq': nrm(20, (n_odd, MLA_Q_RANK, MLA_HEADS * (MLA_NOPE + MLA_ROPE)), MLA_Q_RANK ** -0.5),
        'od_w_ukv': nrm(21, (n_odd, MLA_KV_RANK, MLA_HEADS * (MLA_NOPE + MLA_DV)), MLA_KV_RANK ** -0.5),
        'od_w_out': nrm(22, (n_odd, MLA_MIX, d), MLA_MIX ** -0.5 * DN_BETA),
    }


def reference(x, c, ctx, c_ctx, w_ada, b_ada, ln_g, ln_b, w_ffn_in, w_ffn_out,
              ev_w_in, ev_gla_wg_f, ev_gla_bg_f, ev_gla_wg_b, ev_gla_bg_b, ev_gla_norm, ev_w_out,
              od_w_in, od_q_norm, od_kv_norm, od_w_uq, od_w_ukv, od_w_out):
    n_lat = x.shape[1]
    axial = axial_tables(n_lat, MLA_ROPE)
    ret_rope = rope_tables(jnp.arange(n_lat), RET_DK)
    s_c = jax.nn.silu(c)
    s_cc = jax.nn.silu(c_ctx)
    x_lat, x_ctx = x, ctx
    for l in range(DEPTH):
        need_ctx = l < DEPTH - 1
        m_lat = jnp.split((s_c @ w_ada[l] + b_ada[l])[:, None, :], N_MOD, axis=-1)
        m_ctx = jnp.split((s_cc @ w_ada[l] + b_ada[l])[None, None, :], N_MOD, axis=-1)
        x_ctx = ffn_sublayer(x_ctx, *m_ctx[0:3], w_ffn_in[l, 0], w_ffn_out[l, 0], ln_g[l, 0], ln_b[l, 0])
        x_lat = ffn_sublayer(x_lat, *m_lat[0:3], w_ffn_in[l, 0], w_ffn_out[l, 0], ln_g[l, 0], ln_b[l, 0])
        h_ctx = modulate(x_ctx, m_ctx[3], m_ctx[4])
        h_lat = modulate(x_lat, m_lat[3], m_lat[4])
        if l % 2 == 0:
            e = l // 2
            y_ctx, y_lat = even_mixer(h_ctx, h_lat, ret_rope, ev_w_in[e], ev_gla_wg_f[e], ev_gla_bg_f[e],
                                      ev_gla_wg_b[e], ev_gla_bg_b[e], ev_gla_norm[e], ev_w_out[e], need_ctx)
        else:
            o = l // 2
            y_ctx, y_lat = odd_mixer(h_ctx, h_lat, axial, od_w_in[o], od_q_norm[o], od_kv_norm[o],
                                     od_w_uq[o], od_w_ukv[o], od_w_out[o], need_ctx)
        x_lat = layer_norm(DN_ALPHA * x_lat + m_lat[5] * y_lat, ln_g[l, 1], ln_b[l, 1])
        x_lat = ffn_sublayer(x_lat, *m_lat[6:9], w_ffn_in[l, 1], w_ffn_out[l, 1], ln_g[l, 2], ln_b[l, 2])
        if need_ctx:
            x_ctx = layer_norm(DN_ALPHA * x_ctx + m_ctx[5] * y_ctx, ln_g[l, 1], ln_b[l, 1])
            x_ctx = ffn_sublayer(x_ctx, *m_ctx[6:9], w_ffn_in[l, 1], w_ffn_out[l, 1], ln_g[l, 2], ln_b[l, 2])
    return x_lat
```

```python
import functools
import math

import numpy as np
import jax
import jax.numpy as jnp
from jax import lax
from jax.experimental import pallas as pl
from jax.experimental.pallas import tpu as pltpu

F32 = jnp.float32
MXU_DTYPE = jnp.bfloat16

N_MOD = 9
LN_EPS = 1e-5
RMS_EPS = 1e-6
CHUNK = 64
ROPE_BASE = 10000.0
GRID_W = 64
GLA_HEADS = 4
GLA_DK = 128
GLA_DV = 256
GLA_GATE_RANK = 16
GLA_GATE_NORM = 16.0
RET_HEADS = 4
RET_DK = 128
RET_DV = 256
MLA_HEADS = 16
MLA_Q_RANK = 512
MLA_KV_RANK = 512
MLA_NOPE = 128
MLA_ROPE = 64
MLA_DV = 128
MLA_SCALE = (MLA_NOPE + MLA_ROPE) ** -0.5
LOG2E = 1.4426950408889634

LANES = 128
ROW_TILE = 512
FF_CHUNK = 512
SCAN_ROWS = 256
MLA_PROJ_ROWS = 256
KV_TILE = 1280
QK_PAD = 256
VMEM_LIMIT = 56 * 1024 * 1024


def _cparams(n_axes, vmem=VMEM_LIMIT):
    return pltpu.CompilerParams(
        dimension_semantics=("arbitrary",) * n_axes, vmem_limit_bytes=vmem)


def _mx(a):
    return a.astype(MXU_DTYPE)


def _dot(a, b):
    return jnp.dot(a, b, preferred_element_type=F32)


def _dot_nt(a, b):
    return lax.dot_general(a, b, (((1,), (1,)), ((), ())), preferred_element_type=F32)


def _dot_tn(a, b):
    return lax.dot_general(a, b, (((0,), (0,)), ((), ())), preferred_element_type=F32)


def _silu(v):
    return v / (1.0 + jnp.exp(-v))


def _layer_norm(v, g, b):
    mu = jnp.mean(v, axis=-1, keepdims=True)
    d = v - mu
    var = jnp.mean(d * d, axis=-1, keepdims=True)
    return d * lax.rsqrt(var + LN_EPS) * g + b


def _ada_kernel(c_ref, w_ref, b_ref, o_ref):
    s = _silu(c_ref[...])
    o_ref[...] = _dot(_mx(s), _mx(w_ref[...])) + b_ref[...]


def _ada(c_rows, w_ada, b_ada):
    depth, d, n = w_ada.shape
    rows = c_rows.shape[0]
    tn = 1024
    return pl.pallas_call(
        _ada_kernel,
        grid=(depth, n // tn),
        in_specs=[
            pl.BlockSpec((rows, d), lambda l, j: (0, 0)),
            pl.BlockSpec((None, d, tn), lambda l, j: (l, 0, j)),
            pl.BlockSpec((None, 1, tn), lambda l, j: (l, 0, j)),
        ],
        out_specs=pl.BlockSpec((None, rows, tn), lambda l, j: (l, 0, j)),
        out_shape=jax.ShapeDtypeStruct((depth, rows, n), F32),
        compiler_params=_cparams(2),
        name="ada_mod",
    )(c_rows, w_ada, b_ada.reshape(depth, 1, n))


def _ffn_kernel(x_ref, mod_ref, win_ref, wout_ref, g_ref, b_ref, *rest,
                mod_row, emit_row, alpha, tk):
    if emit_row is None:
        o_ref, xm_scr, acc_scr = rest
        h_ref = None
    else:
        o_ref, h_ref, xm_scr, acc_scr = rest
    k = pl.program_id(1)

    @pl.when(k == 0)
    def _():
        shift = mod_ref[mod_row:mod_row + 1, :]
        scale = mod_ref[mod_row + 1:mod_row + 2, :]
        xm_scr[...] = _mx(x_ref[...] * (1.0 + scale) + shift)
        acc_scr[...] = jnp.zeros_like(acc_scr)

    h = _dot(xm_scr[...], win_ref[...])
    act = _silu(h[:, :tk]) * h[:, tk:]
    acc_scr[...] += _dot(_mx(act), wout_ref[...])

    @pl.when(k == pl.num_programs(1) - 1)
    def _():
        gate = mod_ref[mod_row + 2:mod_row + 3, :]
        v = alpha * x_ref[...] + 0.5 * gate * acc_scr[...]
        y = _layer_norm(v, g_ref[...], b_ref[...])
        o_ref[...] = y
        if h_ref is not None:
            shift = mod_ref[emit_row:emit_row + 1, :]
            scale = mod_ref[emit_row + 1:emit_row + 2, :]
            h_ref[...] = _mx(y * (1.0 + scale) + shift)


def _ffn(x_all, mods, w_in_p, w_out_p, ln_g, ln_b, *, mod_row, emit_row, alpha,
         n_lat_tiles, n_tiles):
    d = x_all.shape[1]
    tm, tk = ROW_TILE, FF_CHUNK
    n_chunks = w_out_p.shape[0] // tk
    rows = n_tiles * tm
    out_shape = [jax.ShapeDtypeStruct((rows, d), F32)]
    out_specs = [pl.BlockSpec((tm, d), lambda i, k: (i, 0))]
    if emit_row is not None:
        out_shape.append(jax.ShapeDtypeStruct((rows, d), MXU_DTYPE))
        out_specs.append(pl.BlockSpec((tm, d), lambda i, k: (i, 0)))
    kern = functools.partial(_ffn_kernel, mod_row=mod_row, emit_row=emit_row,
                             alpha=alpha, tk=tk)
    res = pl.pallas_call(
        kern,
        grid=(n_tiles, n_chunks),
        in_specs=[
            pl.BlockSpec((tm, d), lambda i, k: (i, 0)),
            pl.BlockSpec((None, N_MOD, d), lambda i, k: (i // n_lat_tiles, 0, 0)),
            pl.BlockSpec((d, 2 * tk), lambda i, k: (0, k)),
            pl.BlockSpec((tk, d), lambda i, k: (k, 0)),
            pl.BlockSpec((1, d), lambda i, k: (0, 0)),
            pl.BlockSpec((1, d), lambda i, k: (0, 0)),
        ],
        out_specs=out_specs,
        out_shape=out_shape,
        scratch_shapes=[pltpu.VMEM((tm, d), MXU_DTYPE), pltpu.VMEM((tm, d), F32)],
        compiler_params=_cparams(2),
        name="ffn",
    )(x_all, mods, w_in_p, w_out_p, ln_g.reshape(1, d), ln_b.reshape(1, d))
    return res if emit_row is not None else (res[0], None)


def _proj_kernel(h_ref, w_ref, o_ref):
    o_ref[...] = _dot(h_ref[...], w_ref[...])


def _proj(h, w, tn):
    rows, kdim = h.shape
    n = w.shape[1]
    tm = ROW_TILE
    return pl.pallas_call(
        _proj_kernel,
        grid=(n // tn, rows // tm),
        in_specs=[
            pl.BlockSpec((tm, kdim), lambda j, i: (i, 0)),
            pl.BlockSpec((kdim, tn), lambda j, i: (0, j)),
        ],
        out_specs=pl.BlockSpec((tm, tn), lambda j, i: (i, j)),
        out_shape=jax.ShapeDtypeStruct((rows, n), F32),
        compiler_params=_cparams(2),
        name="even_proj",
    )(h, w)


def _outproj_kernel(a_ref, w_ref, x_ref, mod_ref, g_ref, b_ref, o_ref, *, gate_row, alpha):
    y = _dot(a_ref[...], w_ref[...])
    gate = mod_ref[gate_row:gate_row + 1, :]
    v = alpha * x_ref[...] + gate * y
    o_ref[...] = _layer_norm(v, g_ref[...], b_ref[...])


def _outproj_ln(a, w, x_all, mods, ln_g, ln_b, *, gate_row, alpha, n_lat_tiles):
    rows, d = x_all.shape
    kdim = a.shape[1]
    tm = ROW_TILE
    kern = functools.partial(_outproj_kernel, gate_row=gate_row, alpha=alpha)
    return pl.pallas_call(
        kern,
        grid=(rows // tm,),
        in_specs=[
            pl.BlockSpec((tm, kdim), lambda i: (i, 0)),
            pl.BlockSpec((kdim, d), lambda i: (0, 0)),
            pl.BlockSpec((tm, d), lambda i: (i, 0)),
            pl.BlockSpec((None, N_MOD, d), lambda i: (i // n_lat_tiles, 0, 0)),
            pl.BlockSpec((1, d), lambda i: (0, 0)),
            pl.BlockSpec((1, d), lambda i: (0, 0)),
        ],
        out_specs=pl.BlockSpec((tm, d), lambda i: (i, 0)),
        out_shape=jax.ShapeDtypeStruct((rows, d), F32),
        compiler_params=_cparams(1),
        name="mix_out",
    )(a, w, x_all, mods, ln_g.reshape(1, d), ln_b.reshape(1, d))


_GLA_LEVELS = (64, 32, 16)


def _scan_positions(reverse):
    idx = np.arange(CHUNK)
    return idx[::-1].copy() if reverse else idx


def _gla_exponent_matrix(reverse):
    pos = _scan_positions(reverse)
    pi, pt = pos[:, None], pos[None, :]
    mats = [pt <= pi,
            pt > pi]
    for n in _GLA_LEVELS:
        half = n // 2
        ref = (pi // n) * n + half - 1
        q_side = (pi % n) >= half
        mats.append(np.where(q_side, (pt > ref) & (pt <= pi), (pt > pi) & (pt <= ref)))
    return np.concatenate(mats, axis=0).astype(np.float32)


def _ret_constants(reverse):
    scales = np.arange(RET_HEADS, dtype=np.float64)
    off = 5.5 if reverse else 5.0
    log_gamma = np.log1p(-np.exp2(-off - scales))
    idx = np.arange(CHUNK, dtype=np.float64)
    diff = idx[:, None] - idx[None, :]
    dmat = np.where(diff >= 0, np.exp(log_gamma[:, None, None] * np.maximum(diff, 0.0)), 0.0)
    q_dec = np.exp(log_gamma[:, None] * (idx + 1.0))
    k_dec = np.exp(log_gamma[:, None] * (CHUNK - 1.0 - idx))
    s_dec = np.exp(log_gamma * CHUNK)
    if reverse:
        dmat = dmat[:, ::-1, ::-1]
        q_dec = q_dec[:, ::-1]
        k_dec = k_dec[:, ::-1]
    expand = lambda a: np.repeat(a.T, RET_DK, axis=1)
    return (dmat.astype(np.float32), expand(q_dec).astype(np.float32),
            expand(k_dec).astype(np.float32),
            np.repeat(s_dec, RET_DK)[None, :].astype(np.float32))


def _log_sigmoid(v):
    return jnp.minimum(v, 0.0) - jnp.log1p(jnp.exp(-jnp.abs(v)))


def _gla_masks(reverse):
    c = CHUNK
    row = lax.broadcasted_iota(jnp.int32, (c, c), 0)
    col = lax.broadcasted_iota(jnp.int32, (c, c), 1)
    pr, pc = ((c - 1) - row, (c - 1) - col) if reverse else (row, col)
    levels = []
    for n in _GLA_LEVELS:
        sh, half = int(math.log2(n)), n // 2
        levels.append((lax.shift_right_logical(pr, sh) == lax.shift_right_logical(pc, sh))
                      & ((pr & (n - 1)) >= half) & ((pc & (n - 1)) < half))
    sub = lax.broadcasted_iota(jnp.int32, (c, 1), 0) & 7
    if reverse:
        sub = 7 - sub
    diags = [(pr - pc == delta) & ((pr & 7) >= delta) for delta in range(8)]
    return levels, diags, sub


def _gla_chunk(q, k, v, e_all, st_ref, masks, reverse):
    c = CHUNK
    level_masks, diag_masks, sub = masks
    e_cum = e_all[0:c]
    w_all = jnp.exp(e_all)
    w_cum, w_end = w_all[0:c], w_all[c:2 * c]
    s_t = st_ref[...]
    o = _dot_nt(_mx(q * w_cum), _mx(s_t))
    a = jnp.zeros((c, c), F32)
    for li in range(len(_GLA_LEVELS)):
        w_l = w_all[(2 + li) * c:(3 + li) * c]
        a = a + jnp.where(level_masks[li], _dot_nt(_mx(q * w_l), _mx(k * w_l)), 0.0)
    for delta in range(8):
        if delta == 0:
            t = q * k
        else:
            shift = (c - delta) if reverse else delta
            k_s = pltpu.roll(k, shift, 0)
            e_s = pltpu.roll(e_cum, shift, 0)
            t = q * k_s * jnp.exp(jnp.where(sub >= delta, e_cum - e_s, -jnp.inf))
        dsum = jnp.sum(t, axis=-1, keepdims=True)
        a = a + jnp.where(diag_masks[delta], dsum, 0.0)
    o = o + _dot(_mx(a), _mx(v))
    if reverse:
        decay = w_cum[0:1, :]
    else:
        decay = w_cum[c - 1:c, :]
    st_ref[...] = decay * s_t + _dot_tn(_mx(v), _mx(k * w_end))
    return o


def _ret_chunk(q, k, v, dmat, q_dec, k_dec, s_dec, st_ref):
    s_t = st_ref[...]
    a = _dot_nt(_mx(q), _mx(k)) * dmat
    o = _dot(_mx(a), _mx(v)) + _dot_nt(_mx(q * q_dec), _mx(s_t))
    st_ref[...] = s_dec * s_t + _dot_tn(_mx(v), _mx(k * k_dec))
    return o


def _rope128(xh, cos, sin_signed):
    return xh * cos + pltpu.roll(xh, 64, 1) * sin_signed


def _scan_dir(refs, consts, o_ref, sg_scr, sr_scr, d):
    zqk, zv, zrqk, zrv, zgd, cos_ref, sin_ref = refs
    wg_ref, bg_ref, mexp_ref, dmat_ref, qdec_ref, kdec_ref, sdec_ref = consts
    reverse = d == 1
    n_chunks = SCAN_ROWS // CHUNK
    hq = GLA_HEADS * GLA_DK
    pre = _dot(_mx(zgd[...]), wg_ref[d]) + bg_ref[d]
    lg = _log_sigmoid(pre) * (1.0 / GLA_GATE_NORM)
    cos = cos_ref[...]
    sin = sin_ref[...]
    mexp = mexp_ref[d]
    masks = _gla_masks(reverse)
    for cc in range(n_chunks):
        ci = (n_chunks - 1 - cc) if reverse else cc
        r0 = ci * CHUNK
        g = lg[r0:r0 + CHUNK]
        g_hi = _mx(g)
        g_lo = _mx(g - g_hi.astype(F32))
        e = _dot(mexp, g_hi) + _dot(mexp, g_lo)
        for h in range(GLA_HEADS):
            q = zqk[r0:r0 + CHUNK, h * GLA_DK:(h + 1) * GLA_DK] * (GLA_DK ** -0.5)
            k = zqk[r0:r0 + CHUNK, hq + h * GLA_DK:hq + (h + 1) * GLA_DK]
            v = zv[r0:r0 + CHUNK, h * GLA_DV:(h + 1) * GLA_DV]
            o = _gla_chunk(q, k, v, e[:, h * GLA_DK:(h + 1) * GLA_DK], sg_scr.at[d, h], masks, reverse)
            o_ref[r0:r0 + CHUNK, h * GLA_DV:(h + 1) * GLA_DV] = o
        hr = RET_HEADS * RET_DK
        o_off = GLA_HEADS * GLA_DV
        cs = cos[r0:r0 + CHUNK]
        sn = sin[r0:r0 + CHUNK]
        for h in range(RET_HEADS):
            lanes = slice(h * RET_DK, (h + 1) * RET_DK)
            q = _rope128(zrqk[r0:r0 + CHUNK, h * RET_DK:(h + 1) * RET_DK], cs, sn)
            k = _rope128(zrqk[r0:r0 + CHUNK, hr + h * RET_DK:hr + (h + 1) * RET_DK]
                         * (RET_DK ** -0.5), cs, sn)
            v = zrv[r0:r0 + CHUNK, h * RET_DV:(h + 1) * RET_DV]
            o = _ret_chunk(q, k, v, dmat_ref[d, h], qdec_ref[d][:, lanes], kdec_ref[d][:, lanes],
                           sdec_ref[d][:, lanes], sr_scr.at[d, h])
            o_ref[r0:r0 + CHUNK, o_off + h * RET_DV:o_off + (h + 1) * RET_DV] = o


def _scan_kernel(*refs):
    fwd, bwd, consts = refs[0:7], refs[7:14], refs[14:21]
    of_ref, ob_ref, sg_scr, sr_scr = refs[21:]

    @pl.when(pl.program_id(0) == 0)
    def _():
        sg_scr[...] = jnp.zeros_like(sg_scr)
        sr_scr[...] = jnp.zeros_like(sr_scr)

    _scan_dir(fwd, consts, of_ref, sg_scr, sr_scr, 0)
    _scan_dir(bwd, consts, ob_ref, sg_scr, sr_scr, 1)


def _even_scan(z, cos_t, sin_t, wg, bg, n_seq_blocks):
    rows = z.shape[0]
    n_blocks = rows // SCAN_ROWS
    n_lat_blocks = n_seq_blocks - (256 // SCAN_ROWS)
    r = SCAN_ROWS

    def rf(i):
        return jnp.where(i < n_seq_blocks, (i + n_lat_blocks) % n_seq_blocks, i)

    def rb(i):
        return jnp.where(i < n_seq_blocks, n_seq_blocks - 1 - i, i)

    def stream_specs(rmap):
        return [
            pl.BlockSpec((r, 1024), lambda i: (rmap(i), 0)),
            pl.BlockSpec((r, 1024), lambda i: (rmap(i), 1)),
            pl.BlockSpec((r, 1024), lambda i: (rmap(i), 3)),
            pl.BlockSpec((r, 1024), lambda i: (rmap(i), 4)),
            pl.BlockSpec((r, LANES), lambda i: (rmap(i), 48)),
            pl.BlockSpec((r, LANES), lambda i: (rmap(i), 0)),
            pl.BlockSpec((r, LANES), lambda i: (rmap(i), 0)),
        ]

    mexp = jnp.asarray(np.stack([_gla_exponent_matrix(False), _gla_exponent_matrix(True)]),
                       MXU_DTYPE)
    rc = [_ret_constants(False), _ret_constants(True)]
    dmat = jnp.asarray(np.stack([rc[0][0], rc[1][0]]))
    qdec = jnp.asarray(np.stack([rc[0][1], rc[1][1]]))
    kdec = jnp.asarray(np.stack([rc[0][2], rc[1][2]]))
    sdec = jnp.asarray(np.stack([rc[0][3], rc[1][3]]))
    consts = [wg, bg, mexp, dmat, qdec, kdec, sdec]

    def full_spec(a):
        nd = a.ndim
        return pl.BlockSpec(a.shape, lambda i: (0,) * nd)

    d_out = GLA_HEADS * GLA_DV + RET_HEADS * RET_DV
    return pl.pallas_call(
        _scan_kernel,
        grid=(n_blocks,),
        in_specs=stream_specs(rf) + stream_specs(rb) + [full_spec(a) for a in consts],
        out_specs=[pl.BlockSpec((r, d_out), lambda i: (rf(i), 0)),
                   pl.BlockSpec((r, d_out), lambda i: (rb(i), 0))],
        out_shape=[jax.ShapeDtypeStruct((rows, d_out), F32)] * 2,
        scratch_shapes=[pltpu.VMEM((2, GLA_HEADS, GLA_DV, GLA_DK), F32),
                        pltpu.VMEM((2, RET_HEADS, RET_DV, RET_DK), F32)],
        compiler_params=_cparams(1),
        name="even_scan",
    )(z, z, z, z, z, cos_t, sin_t, z, z, z, z, z, cos_t, sin_t, *consts)


def _evgate_kernel(of_ref, ob_ref, gr_ref, rg_ref, nw_ref, a_ref):
    n_heads = GLA_HEADS + RET_HEADS
    for h in range(n_heads):
        lanes = slice(h * GLA_DV, (h + 1) * GLA_DV)
        oh = of_ref[:, lanes] + ob_ref[:, lanes]
        y = oh * lax.rsqrt(jnp.mean(oh * oh, axis=-1, keepdims=True) + RMS_EPS)
        if h < GLA_HEADS:
            y = y * nw_ref[...]
            gate = gr_ref[:, h * GLA_DV:(h + 1) * GLA_DV]
        else:
            hh = h - GLA_HEADS
            gate = rg_ref[:, hh * RET_DV:(hh + 1) * RET_DV]
        a_ref[:, lanes] = _mx(y * _silu(gate))


def _even_gate(o_f, o_b, z, norm_w):
    rows, d_out = o_f.shape
    tm = ROW_TILE
    return pl.pallas_call(
        _evgate_kernel,
        grid=(rows // tm,),
        in_specs=[
            pl.BlockSpec((tm, d_out), lambda i: (i, 0)),
            pl.BlockSpec((tm, d_out), lambda i: (i, 0)),
            pl.BlockSpec((tm, 1024), lambda i: (i, 2)),
            pl.BlockSpec((tm, 1024), lambda i: (i, 5)),
            pl.BlockSpec((1, GLA_DV), lambda i: (0, 0)),
        ],
        out_specs=pl.BlockSpec((tm, d_out), lambda i: (i, 0)),
        out_shape=jax.ShapeDtypeStruct((rows, d_out), MXU_DTYPE),
        compiler_params=_cparams(1),
        name="even_gate",
    )(o_f, o_b, z, z, norm_w.reshape(1, GLA_DV))


def _axial128(xh, cos, sin_a, sin_b):
    return xh * cos + pltpu.roll(xh, LANES - 16, 1) * sin_a + pltpu.roll(xh, 16, 1) * sin_b


def _mlaproj_kernel(h_ref, win_ref, qn_ref, kvn_ref, wuq_ref, wuk_ref, wuvt_ref,
                    cos_ref, sina_ref, sinb_ref, q_ref, k_ref, vt_ref):
    z = _dot(h_ref[...], win_ref[...])
    cq = z[:, :MLA_Q_RANK]
    ckv = z[:, MLA_Q_RANK:MLA_Q_RANK + MLA_KV_RANK]
    kr = z[:, MLA_Q_RANK + MLA_KV_RANK:]
    cos, sin_a, sin_b = cos_ref[...], sina_ref[...], sinb_ref[...]
    cqn = _mx(cq * lax.rsqrt(jnp.mean(cq * cq, axis=-1, keepdims=True) + RMS_EPS) * qn_ref[...])
    ckvn = _mx(ckv * lax.rsqrt(jnp.mean(ckv * ckv, axis=-1, keepdims=True) + RMS_EPS)
               * kvn_ref[...])
    q = _dot(cqn, wuq_ref[...])
    kn = _dot(ckvn, wuk_ref[...])
    vt_ref[...] = _mx(_dot_nt(wuvt_ref[...], ckvn))
    kr_rot = _mx(_axial128(kr, cos, sin_a, sin_b))
    qs = MLA_SCALE * LOG2E
    for h in range(MLA_HEADS):
        b0 = h * QK_PAD
        q_ref[:, b0:b0 + LANES] = _mx(q[:, b0:b0 + LANES] * qs)
        q_ref[:, b0 + LANES:b0 + QK_PAD] = _mx(
            _axial128(q[:, b0 + LANES:b0 + QK_PAD], cos, sin_a, sin_b) * qs)
        k_ref[:, b0:b0 + LANES] = _mx(kn[:, h * MLA_NOPE:(h + 1) * MLA_NOPE])
        k_ref[:, b0 + LANES:b0 + QK_PAD] = kr_rot


def _mla_proj(h, w_in_p, q_norm, kv_norm, w_uq_p, w_uk, w_uvt, cos_t, sina_t, sinb_t):
    rows, d = h.shape
    tm = MLA_PROJ_ROWS
    hq = MLA_HEADS * QK_PAD
    hv = MLA_HEADS * MLA_DV

    def const2(a):
        return pl.BlockSpec(a.shape, lambda i: (0, 0))

    qn = q_norm.reshape(1, -1)
    kvn = kv_norm.reshape(1, -1)
    return pl.pallas_call(
        _mlaproj_kernel,
        grid=(rows // tm,),
        in_specs=[pl.BlockSpec((tm, d), lambda i: (i, 0)),
                  const2(w_in_p), const2(qn), const2(kvn), const2(w_uq_p), const2(w_uk),
                  const2(w_uvt),
                  pl.BlockSpec((tm, LANES), lambda i: (i, 0)),
                  pl.BlockSpec((tm, LANES), lambda i: (i, 0)),
                  pl.BlockSpec((tm, LANES), lambda i: (i, 0))],
        out_specs=[pl.BlockSpec((tm, hq), lambda i: (i, 0)),
                   pl.BlockSpec((tm, hq), lambda i: (i, 0)),
                   pl.BlockSpec((hv, tm), lambda i: (0, i))],
        out_shape=[jax.ShapeDtypeStruct((rows, hq), MXU_DTYPE),
                   jax.ShapeDtypeStruct((rows, hq), MXU_DTYPE),
                   jax.ShapeDtypeStruct((hv, rows), MXU_DTYPE)],
        compiler_params=_cparams(1),
        name="mla_proj",
    )(h, w_in_p, qn, kvn, w_uq_p, w_uk, w_uvt, cos_t, sina_t, sinb_t)


def _flash_kernel(q_ref, k_ref, vt_ref, o_ref, *, n_lat_tiles, n_lat, n_ctx):
    tq = q_ref.shape[0]
    q = q_ref[...]

    def attend(kv_slices):
        m = jnp.full((1, tq), -jnp.inf, F32)
        l = jnp.zeros((1, tq), F32)
        acc = jnp.zeros((MLA_DV, tq), F32)
        for start, size in kv_slices:
            s = _dot_nt(k_ref[start:start + size, :], q)
            m_new = jnp.maximum(m, jnp.max(s, axis=0, keepdims=True))
            alpha = jnp.exp2(m - m_new)
            p = jnp.exp2(s - m_new)
            l = alpha * l + jnp.sum(p, axis=0, keepdims=True)
            acc = alpha * acc + _dot(vt_ref[:, start:start + size], _mx(p))
            m = m_new
        o_ref[...] = (acc / l).T.astype(o_ref.dtype)

    n_kv = n_lat + n_ctx
    i = pl.program_id(1)

    @pl.when(i < n_lat_tiles)
    def _():
        attend([(s0, KV_TILE) for s0 in range(0, n_kv, KV_TILE)])

    @pl.when(i >= n_lat_tiles)
    def _():
        attend([(n_lat, n_ctx)])


def _mla_attention(q, k, vt, *, n_lat, n_ctx, n_lat_tiles):
    rows = q.shape[0]
    n_kv = n_lat + n_ctx
    tm = ROW_TILE
    kern = functools.partial(_flash_kernel, n_lat_tiles=n_lat_tiles, n_lat=n_lat, n_ctx=n_ctx)
    return pl.pallas_call(
        kern,
        grid=(MLA_HEADS, rows // tm),
        in_specs=[pl.BlockSpec((tm, QK_PAD), lambda h, i: (i, h)),
                  pl.BlockSpec((n_kv, QK_PAD), lambda h, i: (0, h)),
                  pl.BlockSpec((MLA_DV, n_kv), lambda h, i: (h, 0))],
        out_specs=pl.BlockSpec((tm, MLA_DV), lambda h, i: (i, h)),
        out_shape=jax.ShapeDtypeStruct((rows, MLA_HEADS * MLA_DV), MXU_DTYPE),
        compiler_params=_cparams(2),
        name="mla_flash",
    )(q, k, vt)


def _prep_ffn(w_in, w_out):
    d, two_f = w_in.shape
    f = two_f // 2
    f_pad = -(-f // FF_CHUNK) * FF_CHUNK
    n = f_pad // FF_CHUNK
    gate = jnp.pad(w_in[:, :f], ((0, 0), (0, f_pad - f))).reshape(d, n, 1, FF_CHUNK)
    up = jnp.pad(w_in[:, f:], ((0, 0), (0, f_pad - f))).reshape(d, n, 1, FF_CHUNK)
    w_in_p = _mx(jnp.concatenate([gate, up], axis=2).reshape(d, 2 * f_pad))
    w_out_p = _mx(jnp.pad(w_out, ((0, f_pad - f), (0, 0))))
    return w_in_p, w_out_p


def _prep_even(w_in, wg_f, bg_f, wg_b, bg_b):
    gqk, gv, gr = w_in[:, 0:1024], w_in[:, 1024:2048], w_in[:, 2048:3072]
    gd = w_in[:, 3072:3104]
    rqk, rv, rg = w_in[:, 3104:4128], w_in[:, 4128:5152], w_in[:, 5152:6176]
    d = w_in.shape[0]
    w = jnp.concatenate([gqk, gv, gr, rqk, rv, rg, gd, jnp.zeros((d, 256 - 32), w_in.dtype)],
                        axis=1)
    r = GLA_GATE_RANK
    hq = GLA_HEADS * GLA_DK
    wg = jnp.zeros((2, LANES, hq), F32)
    wg = wg.at[0, 0:r].set(wg_f).at[1, r:2 * r].set(wg_b)
    bg = jnp.stack([bg_f, bg_b]).reshape(2, 1, hq)
    return _mx(w), _mx(wg), bg


def _prep_mla(w_in, w_uq, w_ukv):
    d = w_in.shape[0]
    w_in_p = _mx(jnp.pad(w_in, ((0, 0), (0, 1152 - w_in.shape[1]))))
    r = w_uq.shape[0]
    uq = w_uq.reshape(r, MLA_HEADS, MLA_NOPE + MLA_ROPE)
    uq = jnp.pad(uq, ((0, 0), (0, 0), (0, QK_PAD - MLA_NOPE - MLA_ROPE)))
    w_uq_p = _mx(uq.reshape(r, MLA_HEADS * QK_PAD))
    ukv = w_ukv.reshape(w_ukv.shape[0], MLA_HEADS, MLA_NOPE + MLA_DV)
    w_uk = _mx(ukv[:, :, :MLA_NOPE].reshape(-1, MLA_HEADS * MLA_NOPE))
    w_uvt = _mx(ukv[:, :, MLA_NOPE:].reshape(-1, MLA_HEADS * MLA_DV).T)
    return w_in_p, w_uq_p, w_uk, w_uvt


def _rope_angles(pos, dim):
    inv = ROPE_BASE ** (-jnp.arange(0, dim, 2, dtype=F32) / dim)
    ang = pos.astype(F32)[:, None] * inv[None, :]
    return jnp.concatenate([ang, ang], axis=-1)


def _ret_tables(n_lat, rows):
    ang = _rope_angles(jnp.arange(n_lat), RET_DK)
    cos = jnp.ones((rows, RET_DK), F32).at[:n_lat].set(jnp.cos(ang))
    sin = jnp.sin(ang)
    sign = jnp.where(jnp.arange(RET_DK) < RET_DK // 2, -1.0, 1.0)
    sin_s = jnp.zeros((rows, RET_DK), F32).at[:n_lat].set(sin * sign)
    return cos, sin_s


def _axial_tables(n_lat, rows):
    n_rows = n_lat // GRID_W
    row = jnp.repeat(jnp.arange(n_rows), GRID_W)
    col = jnp.tile(jnp.arange(GRID_W), n_rows)
    half = MLA_ROPE // 2
    ang = jnp.concatenate([_rope_angles(row, half), _rope_angles(col, half)], axis=-1)
    lane = jnp.arange(MLA_ROPE)
    lo = (lane % half) < (half // 2)
    cos = jnp.ones((rows, LANES), F32).at[:n_lat, :MLA_ROPE].set(jnp.cos(ang))
    sin = jnp.sin(ang)
    sin_a = jnp.zeros((rows, LANES), F32).at[:n_lat, :MLA_ROPE].set(jnp.where(lo, -sin, 0.0))
    sin_b = jnp.zeros((rows, LANES), F32).at[:n_lat, :MLA_ROPE].set(jnp.where(lo, 0.0, sin))
    return cos, sin_a, sin_b


def kernel(x, c, ctx, c_ctx, w_ada, b_ada, ln_g, ln_b, w_ffn_in, w_ffn_out, ev_w_in, ev_gla_wg_f, ev_gla_bg_f, ev_gla_wg_b, ev_gla_bg_b, ev_gla_norm, ev_w_out, od_w_in, od_q_norm, od_kv_norm, od_w_uq, od_w_ukv, od_w_out):
    n_lat, d = x.shape[1], x.shape[2]
    n_ctx = ctx.shape[1]
    depth = w_ada.shape[0]
    assert x.shape[0] == 1 and n_lat % ROW_TILE == 0 and n_ctx == SCAN_ROWS
    assert (n_lat + n_ctx) % KV_TILE == 0
    alpha = (2 * depth) ** 0.25
    n_lat_tiles = n_lat // ROW_TILE
    n_tiles = n_lat_tiles + 1
    rows = n_tiles * ROW_TILE
    n_seq_blocks = (n_lat + n_ctx) // SCAN_ROWS

    x_all = jnp.concatenate(
        [x[0], ctx[0], jnp.zeros((rows - n_lat - n_ctx, d), x.dtype)], axis=0)
    c_rows = jnp.zeros((16, d), F32).at[0].set(c[0]).at[1].set(c_ctx)
    mods = _ada(c_rows, w_ada, b_ada)[:, :2].reshape(depth, 2, N_MOD, d)

    ret_cos, ret_sin = _ret_tables(n_lat, rows)
    ax_cos, ax_sina, ax_sinb = _axial_tables(n_lat, rows)

    for l in range(depth):
        last = l == depth - 1
        wi, wo = _prep_ffn(w_ffn_in[l, 0], w_ffn_out[l, 0])
        x_all, h_mix = _ffn(x_all, mods[l], wi, wo, ln_g[l, 0], ln_b[l, 0], mod_row=0,
                            emit_row=3, alpha=alpha, n_lat_tiles=n_lat_tiles, n_tiles=n_tiles)
        if l % 2 == 0:
            e = l // 2
            w_p, wg, bg = _prep_even(ev_w_in[e], ev_gla_wg_f[e], ev_gla_bg_f[e],
                                     ev_gla_wg_b[e], ev_gla_bg_b[e])
            z = _proj(h_mix, w_p, 1280)
            o_f, o_b = _even_scan(z, ret_cos, ret_sin, wg, bg, n_seq_blocks)
            a = _even_gate(o_f, o_b, z, ev_gla_norm[e])
            w_out = _mx(ev_w_out[e])
        else:
            o = l // 2
            w_in_p, w_uq_p, w_uk, w_uvt = _prep_mla(od_w_in[o], od_w_uq[o], od_w_ukv[o])
            q, k, vt = _mla_proj(h_mix, w_in_p, od_q_norm[o], od_kv_norm[o], w_uq_p, w_uk,
                                 w_uvt, ax_cos, ax_sina, ax_sinb)
            a = _mla_attention(q, k, vt, n_lat=n_lat, n_ctx=n_ctx, n_lat_tiles=n_lat_tiles)
            w_out = _mx(od_w_out[o])
        x_all = _outproj_ln(a, w_out, x_all, mods[l], ln_g[l, 1], ln_b[l, 1], gate_row=5,
                            alpha=alpha, n_lat_tiles=n_lat_tiles)
        wi, wo = _prep_ffn(w_ffn_in[l, 1], w_ffn_out[l, 1])
        x_all, _ = _ffn(x_all, mods[l], wi, wo, ln_g[l, 2], ln_b[l, 2], mod_row=6,
                        emit_row=None, alpha=alpha, n_lat_tiles=n_lat_tiles,
                        n_tiles=n_lat_tiles if last else n_tiles)
    return x_all[None]
```

```python
import functools
import math

import numpy as np
import jax
import jax.numpy as jnp
from jax import lax
from jax.experimental import pallas as pl
from jax.experimental.pallas import tpu as pltpu

F32 = jnp.float32
MXU_DTYPE = jnp.bfloat16

N_MOD = 9
LN_EPS = 1e-5
RMS_EPS = 1e-6
CHUNK = 64
ROPE_BASE = 10000.0
GRID_W = 64
GLA_HEADS = 4
GLA_DK = 128
GLA_DV = 256
GLA_GATE_RANK = 16
GLA_GATE_NORM = 16.0
RET_HEADS = 4
RET_DK = 128
RET_DV = 256
MLA_HEADS = 16
MLA_Q_RANK = 512
MLA_KV_RANK = 512
MLA_NOPE = 128
MLA_ROPE = 64
MLA_DV = 128
MLA_SCALE = (MLA_NOPE + MLA_ROPE) ** -0.5
LOG2E = 1.4426950408889634
FIXED_SHIFT_LIMIT = 2.0 ** 100

LANES = 128
ROW_TILE = 512
FF_CHUNK = 512
SCAN_ROWS = 256
MLA_PROJ_ROWS = 256
KV_TILE = 1280
QK_PAD = 256
VMEM_LIMIT = 56 * 1024 * 1024


def _cparams(n_axes, vmem=VMEM_LIMIT):
    return pltpu.CompilerParams(
        dimension_semantics=("arbitrary",) * n_axes, vmem_limit_bytes=vmem)


def _mx(a):
    return a.astype(MXU_DTYPE)


def _dot(a, b):
    return jnp.dot(a, b, preferred_element_type=F32)


def _dot_nt(a, b):
    return lax.dot_general(a, b, (((1,), (1,)), ((), ())), preferred_element_type=F32)


def _dot_tn(a, b):
    return lax.dot_general(a, b, (((0,), (0,)), ((), ())), preferred_element_type=F32)


def _silu(v):
    return v / (1.0 + jnp.exp(-v))


def _layer_norm(v, g, b):
    mu = jnp.mean(v, axis=-1, keepdims=True)
    d = v - mu
    var = jnp.mean(d * d, axis=-1, keepdims=True)
    return d * lax.rsqrt(var + LN_EPS) * g + b


def _ada_kernel(c_ref, w_ref, b_ref, o_ref):
    s = _silu(c_ref[...])
    o_ref[...] = _dot(_mx(s), _mx(w_ref[...])) + b_ref[...]


def _ada(c_rows, w_ada, b_ada):
    depth, d, n = w_ada.shape
    rows = c_rows.shape[0]
    tn = 1024
    return pl.pallas_call(
        _ada_kernel,
        grid=(depth, n // tn),
        in_specs=[
            pl.BlockSpec((rows, d), lambda l, j: (0, 0)),
            pl.BlockSpec((None, d, tn), lambda l, j: (l, 0, j)),
            pl.BlockSpec((None, 1, tn), lambda l, j: (l, 0, j)),
        ],
        out_specs=pl.BlockSpec((None, rows, tn), lambda l, j: (l, 0, j)),
        out_shape=jax.ShapeDtypeStruct((depth, rows, n), F32),
        compiler_params=_cparams(2),
        name="ada_mod",
    )(c_rows, w_ada, b_ada.reshape(depth, 1, n))


def _ffn_kernel(x_ref, mod_ref, wg_ref, wu_ref, wout_ref, g_ref, b_ref, *rest,
                mod_row, emit_row, alpha, overlap):
    if emit_row is None:
        o_ref, xm_scr, acc_scr = rest
        h_ref = None
    else:
        o_ref, h_ref, xm_scr, acc_scr = rest
    k = pl.program_id(1)

    @pl.when(k == 0)
    def _():
        shift = mod_ref[mod_row:mod_row + 1, :]
        scale = mod_ref[mod_row + 1:mod_row + 2, :]
        xm_scr[...] = _mx(x_ref[...] * (1.0 + scale) + shift)
        acc_scr[...] = jnp.zeros_like(acc_scr)

    xm = xm_scr[...]
    act = _silu(_dot(xm, wg_ref[...])) * _dot(xm, wu_ref[...])
    if overlap:
        col = lax.broadcasted_iota(jnp.int32, (1, act.shape[1]), 1)
        act = jnp.where((k < pl.num_programs(1) - 1) | (col >= overlap), act, 0.0)
    acc_scr[...] += _dot(_mx(act), wout_ref[...])

    @pl.when(k == pl.num_programs(1) - 1)
    def _():
        gate = mod_ref[mod_row + 2:mod_row + 3, :]
        v = alpha * x_ref[...] + 0.5 * gate * acc_scr[...]
        y = _layer_norm(v, g_ref[...], b_ref[...])
        o_ref[...] = y
        if h_ref is not None:
            shift = mod_ref[emit_row:emit_row + 1, :]
            scale = mod_ref[emit_row + 1:emit_row + 2, :]
            h_ref[...] = _mx(y * (1.0 + scale) + shift)


def _ffn(x_all, mods, w_in, w_out, ln_g, ln_b, *, layer, half, mod_row, emit_row, alpha,
         n_lat_tiles, n_tiles):
    d = x_all.shape[1]
    f = w_out.shape[2]
    tm, tk = ROW_TILE, FF_CHUNK
    n_chunks = -(-f // tk)
    overlap = n_chunks * tk - f
    rows = n_tiles * tm

    assert f % LANES == 0 and tk % LANES == 0

    def win(k, base=0):
        return (jnp.minimum(k * (tk // LANES), (f - tk) // LANES) + base // LANES) * LANES

    out_shape = [jax.ShapeDtypeStruct((rows, d), F32)]
    out_specs = [pl.BlockSpec((tm, d), lambda i, k: (i, 0))]
    if emit_row is not None:
        out_shape.append(jax.ShapeDtypeStruct((rows, d), MXU_DTYPE))
        out_specs.append(pl.BlockSpec((tm, d), lambda i, k: (i, 0)))
    kern = functools.partial(_ffn_kernel, mod_row=mod_row, emit_row=emit_row,
                             alpha=alpha, overlap=overlap)
    res = pl.pallas_call(
        kern,
        grid=(n_tiles, n_chunks),
        in_specs=[
            pl.BlockSpec((tm, d), lambda i, k: (i, 0)),
            pl.BlockSpec((None, N_MOD, d), lambda i, k: (i // n_lat_tiles, 0, 0)),
            pl.BlockSpec((None, None, pl.Element(d), pl.Element(tk)),
                         lambda i, k: (layer, half, 0, win(k))),
            pl.BlockSpec((None, None, pl.Element(d), pl.Element(tk)),
                         lambda i, k: (layer, half, 0, win(k, f))),
            pl.BlockSpec((None, None, pl.Element(tk), pl.Element(d)),
                         lambda i, k: (layer, half, win(k), 0)),
            pl.BlockSpec((1, d), lambda i, k: (0, 0)),
            pl.BlockSpec((1, d), lambda i, k: (0, 0)),
        ],
        out_specs=out_specs,
        out_shape=out_shape,
        scratch_shapes=[pltpu.VMEM((tm, d), MXU_DTYPE), pltpu.VMEM((tm, d), F32)],
        compiler_params=_cparams(2),
        name="ffn",
    )(x_all, mods, w_in, w_in, w_out, ln_g.reshape(1, d), ln_b.reshape(1, d))
    return res if emit_row is not None else (res[0], None)


def _proj_kernel(h_ref, w_ref, o_ref):
    o_ref[...] = _dot(h_ref[...], w_ref[...])


def _proj(h, w, tn):
    rows, kdim = h.shape
    n = w.shape[1]
    tm = ROW_TILE
    return pl.pallas_call(
        _proj_kernel,
        grid=(n // tn, rows // tm),
        in_specs=[
            pl.BlockSpec((tm, kdim), lambda j, i: (i, 0)),
            pl.BlockSpec((kdim, tn), lambda j, i: (0, j)),
        ],
        out_specs=pl.BlockSpec((tm, tn), lambda j, i: (i, j)),
        out_shape=jax.ShapeDtypeStruct((rows, n), F32),
        compiler_params=_cparams(2),
        name="even_proj",
    )(h, w)


def _outproj_kernel(a_ref, w_ref, x_ref, mod_ref, g_ref, b_ref, o_ref, *, gate_row, alpha):
    y = _dot(a_ref[...], w_ref[...])
    gate = mod_ref[gate_row:gate_row + 1, :]
    v = alpha * x_ref[...] + gate * y
    o_ref[...] = _layer_norm(v, g_ref[...], b_ref[...])


def _outproj_ln(a, w, x_all, mods, ln_g, ln_b, *, gate_row, alpha, n_lat_tiles):
    rows, d = x_all.shape
    kdim = a.shape[1]
    tm = ROW_TILE
    kern = functools.partial(_outproj_kernel, gate_row=gate_row, alpha=alpha)
    return pl.pallas_call(
        kern,
        grid=(rows // tm,),
        in_specs=[
            pl.BlockSpec((tm, kdim), lambda i: (i, 0)),
            pl.BlockSpec((kdim, d), lambda i: (0, 0)),
            pl.BlockSpec((tm, d), lambda i: (i, 0)),
            pl.BlockSpec((None, N_MOD, d), lambda i: (i // n_lat_tiles, 0, 0)),
            pl.BlockSpec((1, d), lambda i: (0, 0)),
            pl.BlockSpec((1, d), lambda i: (0, 0)),
        ],
        out_specs=pl.BlockSpec((tm, d), lambda i: (i, 0)),
        out_shape=jax.ShapeDtypeStruct((rows, d), F32),
        compiler_params=_cparams(1),
        name="mix_out",
    )(a, w, x_all, mods, ln_g.reshape(1, d), ln_b.reshape(1, d))


_GLA_LEVELS = (64, 32, 16)


def _scan_positions(reverse):
    idx = np.arange(CHUNK)
    return idx[::-1].copy() if reverse else idx


def _gla_exponent_matrix(reverse):
    pos = _scan_positions(reverse)
    pi, pt = pos[:, None], pos[None, :]
    mats = [pt <= pi,
            pt > pi]
    for n in _GLA_LEVELS:
        half = n // 2
        ref = (pi // n) * n + half - 1
        q_side = (pi % n) >= half
        mats.append(np.where(q_side, (pt > ref) & (pt <= pi), (pt > pi) & (pt <= ref)))
    return np.concatenate(mats, axis=0).astype(np.float32)


def _ret_constants(reverse):
    scales = np.arange(RET_HEADS, dtype=np.float64)
    off = 5.5 if reverse else 5.0
    log_gamma = np.log1p(-np.exp2(-off - scales))
    idx = np.arange(CHUNK, dtype=np.float64)
    diff = idx[:, None] - idx[None, :]
    dmat = np.where(diff >= 0, np.exp(log_gamma[:, None, None] * np.maximum(diff, 0.0)), 0.0)
    q_dec = np.exp(log_gamma[:, None] * (idx + 1.0))
    k_dec = np.exp(log_gamma[:, None] * (CHUNK - 1.0 - idx))
    s_dec = np.exp(log_gamma * CHUNK)
    if reverse:
        dmat = dmat[:, ::-1, ::-1]
        q_dec = q_dec[:, ::-1]
        k_dec = k_dec[:, ::-1]
    expand = lambda a: np.repeat(a.T, RET_DK, axis=1)
    return (dmat.astype(np.float32), expand(q_dec).astype(np.float32),
            expand(k_dec).astype(np.float32),
            np.repeat(s_dec, RET_DK)[None, :].astype(np.float32))


def _log_sigmoid(v):
    return jnp.minimum(v, 0.0) - jnp.log1p(jnp.exp(-jnp.abs(v)))


def _gla_masks(reverse):
    c = CHUNK
    row = lax.broadcasted_iota(jnp.int32, (c, c), 0)
    col = lax.broadcasted_iota(jnp.int32, (c, c), 1)
    pr, pc = ((c - 1) - row, (c - 1) - col) if reverse else (row, col)
    levels = []
    for n in _GLA_LEVELS:
        sh, half = int(math.log2(n)), n // 2
        levels.append((lax.shift_right_logical(pr, sh) == lax.shift_right_logical(pc, sh))
                      & ((pr & (n - 1)) >= half) & ((pc & (n - 1)) < half))
    sub = lax.broadcasted_iota(jnp.int32, (c, 1), 0) & 7
    if reverse:
        sub = 7 - sub
    diags = [(pr - pc == delta) & ((pr & 7) >= delta) for delta in range(8)]
    return levels, diags, sub


def _gla_chunk(q, k, v, e_all, st_ref, masks, reverse):
    c = CHUNK
    level_masks, diag_masks, sub = masks
    e_cum = e_all[0:c]
    w_all = jnp.exp(e_all)
    w_cum, w_end = w_all[0:c], w_all[c:2 * c]
    s_t = st_ref[...]
    o = _dot_nt(_mx(q * w_cum), _mx(s_t))
    a = jnp.zeros((c, c), F32)
    for li in range(len(_GLA_LEVELS)):
        w_l = w_all[(2 + li) * c:(3 + li) * c]
        a = a + jnp.where(level_masks[li], _dot_nt(_mx(q * w_l), _mx(k * w_l)), 0.0)
    for delta in range(8):
        if delta == 0:
            t = q * k
        else:
            shift = (c - delta) if reverse else delta
            k_s = pltpu.roll(k, shift, 0)
            e_s = pltpu.roll(e_cum, shift, 0)
            t = q * k_s * jnp.exp(jnp.where(sub >= delta, e_cum - e_s, -jnp.inf))
        dsum = jnp.sum(t, axis=-1, keepdims=True)
        a = a + jnp.where(diag_masks[delta], dsum, 0.0)
    o = o + _dot(_mx(a), _mx(v))
    if reverse:
        decay = w_cum[0:1, :]
    else:
        decay = w_cum[c - 1:c, :]
    st_ref[...] = decay * s_t + _dot_tn(_mx(v), _mx(k * w_end))
    return o


def _ret_chunk(q, k, v, dmat, q_dec, k_dec, s_dec, st_ref):
    s_t = st_ref[...]
    a = _dot_nt(_mx(q), _mx(k)) * dmat
    o = _dot(_mx(a), _mx(v)) + _dot_nt(_mx(q * q_dec), _mx(s_t))
    st_ref[...] = s_dec * s_t + _dot_tn(_mx(v), _mx(k * k_dec))
    return o


def _rope128(xh, cos, sin_signed):
    return xh * cos + pltpu.roll(xh, 64, 1) * sin_signed


def _scan_dir(refs, consts, o_ref, sg_scr, sr_scr, d):
    zqk, zv, zrqk, zrv, zgd, cos_ref, sin_ref = refs
    wg_ref, bg_ref, mexp_ref, dmat_ref, qdec_ref, kdec_ref, sdec_ref = consts
    reverse = d == 1
    n_chunks = SCAN_ROWS // CHUNK
    hq = GLA_HEADS * GLA_DK
    pre = _dot(_mx(zgd[...]), wg_ref[d]) + bg_ref[d]
    lg = _log_sigmoid(pre) * (1.0 / GLA_GATE_NORM)
    cos = cos_ref[...]
    sin = sin_ref[...]
    mexp = mexp_ref[d]
    masks = _gla_masks(reverse)
    for cc in range(n_chunks):
        ci = (n_chunks - 1 - cc) if reverse else cc
        r0 = ci * CHUNK
        g = lg[r0:r0 + CHUNK]
        g_hi = _mx(g)
        g_lo = _mx(g - g_hi.astype(F32))
        e = _dot(mexp, g_hi) + _dot(mexp, g_lo)
        for h in range(GLA_HEADS):
            q = zqk[r0:r0 + CHUNK, h * GLA_DK:(h + 1) * GLA_DK] * (GLA_DK ** -0.5)
            k = zqk[r0:r0 + CHUNK, hq + h * GLA_DK:hq + (h + 1) * GLA_DK]
            v = zv[r0:r0 + CHUNK, h * GLA_DV:(h + 1) * GLA_DV]
            o = _gla_chunk(q, k, v, e[:, h * GLA_DK:(h + 1) * GLA_DK], sg_scr.at[d, h], masks, reverse)
            o_ref[r0:r0 + CHUNK, h * GLA_DV:(h + 1) * GLA_DV] = o
        hr = RET_HEADS * RET_DK
        o_off = GLA_HEADS * GLA_DV
        cs = cos[r0:r0 + CHUNK]
        sn = sin[r0:r0 + CHUNK]
        for h in range(RET_HEADS):
            lanes = slice(h * RET_DK, (h + 1) * RET_DK)
            q = _rope128(zrqk[r0:r0 + CHUNK, h * RET_DK:(h + 1) * RET_DK], cs, sn)
            k = _rope128(zrqk[r0:r0 + CHUNK, hr + h * RET_DK:hr + (h + 1) * RET_DK]
                         * (RET_DK ** -0.5), cs, sn)
            v = zrv[r0:r0 + CHUNK, h * RET_DV:(h + 1) * RET_DV]
            o = _ret_chunk(q, k, v, dmat_ref[d, h], qdec_ref[d][:, lanes], kdec_ref[d][:, lanes],
                           sdec_ref[d][:, lanes], sr_scr.at[d, h])
            o_ref[r0:r0 + CHUNK, o_off + h * RET_DV:o_off + (h + 1) * RET_DV] = o


def _scan_kernel(*refs):
    fwd, bwd, consts = refs[0:7], refs[7:14], refs[14:21]
    of_ref, ob_ref, sg_scr, sr_scr = refs[21:]

    @pl.when(pl.program_id(0) == 0)
    def _():
        sg_scr[...] = jnp.zeros_like(sg_scr)
        sr_scr[...] = jnp.zeros_like(sr_scr)

    _scan_dir(fwd, consts, of_ref, sg_scr, sr_scr, 0)
    _scan_dir(bwd, consts, ob_ref, sg_scr, sr_scr, 1)


def _even_scan(z, cos_t, sin_t, wg, bg, n_seq_blocks):
    rows = z.shape[0]
    n_blocks = rows // SCAN_ROWS
    n_lat_blocks = n_seq_blocks - (256 // SCAN_ROWS)
    r = SCAN_ROWS

    def rf(i):
        return jnp.where(i < n_seq_blocks, (i + n_lat_blocks) % n_seq_blocks, i)

    def rb(i):
        return jnp.where(i < n_seq_blocks, n_seq_blocks - 1 - i, i)

    def stream_specs(rmap):
        return [
            pl.BlockSpec((r, 1024), lambda i: (rmap(i), 0)),
            pl.BlockSpec((r, 1024), lambda i: (rmap(i), 1)),
            pl.BlockSpec((r, 1024), lambda i: (rmap(i), 3)),
            pl.BlockSpec((r, 1024), lambda i: (rmap(i), 4)),
            pl.BlockSpec((r, LANES), lambda i: (rmap(i), 48)),
            pl.BlockSpec((r, LANES), lambda i: (rmap(i), 0)),
            pl.BlockSpec((r, LANES), lambda i: (rmap(i), 0)),
        ]

    mexp = jnp.asarray(np.stack([_gla_exponent_matrix(False), _gla_exponent_matrix(True)]),
                       MXU_DTYPE)
    rc = [_ret_constants(False), _ret_constants(True)]
    dmat = jnp.asarray(np.stack([rc[0][0], rc[1][0]]))
    qdec = jnp.asarray(np.stack([rc[0][1], rc[1][1]]))
    kdec = jnp.asarray(np.stack([rc[0][2], rc[1][2]]))
    sdec = jnp.asarray(np.stack([rc[0][3], rc[1][3]]))
    consts = [wg, bg, mexp, dmat, qdec, kdec, sdec]

    def full_spec(a):
        nd = a.ndim
        return pl.BlockSpec(a.shape, lambda i: (0,) * nd)

    d_out = GLA_HEADS * GLA_DV + RET_HEADS * RET_DV
    return pl.pallas_call(
        _scan_kernel,
        grid=(n_blocks,),
        in_specs=stream_specs(rf) + stream_specs(rb) + [full_spec(a) for a in consts],
        out_specs=[pl.BlockSpec((r, d_out), lambda i: (rf(i), 0)),
                   pl.BlockSpec((r, d_out), lambda i: (rb(i), 0))],
        out_shape=[jax.ShapeDtypeStruct((rows, d_out), F32)] * 2,
        scratch_shapes=[pltpu.VMEM((2, GLA_HEADS, GLA_DV, GLA_DK), F32),
                        pltpu.VMEM((2, RET_HEADS, RET_DV, RET_DK), F32)],
        compiler_params=_cparams(1),
        name="even_scan",
    )(z, z, z, z, z, cos_t, sin_t, z, z, z, z, z, cos_t, sin_t, *consts)


def _evgate_kernel(of_ref, ob_ref, gr_ref, rg_ref, nw_ref, a_ref):
    n_heads = GLA_HEADS + RET_HEADS
    for h in range(n_heads):
        lanes = slice(h * GLA_DV, (h + 1) * GLA_DV)
        oh = of_ref[:, lanes] + ob_ref[:, lanes]
        y = oh * lax.rsqrt(jnp.mean(oh * oh, axis=-1, keepdims=True) + RMS_EPS)
        if h < GLA_HEADS:
            y = y * nw_ref[...]
            gate = gr_ref[:, h * GLA_DV:(h + 1) * GLA_DV]
        else:
            hh = h - GLA_HEADS
            gate = rg_ref[:, hh * RET_DV:(hh + 1) * RET_DV]
        a_ref[:, lanes] = _mx(y * _silu(gate))


def _even_gate(o_f, o_b, z, norm_w):
    rows, d_out = o_f.shape
    tm = ROW_TILE
    return pl.pallas_call(
        _evgate_kernel,
        grid=(rows // tm,),
        in_specs=[
            pl.BlockSpec((tm, d_out), lambda i: (i, 0)),
            pl.BlockSpec((tm, d_out), lambda i: (i, 0)),
            pl.BlockSpec((tm, 1024), lambda i: (i, 2)),
            pl.BlockSpec((tm, 1024), lambda i: (i, 5)),
            pl.BlockSpec((1, GLA_DV), lambda i: (0, 0)),
        ],
        out_specs=pl.BlockSpec((tm, d_out), lambda i: (i, 0)),
        out_shape=jax.ShapeDtypeStruct((rows, d_out), MXU_DTYPE),
        compiler_params=_cparams(1),
        name="even_gate",
    )(o_f, o_b, z, z, norm_w.reshape(1, GLA_DV))


def _axial128(xh, cos, sin_a, sin_b):
    return xh * cos + pltpu.roll(xh, LANES - 16, 1) * sin_a + pltpu.roll(xh, 16, 1) * sin_b


def _mlaproj_kernel(h_ref, win_ref, qn_ref, kvn_ref, wuq_ref, wuk_ref, wuvt_ref,
                    cos_ref, sina_ref, sinb_ref, q_ref, k_ref, vt_ref):
    z = _dot(h_ref[...], win_ref[...])
    cq = z[:, :MLA_Q_RANK]
    ckv = z[:, MLA_Q_RANK:MLA_Q_RANK + MLA_KV_RANK]
    kr = z[:, MLA_Q_RANK + MLA_KV_RANK:]
    cos, sin_a, sin_b = cos_ref[...], sina_ref[...], sinb_ref[...]
    cqn = _mx(cq * lax.rsqrt(jnp.mean(cq * cq, axis=-1, keepdims=True) + RMS_EPS) * qn_ref[...])
    ckvn = _mx(ckv * lax.rsqrt(jnp.mean(ckv * ckv, axis=-1, keepdims=True) + RMS_EPS)
               * kvn_ref[...])
    q = _dot(cqn, wuq_ref[...])
    kn = _dot(ckvn, wuk_ref[...])
    vt_ref[...] = _mx(_dot_nt(wuvt_ref[...], ckvn))
    kr_rot = _mx(_axial128(kr, cos, sin_a, sin_b))
    qs = MLA_SCALE * LOG2E
    for h in range(MLA_HEADS):
        b0 = h * QK_PAD
        q_ref[:, b0:b0 + LANES] = _mx(q[:, b0:b0 + LANES] * qs)
        q_ref[:, b0 + LANES:b0 + QK_PAD] = _mx(
            _axial128(q[:, b0 + LANES:b0 + QK_PAD], cos, sin_a, sin_b) * qs)
        k_ref[:, b0:b0 + LANES] = _mx(kn[:, h * MLA_NOPE:(h + 1) * MLA_NOPE])
        k_ref[:, b0 + LANES:b0 + QK_PAD] = kr_rot


def _mla_proj(h, w_in_p, q_norm, kv_norm, w_uq_p, w_uk, w_uvt, cos_t, sina_t, sinb_t):
    rows, d = h.shape
    tm = MLA_PROJ_ROWS
    hq = MLA_HEADS * QK_PAD
    hv = MLA_HEADS * MLA_DV

    def const2(a):
        return pl.BlockSpec(a.shape, lambda i: (0, 0))

    qn = q_norm.reshape(1, -1)
    kvn = kv_norm.reshape(1, -1)
    return pl.pallas_call(
        _mlaproj_kernel,
        grid=(rows // tm,),
        in_specs=[pl.BlockSpec((tm, d), lambda i: (i, 0)),
                  const2(w_in_p), const2(qn), const2(kvn), const2(w_uq_p), const2(w_uk),
                  const2(w_uvt),
                  pl.BlockSpec((tm, LANES), lambda i: (i, 0)),
                  pl.BlockSpec((tm, LANES), lambda i: (i, 0)),
                  pl.BlockSpec((tm, LANES), lambda i: (i, 0))],
        out_specs=[pl.BlockSpec((tm, hq), lambda i: (i, 0)),
                   pl.BlockSpec((tm, hq), lambda i: (i, 0)),
                   pl.BlockSpec((hv, tm), lambda i: (0, i))],
        out_shape=[jax.ShapeDtypeStruct((rows, hq), MXU_DTYPE),
                   jax.ShapeDtypeStruct((rows, hq), MXU_DTYPE),
                   jax.ShapeDtypeStruct((hv, rows), MXU_DTYPE)],
        compiler_params=_cparams(1),
        name="mla_proj",
    )(h, w_in_p, qn, kvn, w_uq_p, w_uk, w_uvt, cos_t, sina_t, sinb_t)


def _flash_kernel(q_ref, k_ref, vt_ref, o_ref, *, n_lat_tiles, n_lat, n_ctx):
    tq = q_ref.shape[0]
    q = q_ref[...]

    def scores(start, size):
        return _dot_nt(k_ref[start:start + size, :], q)

    def store(acc, l):
        o_ref[...] = (acc / l).T.astype(o_ref.dtype)

    def attend_online(kv_slices):
        m = jnp.full((1, tq), -jnp.inf, F32)
        l = jnp.zeros((1, tq), F32)
        acc = jnp.zeros((MLA_DV, tq), F32)
        for start, size in kv_slices:
            s = scores(start, size)
            m_new = jnp.maximum(m, jnp.max(s, axis=0, keepdims=True))
            alpha = jnp.exp2(m - m_new)
            p = jnp.exp2(s - m_new)
            l = alpha * l + jnp.sum(p, axis=0, keepdims=True)
            acc = alpha * acc + _dot(vt_ref[:, start:start + size], _mx(p))
            m = m_new
        store(acc, l)

    def attend_fixed_shift(kv_slices):
        start, size = kv_slices[0]
        s = scores(start, size)
        m = jnp.max(s, axis=0, keepdims=True)
        p = jnp.exp2(s - m)
        l = jnp.sum(p, axis=0, keepdims=True)
        acc = _dot(vt_ref[:, start:start + size], _mx(p))
        for start, size in kv_slices[1:]:
            p = jnp.exp2(scores(start, size) - m)
            l = l + jnp.sum(p, axis=0, keepdims=True)
            acc = acc + _dot(vt_ref[:, start:start + size], _mx(p))
        store(acc, l)
        n_bad = (jnp.sum(jnp.where(l < FIXED_SHIFT_LIMIT, 0.0, 1.0))
                 + jnp.sum(jnp.where(jnp.abs(acc) < FIXED_SHIFT_LIMIT, 0.0, 1.0)))
        return n_bad == 0.0

    n_kv = n_lat + n_ctx
    i = pl.program_id(1)
    lat_slices = [(s0, KV_TILE) for s0 in range(0, n_kv, KV_TILE)]

    @pl.when(i < n_lat_tiles)
    def _():
        ok = attend_fixed_shift(lat_slices)

        @pl.when(jnp.logical_not(ok))
        def _():
            attend_online(lat_slices)

    @pl.when(i >= n_lat_tiles)
    def _():
        attend_online([(n_lat, n_ctx)])


def _mla_attention(q, k, vt, *, n_lat, n_ctx, n_lat_tiles):
    rows = q.shape[0]
    n_kv = n_lat + n_ctx
    tm = ROW_TILE
    kern = functools.partial(_flash_kernel, n_lat_tiles=n_lat_tiles, n_lat=n_lat, n_ctx=n_ctx)
    return pl.pallas_call(
        kern,
        grid=(MLA_HEADS, rows // tm),
        in_specs=[pl.BlockSpec((tm, QK_PAD), lambda h, i: (i, h)),
                  pl.BlockSpec((n_kv, QK_PAD), lambda h, i: (0, h)),
                  pl.BlockSpec((MLA_DV, n_kv), lambda h, i: (h, 0))],
        out_specs=pl.BlockSpec((tm, MLA_DV), lambda h, i: (i, h)),
        out_shape=jax.ShapeDtypeStruct((rows, MLA_HEADS * MLA_DV), MXU_DTYPE),
        compiler_params=_cparams(2),
        name="mla_flash",
    )(q, k, vt)


def _prep_even(w_in, wg_f, bg_f, wg_b, bg_b):
    gqk, gv, gr = w_in[:, 0:1024], w_in[:, 1024:2048], w_in[:, 2048:3072]
    gd = w_in[:, 3072:3104]
    rqk, rv, rg = w_in[:, 3104:4128], w_in[:, 4128:5152], w_in[:, 5152:6176]
    d = w_in.shape[0]
    w = jnp.concatenate([gqk, gv, gr, rqk, rv, rg, gd, jnp.zeros((d, 256 - 32), w_in.dtype)],
                        axis=1)
    r = GLA_GATE_RANK
    hq = GLA_HEADS * GLA_DK
    wg = jnp.zeros((2, LANES, hq), F32)
    wg = wg.at[0, 0:r].set(wg_f).at[1, r:2 * r].set(wg_b)
    bg = jnp.stack([bg_f, bg_b]).reshape(2, 1, hq)
    return _mx(w), _mx(wg), bg


def _prep_mla(w_in, w_uq, w_ukv):
    d = w_in.shape[0]
    w_in_p = _mx(jnp.pad(w_in, ((0, 0), (0, 1152 - w_in.shape[1]))))
    r = w_uq.shape[0]
    uq = w_uq.reshape(r, MLA_HEADS, MLA_NOPE + MLA_ROPE)
    uq = jnp.pad(uq, ((0, 0), (0, 0), (0, QK_PAD - MLA_NOPE - MLA_ROPE)))
    w_uq_p = _mx(uq.reshape(r, MLA_HEADS * QK_PAD))
    ukv = w_ukv.reshape(w_ukv.shape[0], MLA_HEADS, MLA_NOPE + MLA_DV)
    w_uk = _mx(ukv[:, :, :MLA_NOPE].reshape(-1, MLA_HEADS * MLA_NOPE))
    w_uvt = _mx(ukv[:, :, MLA_NOPE:].reshape(-1, MLA_HEADS * MLA_DV).T)
    return w_in_p, w_uq_p, w_uk, w_uvt


def _rope_angles(pos, dim):
    inv = ROPE_BASE ** (-jnp.arange(0, dim, 2, dtype=F32) / dim)
    ang = pos.astype(F32)[:, None] * inv[None, :]
    return jnp.concatenate([ang, ang], axis=-1)


def _ret_tables(n_lat, rows):
    ang = _rope_angles(jnp.arange(n_lat), RET_DK)
    cos = jnp.ones((rows, RET_DK), F32).at[:n_lat].set(jnp.cos(ang))
    sin = jnp.sin(ang)
    sign = jnp.where(jnp.arange(RET_DK) < RET_DK // 2, -1.0, 1.0)
    sin_s = jnp.zeros((rows, RET_DK), F32).at[:n_lat].set(sin * sign)
    return cos, sin_s


def _axial_tables(n_lat, rows):
    n_rows = n_lat // GRID_W
    row = jnp.repeat(jnp.arange(n_rows), GRID_W)
    col = jnp.tile(jnp.arange(GRID_W), n_rows)
    half = MLA_ROPE // 2
    ang = jnp.concatenate([_rope_angles(row, half), _rope_angles(col, half)], axis=-1)
    lane = jnp.arange(MLA_ROPE)
    lo = (lane % half) < (half // 2)
    cos = jnp.ones((rows, LANES), F32).at[:n_lat, :MLA_ROPE].set(jnp.cos(ang))
    sin = jnp.sin(ang)
    sin_a = jnp.zeros((rows, LANES), F32).at[:n_lat, :MLA_ROPE].set(jnp.where(lo, -sin, 0.0))
    sin_b = jnp.zeros((rows, LANES), F32).at[:n_lat, :MLA_ROPE].set(jnp.where(lo, 0.0, sin))
    return cos, sin_a, sin_b


def kernel(x, c, ctx, c_ctx, w_ada, b_ada, ln_g, ln_b, w_ffn_in, w_ffn_out, ev_w_in, ev_gla_wg_f, ev_gla_bg_f, ev_gla_wg_b, ev_gla_bg_b, ev_gla_norm, ev_w_out, od_w_in, od_q_norm, od_kv_norm, od_w_uq, od_w_ukv, od_w_out):
    n_lat, d = x.shape[1], x.shape[2]
    n_ctx = ctx.shape[1]
    depth = w_ada.shape[0]
    assert x.shape[0] == 1 and n_lat % ROW_TILE == 0 and n_ctx == SCAN_ROWS
    assert (n_lat + n_ctx) % KV_TILE == 0
    alpha = (2 * depth) ** 0.25
    n_lat_tiles = n_lat // ROW_TILE
    n_tiles = n_lat_tiles + 1
    rows = n_tiles * ROW_TILE
    n_seq_blocks = (n_lat + n_ctx) // SCAN_ROWS

    x_all = jnp.concatenate(
        [x[0], ctx[0], jnp.zeros((rows - n_lat - n_ctx, d), x.dtype)], axis=0)
    c_rows = jnp.zeros((16, d), F32).at[0].set(c[0]).at[1].set(c_ctx)
    mods = _ada(c_rows, w_ada, b_ada)[:, :2].reshape(depth, 2, N_MOD, d)

    ret_cos, ret_sin = _ret_tables(n_lat, rows)
    ax_cos, ax_sina, ax_sinb = _axial_tables(n_lat, rows)

    wi, wo = _mx(w_ffn_in), _mx(w_ffn_out)
    for l in range(depth):
        last = l == depth - 1
        x_all, h_mix = _ffn(x_all, mods[l], wi, wo, ln_g[l, 0], ln_b[l, 0], layer=l, half=0,
                            mod_row=0, emit_row=3, alpha=alpha, n_lat_tiles=n_lat_tiles,
                            n_tiles=n_tiles)
        if l % 2 == 0:
            e = l // 2
            w_p, wg, bg = _prep_even(ev_w_in[e], ev_gla_wg_f[e], ev_gla_bg_f[e],
                                     ev_gla_wg_b[e], ev_gla_bg_b[e])
            z = _proj(h_mix, w_p, 1280)
            o_f, o_b = _even_scan(z, ret_cos, ret_sin, wg, bg, n_seq_blocks)
            a = _even_gate(o_f, o_b, z, ev_gla_norm[e])
            w_out = _mx(ev_w_out[e])
        else:
            o = l // 2
            w_in_p, w_uq_p, w_uk, w_uvt = _prep_mla(od_w_in[o], od_w_uq[o], od_w_ukv[o])
            q, k, vt = _mla_proj(h_mix, w_in_p, od_q_norm[o], od_kv_norm[o], w_uq_p, w_uk,
                                 w_uvt, ax_cos, ax_sina, ax_sinb)
            a = _mla_attention(q, k, vt, n_lat=n_lat, n_ctx=n_ctx, n_lat_tiles=n_lat_tiles)
            w_out = _mx(od_w_out[o])
        x_all = _outproj_ln(a, w_out, x_all, mods[l], ln_g[l, 1], ln_b[l, 1], gate_row=5,
                            alpha=alpha, n_lat_tiles=n_lat_tiles)
        x_all, _ = _ffn(x_all, mods[l], wi, wo, ln_g[l, 2], ln_b[l, 2], layer=l, half=1,
                        mod_row=6, emit_row=None, alpha=alpha, n_lat_tiles=n_lat_tiles,
                        n_tiles=n_lat_tiles if last else n_tiles)
    return x_all[None]
```

```python
import functools
import math

import numpy as np
import jax
import jax.numpy as jnp
from jax import lax
from jax.experimental import pallas as pl
from jax.experimental.pallas import tpu as pltpu

F32 = jnp.float32
MXU_DTYPE = jnp.bfloat16

N_MOD = 9
LN_EPS = 1e-5
RMS_EPS = 1e-6
CHUNK = 64
ROPE_BASE = 10000.0
GRID_W = 64
GLA_HEADS = 4
GLA_DK = 128
GLA_DV = 256
GLA_GATE_RANK = 16
GLA_GATE_NORM = 16.0
RET_HEADS = 4
RET_DK = 128
RET_DV = 256
MLA_HEADS = 16
MLA_Q_RANK = 512
MLA_KV_RANK = 512
MLA_NOPE = 128
MLA_ROPE = 64
MLA_DV = 128
MLA_SCALE = (MLA_NOPE + MLA_ROPE) ** -0.5
LOG2E = 1.4426950408889634
FIXED_SHIFT_LIMIT = 2.0 ** 100

LANES = 128
ROW_TILE = 512
FF_CHUNK = 512
LN_SLICE = 64
SCAN_ROWS = 256
MLA_PROJ_ROWS = 256
KV_TILE = 1280
QK_PAD = 256
VMEM_LIMIT = 56 * 1024 * 1024


def _cparams(n_axes, vmem=VMEM_LIMIT):
    return pltpu.CompilerParams(
        dimension_semantics=("arbitrary",) * n_axes, vmem_limit_bytes=vmem)


def _mx(a):
    return a.astype(MXU_DTYPE)


def _dot(a, b):
    return jnp.dot(a, b, preferred_element_type=F32)


def _dot_nt(a, b):
    return lax.dot_general(a, b, (((1,), (1,)), ((), ())), preferred_element_type=F32)


def _dot_tn(a, b):
    return lax.dot_general(a, b, (((0,), (0,)), ((), ())), preferred_element_type=F32)


def _silu(v):
    return v / (1.0 + jnp.exp(-v))


def _layer_norm(v, g, b):
    mu = jnp.mean(v, axis=-1, keepdims=True)
    d = v - mu
    var = jnp.mean(d * d, axis=-1, keepdims=True)
    return d * lax.rsqrt(var + LN_EPS) * g + b


def _ada_kernel(c_ref, w_ref, b_ref, o_ref):
    s = _silu(c_ref[...])
    o_ref[...] = _dot(_mx(s), _mx(w_ref[...])) + b_ref[...]


def _ada(c_rows, w_ada, b_ada):
    depth, d, n = w_ada.shape
    rows = c_rows.shape[0]
    tn = 1024
    return pl.pallas_call(
        _ada_kernel,
        grid=(depth, n // tn),
        in_specs=[
            pl.BlockSpec((rows, d), lambda l, j: (0, 0)),
            pl.BlockSpec((None, d, tn), lambda l, j: (l, 0, j)),
            pl.BlockSpec((None, 1, tn), lambda l, j: (l, 0, j)),
        ],
        out_specs=pl.BlockSpec((None, rows, tn), lambda l, j: (l, 0, j)),
        out_shape=jax.ShapeDtypeStruct((depth, rows, n), F32),
        compiler_params=_cparams(2),
        name="ada_mod",
    )(c_rows, w_ada, b_ada.reshape(depth, 1, n))


def _ffn_kernel(x_ref, mod_ref, wg_ref, wu_ref, wout_ref, g_ref, b_ref, *rest,
                mod_row, emit_row, alpha, overlap, n_tiles):
    if emit_row is None:
        o_ref, xm_scr, acc_scr, v_scr, em_scr = rest
        h_ref = None
    else:
        o_ref, h_ref, xm_scr, acc_scr, v_scr, em_scr = rest
    i, k = pl.program_id(0), pl.program_id(1)
    last_k = pl.num_programs(1) - 1
    n_slices = v_scr.shape[0] // LN_SLICE

    def finish_rows():
        r0 = pl.multiple_of(jnp.minimum(k, n_slices - 1) * LN_SLICE, LN_SLICE)
        rows = pl.ds(r0, LN_SLICE)
        y = _layer_norm(v_scr[rows, :], g_ref[...], b_ref[...])
        o_ref[rows, :] = y
        if h_ref is not None:
            h_ref[rows, :] = _mx(y * (1.0 + em_scr[1:2, :]) + em_scr[0:1, :])

    @pl.when((i == 0) & (k == 0))
    def _():
        v_scr[...] = jnp.zeros_like(v_scr)
        em_scr[...] = jnp.zeros_like(em_scr)

    @pl.when(i < n_tiles)
    def _():
        @pl.when(k == 0)
        def _():
            shift = mod_ref[mod_row:mod_row + 1, :]
            scale = mod_ref[mod_row + 1:mod_row + 2, :]
            xm_scr[...] = _mx(x_ref[...] * (1.0 + scale) + shift)
            acc_scr[...] = jnp.zeros_like(acc_scr)

        xm = xm_scr[...]
        act = _silu(_dot(xm, wg_ref[...])) * _dot(xm, wu_ref[...])
        if overlap:
            col = lax.broadcasted_iota(jnp.int32, (1, act.shape[1]), 1)
            act = jnp.where((k < last_k) | (col >= overlap), act, 0.0)
        acc_scr[...] += _dot(_mx(act), wout_ref[...])
        finish_rows()

        @pl.when(k == last_k)
        def _():
            gate = mod_ref[mod_row + 2:mod_row + 3, :]
            v_scr[...] = alpha * x_ref[...] + 0.5 * gate * acc_scr[...]
            if h_ref is not None:
                em_scr[...] = mod_ref[emit_row:emit_row + 2, :]

    @pl.when(i == n_tiles)
    def _():
        finish_rows()


def _ffn(x_all, mods, w_in, w_out, ln_g, ln_b, *, layer, half, mod_row, emit_row, alpha,
         n_lat_tiles, n_tiles):
    d = x_all.shape[1]
    f = w_out.shape[2]
    tm, tk = ROW_TILE, FF_CHUNK
    n_chunks = -(-f // tk)
    overlap = n_chunks * tk - f
    rows = n_tiles * tm

    assert f % LANES == 0 and tk % LANES == 0 and n_chunks >= tm // LN_SLICE

    def win(i, k, base=0):
        k = jnp.where(i < n_tiles, k, n_chunks - 1)
        return (jnp.minimum(k * (tk // LANES), (f - tk) // LANES) + base // LANES) * LANES

    def src(i):
        return jnp.minimum(i, n_tiles - 1)

    def dst(i):
        return jnp.maximum(i - 1, 0)

    out_shape = [jax.ShapeDtypeStruct((rows, d), F32)]
    out_specs = [pl.BlockSpec((tm, d), lambda i, k: (dst(i), 0))]
    if emit_row is not None:
        out_shape.append(jax.ShapeDtypeStruct((rows, d), MXU_DTYPE))
        out_specs.append(pl.BlockSpec((tm, d), lambda i, k: (dst(i), 0)))
    kern = functools.partial(_ffn_kernel, mod_row=mod_row, emit_row=emit_row,
                             alpha=alpha, overlap=overlap, n_tiles=n_tiles)
    res = pl.pallas_call(
        kern,
        grid=(n_tiles + 1, n_chunks),
        in_specs=[
            pl.BlockSpec((tm, d), lambda i, k: (src(i), 0)),
            pl.BlockSpec((None, N_MOD, d), lambda i, k: (src(i) // n_lat_tiles, 0, 0)),
            pl.BlockSpec((None, None, pl.Element(d), pl.Element(tk)),
                         lambda i, k: (layer, half, 0, win(i, k))),
            pl.BlockSpec((None, None, pl.Element(d), pl.Element(tk)),
                         lambda i, k: (layer, half, 0, win(i, k, f))),
            pl.BlockSpec((None, None, pl.Element(tk), pl.Element(d)),
                         lambda i, k: (layer, half, win(i, k), 0)),
            pl.BlockSpec((1, d), lambda i, k: (0, 0)),
            pl.BlockSpec((1, d), lambda i, k: (0, 0)),
        ],
        out_specs=out_specs,
        out_shape=out_shape,
        scratch_shapes=[pltpu.VMEM((tm, d), MXU_DTYPE), pltpu.VMEM((tm, d), F32),
                        pltpu.VMEM((tm, d), F32), pltpu.VMEM((2, d), F32)],
        compiler_params=_cparams(2),
        name="ffn",
    )(x_all, mods, w_in, w_in, w_out, ln_g.reshape(1, d), ln_b.reshape(1, d))
    return res if emit_row is not None else (res[0], None)


def _proj_kernel(h_ref, w_ref, o_ref):
    o_ref[...] = _dot(h_ref[...], w_ref[...])


def _proj(h, w, tn):
    rows, kdim = h.shape
    n = w.shape[1]
    tm = ROW_TILE
    return pl.pallas_call(
        _proj_kernel,
        grid=(n // tn, rows // tm),
        in_specs=[
            pl.BlockSpec((tm, kdim), lambda j, i: (i, 0)),
            pl.BlockSpec((kdim, tn), lambda j, i: (0, j)),
        ],
        out_specs=pl.BlockSpec((tm, tn), lambda j, i: (i, j)),
        out_shape=jax.ShapeDtypeStruct((rows, n), F32),
        compiler_params=_cparams(2),
        name="even_proj",
    )(h, w)


def _outproj_kernel(a_ref, w_ref, x_ref, mod_ref, g_ref, b_ref, o_ref, *, gate_row, alpha):
    y = _dot(a_ref[...], w_ref[...])
    gate = mod_ref[gate_row:gate_row + 1, :]
    v = alpha * x_ref[...] + gate * y
    o_ref[...] = _layer_norm(v, g_ref[...], b_ref[...])


def _outproj_ln(a, w, x_all, mods, ln_g, ln_b, *, gate_row, alpha, n_lat_tiles):
    rows, d = x_all.shape
    kdim = a.shape[1]
    tm = ROW_TILE
    kern = functools.partial(_outproj_kernel, gate_row=gate_row, alpha=alpha)
    return pl.pallas_call(
        kern,
        grid=(rows // tm,),
        in_specs=[
            pl.BlockSpec((tm, kdim), lambda i: (i, 0)),
            pl.BlockSpec((kdim, d), lambda i: (0, 0)),
            pl.BlockSpec((tm, d), lambda i: (i, 0)),
            pl.BlockSpec((None, N_MOD, d), lambda i: (i // n_lat_tiles, 0, 0)),
            pl.BlockSpec((1, d), lambda i: (0, 0)),
            pl.BlockSpec((1, d), lambda i: (0, 0)),
        ],
        out_specs=pl.BlockSpec((tm, d), lambda i: (i, 0)),
        out_shape=jax.ShapeDtypeStruct((rows, d), F32),
        compiler_params=_cparams(1),
        name="mix_out",
    )(a, w, x_all, mods, ln_g.reshape(1, d), ln_b.reshape(1, d))


_GLA_LEVELS = (64, 32, 16, 8, 4, 2)


def _scan_positions(reverse):
    idx = np.arange(CHUNK)
    return idx[::-1].copy() if reverse else idx


def _gla_exponent_matrix(reverse):
    pos = _scan_positions(reverse)
    pi, pt = pos[:, None], pos[None, :]
    mats = [pt <= pi,
            pt > pi]
    for n in _GLA_LEVELS:
        half = n // 2
        ref = (pi // n) * n + half - 1
        q_side = (pi % n) >= half
        mats.append(np.where(q_side, (pt > ref) & (pt <= pi), (pt > pi) & (pt <= ref)))
    return np.concatenate(mats, axis=0).astype(np.float32)


def _ret_constants(reverse):
    scales = np.arange(RET_HEADS, dtype=np.float64)
    off = 5.5 if reverse else 5.0
    log_gamma = np.log1p(-np.exp2(-off - scales))
    idx = np.arange(CHUNK, dtype=np.float64)
    diff = idx[:, None] - idx[None, :]
    dmat = np.where(diff >= 0, np.exp(log_gamma[:, None, None] * np.maximum(diff, 0.0)), 0.0)
    q_dec = np.exp(log_gamma[:, None] * (idx + 1.0))
    k_dec = np.exp(log_gamma[:, None] * (CHUNK - 1.0 - idx))
    s_dec = np.exp(log_gamma * CHUNK)
    if reverse:
        dmat = dmat[:, ::-1, ::-1]
        q_dec = q_dec[:, ::-1]
        k_dec = k_dec[:, ::-1]
    expand = lambda a: np.repeat(a.T, RET_DK, axis=1)
    return (dmat.astype(np.float32), expand(q_dec).astype(np.float32),
            expand(k_dec).astype(np.float32),
            np.repeat(s_dec, RET_DK)[None, :].astype(np.float32))


def _log_sigmoid(v):
    return jnp.minimum(v, 0.0) - jnp.log1p(jnp.exp(-jnp.abs(v)))


def _gla_masks(reverse):
    c = CHUNK
    row = lax.broadcasted_iota(jnp.int32, (c, c), 0)
    col = lax.broadcasted_iota(jnp.int32, (c, c), 1)
    pr, pc = ((c - 1) - row, (c - 1) - col) if reverse else (row, col)
    levels = []
    for n in _GLA_LEVELS:
        sh, half = int(math.log2(n)), n // 2
        levels.append((lax.shift_right_logical(pr, sh) == lax.shift_right_logical(pc, sh))
                      & ((pr & (n - 1)) >= half) & ((pc & (n - 1)) < half))
    return levels, row == col


def _rope128(xh, cos, sin_signed):
    return xh * cos + pltpu.roll(xh, 64, 1) * sin_signed


def _scan_chunk_step(streams, consts, out_refs, sg_scr, sr_scr, lgs, masks, cc):
    wg_ref, bg_ref, mexp_ref, dmat_ref, qdec_ref, kdec_ref, sdec_ref = consts
    c = CHUNK
    n_chunks = SCAN_ROWS // CHUNK
    hq = GLA_HEADS * GLA_DK
    hr = RET_HEADS * RET_DK
    o_off = GLA_HEADS * GLA_DV
    tasks = []
    for d in (0, 1):
        zqk, zv, zrqk, zrv, _, cos_ref, sin_ref = streams[d]
        reverse = d == 1
        r0 = ((n_chunks - 1 - cc) if reverse else cc) * c
        rows = slice(r0, r0 + c)
        g = lgs[d][rows]
        g_hi = _mx(g)
        g_lo = _mx(g - g_hi.astype(F32))
        mexp = mexp_ref[d]
        w_all = jnp.exp(_dot(mexp, g_hi) + _dot(mexp, g_lo))
        cs, sn = cos_ref[rows, :], sin_ref[rows, :]
        for h in range(GLA_HEADS):
            lanes = slice(h * GLA_DK, (h + 1) * GLA_DK)
            tasks.append(dict(
                kind="gla", d=d, rows=rows, w=w_all[:, lanes],
                q=zqk[rows, lanes] * (GLA_DK ** -0.5),
                k=zqk[rows, hq + h * GLA_DK:hq + (h + 1) * GLA_DK],
                v=zv[rows, h * GLA_DV:(h + 1) * GLA_DV],
                st=sg_scr.at[d, h], cols=slice(h * GLA_DV, (h + 1) * GLA_DV)))
        for h in range(RET_HEADS):
            lanes = slice(h * RET_DK, (h + 1) * RET_DK)
            tasks.append(dict(
                kind="ret", d=d, rows=rows,
                q=_rope128(zrqk[rows, lanes], cs, sn),
                k=_rope128(zrqk[rows, hr + h * RET_DK:hr + (h + 1) * RET_DK]
                           * (RET_DK ** -0.5), cs, sn),
                v=zrv[rows, h * RET_DV:(h + 1) * RET_DV],
                dmat=dmat_ref[d, h], q_dec=qdec_ref[d][:, lanes], k_dec=kdec_ref[d][:, lanes],
                s_dec=sdec_ref[d][:, lanes], st=sr_scr.at[d, h],
                cols=slice(o_off + h * RET_DV, o_off + (h + 1) * RET_DV)))
    for t in tasks:
        t["s_t"] = t["st"][...]
        if t["kind"] == "gla":
            w, q, k = t["w"], t["q"], t["k"]
            t["inter"] = _dot_nt(_mx(q * w[0:c]), _mx(t["s_t"]))
            t["lvl"] = [_dot_nt(_mx(q * w[(2 + li) * c:(3 + li) * c]),
                                _mx(k * w[(2 + li) * c:(3 + li) * c]))
                        for li in range(len(_GLA_LEVELS))]
        else:
            t["inter"] = _dot_nt(_mx(t["q"] * t["q_dec"]), _mx(t["s_t"]))
            t["qk"] = _dot_nt(_mx(t["q"]), _mx(t["k"]))
    for t in tasks:
        if t["kind"] == "gla":
            level_masks, diag_mask = masks[t["d"]]
            a = jnp.where(diag_mask, jnp.sum(t["q"] * t["k"], axis=-1, keepdims=True), 0.0)
            for li in range(len(_GLA_LEVELS)):
                a = a + jnp.where(level_masks[li], t["lvl"][li], 0.0)
        else:
            a = t["qk"] * t["dmat"]
        t["a"] = _mx(a)
    for t in tasks:
        v = _mx(t["v"])
        out_refs[t["d"]][t["rows"], t["cols"]] = t["inter"] + _dot(t["a"], v)
        if t["kind"] == "gla":
            w = t["w"]
            decay = w[0:1, :] if t["d"] == 1 else w[c - 1:c, :]
            k_dec = t["k"] * w[c:2 * c]
        else:
            decay = t["s_dec"]
            k_dec = t["k"] * t["k_dec"]
        t["st"][...] = decay * t["s_t"] + _dot_tn(v, _mx(k_dec))


def _scan_kernel(*refs):
    streams, consts = (refs[0:7], refs[7:14]), refs[14:21]
    of_ref, ob_ref, sg_scr, sr_scr = refs[21:]
    wg_ref, bg_ref = consts[0], consts[1]

    @pl.when(pl.program_id(0) == 0)
    def _():
        sg_scr[...] = jnp.zeros_like(sg_scr)
        sr_scr[...] = jnp.zeros_like(sr_scr)

    lgs = [_log_sigmoid(_dot(_mx(streams[d][4][...]), wg_ref[d]) + bg_ref[d])
           * (1.0 / GLA_GATE_NORM) for d in (0, 1)]
    masks = [_gla_masks(False), _gla_masks(True)]
    for cc in range(SCAN_ROWS // CHUNK):
        _scan_chunk_step(streams, consts, (of_ref, ob_ref), sg_scr, sr_scr, lgs, masks, cc)


def _even_scan(z, cos_t, sin_t, wg, bg, n_seq_blocks):
    rows = z.shape[0]
    n_blocks = rows // SCAN_ROWS
    n_lat_blocks = n_seq_blocks - (256 // SCAN_ROWS)
    r = SCAN_ROWS

    def rf(i):
        return jnp.where(i < n_seq_blocks, (i + n_lat_blocks) % n_seq_blocks, i)

    def rb(i):
        return jnp.where(i < n_seq_blocks, n_seq_blocks - 1 - i, i)

    def stream_specs(rmap):
        return [
            pl.BlockSpec((r, 1024), lambda i: (rmap(i), 0)),
            pl.BlockSpec((r, 1024), lambda i: (rmap(i), 1)),
            pl.BlockSpec((r, 1024), lambda i: (rmap(i), 3)),
            pl.BlockSpec((r, 1024), lambda i: (rmap(i), 4)),
            pl.BlockSpec((r, LANES), lambda i: (rmap(i), 48)),
            pl.BlockSpec((r, LANES), lambda i: (rmap(i), 0)),
            pl.BlockSpec((r, LANES), lambda i: (rmap(i), 0)),
        ]

    mexp = jnp.asarray(np.stack([_gla_exponent_matrix(False), _gla_exponent_matrix(True)]),
                       MXU_DTYPE)
    rc = [_ret_constants(False), _ret_constants(True)]
    dmat = jnp.asarray(np.stack([rc[0][0], rc[1][0]]))
    qdec = jnp.asarray(np.stack([rc[0][1], rc[1][1]]))
    kdec = jnp.asarray(np.stack([rc[0][2], rc[1][2]]))
    sdec = jnp.asarray(np.stack([rc[0][3], rc[1][3]]))
    consts = [wg, bg, mexp, dmat, qdec, kdec, sdec]

    def full_spec(a):
        nd = a.ndim
        return pl.BlockSpec(a.shape, lambda i: (0,) * nd)

    d_out = GLA_HEADS * GLA_DV + RET_HEADS * RET_DV
    return pl.pallas_call(
        _scan_kernel,
        grid=(n_blocks,),
        in_specs=stream_specs(rf) + stream_specs(rb) + [full_spec(a) for a in consts],
        out_specs=[pl.BlockSpec((r, d_out), lambda i: (rf(i), 0)),
                   pl.BlockSpec((r, d_out), lambda i: (rb(i), 0))],
        out_shape=[jax.ShapeDtypeStruct((rows, d_out), F32)] * 2,
        scratch_shapes=[pltpu.VMEM((2, GLA_HEADS, GLA_DV, GLA_DK), F32),
                        pltpu.VMEM((2, RET_HEADS, RET_DV, RET_DK), F32)],
        compiler_params=_cparams(1),
        name="even_scan",
    )(z, z, z, z, z, cos_t, sin_t, z, z, z, z, z, cos_t, sin_t, *consts)


def _evgate_kernel(of_ref, ob_ref, gr_ref, rg_ref, nw_ref, a_ref):
    n_heads = GLA_HEADS + RET_HEADS
    for h in range(n_heads):
        lanes = slice(h * GLA_DV, (h + 1) * GLA_DV)
        oh = of_ref[:, lanes] + ob_ref[:, lanes]
        y = oh * lax.rsqrt(jnp.mean(oh * oh, axis=-1, keepdims=True) + RMS_EPS)
        if h < GLA_HEADS:
            y = y * nw_ref[...]
            gate = gr_ref[:, h * GLA_DV:(h + 1) * GLA_DV]
        else:
            hh = h - GLA_HEADS
            gate = rg_ref[:, hh * RET_DV:(hh + 1) * RET_DV]
        a_ref[:, lanes] = _mx(y * _silu(gate))


def _even_gate(o_f, o_b, z, norm_w):
    rows, d_out = o_f.shape
    tm = ROW_TILE
    return pl.pallas_call(
        _evgate_kernel,
        grid=(rows // tm,),
        in_specs=[
            pl.BlockSpec((tm, d_out), lambda i: (i, 0)),
            pl.BlockSpec((tm, d_out), lambda i: (i, 0)),
            pl.BlockSpec((tm, 1024), lambda i: (i, 2)),
            pl.BlockSpec((tm, 1024), lambda i: (i, 5)),
            pl.BlockSpec((1, GLA_DV), lambda i: (0, 0)),
        ],
        out_specs=pl.BlockSpec((tm, d_out), lambda i: (i, 0)),
        out_shape=jax.ShapeDtypeStruct((rows, d_out), MXU_DTYPE),
        compiler_params=_cparams(1),
        name="even_gate",
    )(o_f, o_b, z, z, norm_w.reshape(1, GLA_DV))


def _axial128(xh, cos, sin_a, sin_b):
    return xh * cos + pltpu.roll(xh, LANES - 16, 1) * sin_a + pltpu.roll(xh, 16, 1) * sin_b


def _mlaproj_kernel(h_ref, win_ref, qn_ref, kvn_ref, wuq_ref, wuk_ref, wuvt_ref,
                    cos_ref, sina_ref, sinb_ref, q_ref, k_ref, vt_ref):
    z = _dot(h_ref[...], win_ref[...])
    cq = z[:, :MLA_Q_RANK]
    ckv = z[:, MLA_Q_RANK:MLA_Q_RANK + MLA_KV_RANK]
    kr = z[:, MLA_Q_RANK + MLA_KV_RANK:]
    cos, sin_a, sin_b = cos_ref[...], sina_ref[...], sinb_ref[...]
    cqn = _mx(cq * lax.rsqrt(jnp.mean(cq * cq, axis=-1, keepdims=True) + RMS_EPS) * qn_ref[...])
    ckvn = _mx(ckv * lax.rsqrt(jnp.mean(ckv * ckv, axis=-1, keepdims=True) + RMS_EPS)
               * kvn_ref[...])
    q = _dot(cqn, wuq_ref[...])
    kn = _dot(ckvn, wuk_ref[...])
    vt_ref[...] = _mx(_dot_nt(wuvt_ref[...], ckvn))
    kr_rot = _mx(_axial128(kr, cos, sin_a, sin_b))
    qs = MLA_SCALE * LOG2E
    for h in range(MLA_HEADS):
        b0 = h * QK_PAD
        q_ref[:, b0:b0 + LANES] = _mx(q[:, b0:b0 + LANES] * qs)
        q_ref[:, b0 + LANES:b0 + QK_PAD] = _mx(
            _axial128(q[:, b0 + LANES:b0 + QK_PAD], cos, sin_a, sin_b) * qs)
        k_ref[:, b0:b0 + LANES] = _mx(kn[:, h * MLA_NOPE:(h + 1) * MLA_NOPE])
        k_ref[:, b0 + LANES:b0 + QK_PAD] = kr_rot


def _mla_proj(h, w_in_p, q_norm, kv_norm, w_uq_p, w_uk, w_uvt, cos_t, sina_t, sinb_t):
    rows, d = h.shape
    tm = MLA_PROJ_ROWS
    hq = MLA_HEADS * QK_PAD
    hv = MLA_HEADS * MLA_DV

    def const2(a):
        return pl.BlockSpec(a.shape, lambda i: (0, 0))

    qn = q_norm.reshape(1, -1)
    kvn = kv_norm.reshape(1, -1)
    return pl.pallas_call(
        _mlaproj_kernel,
        grid=(rows // tm,),
        in_specs=[pl.BlockSpec((tm, d), lambda i: (i, 0)),
                  const2(w_in_p), const2(qn), const2(kvn), const2(w_uq_p), const2(w_uk),
                  const2(w_uvt),
                  pl.BlockSpec((tm, LANES), lambda i: (i, 0)),
                  pl.BlockSpec((tm, LANES), lambda i: (i, 0)),
                  pl.BlockSpec((tm, LANES), lambda i: (i, 0))],
        out_specs=[pl.BlockSpec((tm, hq), lambda i: (i, 0)),
                   pl.BlockSpec((tm, hq), lambda i: (i, 0)),
                   pl.BlockSpec((hv, tm), lambda i: (0, i))],
        out_shape=[jax.ShapeDtypeStruct((rows, hq), MXU_DTYPE),
                   jax.ShapeDtypeStruct((rows, hq), MXU_DTYPE),
                   jax.ShapeDtypeStruct((hv, rows), MXU_DTYPE)],
        compiler_params=_cparams(1),
        name="mla_proj",
    )(h, w_in_p, qn, kvn, w_uq_p, w_uk, w_uvt, cos_t, sina_t, sinb_t)


def _flash_kernel(q_ref, k_ref, vt_ref, o_ref, *, n_lat_tiles, n_lat, n_ctx):
    tq = q_ref.shape[0]
    q = q_ref[...]

    def scores(start, size):
        return _dot_nt(k_ref[start:start + size, :], q)

    def store(acc, l):
        o_ref[...] = (acc / l).T.astype(o_ref.dtype)

    def attend_online(kv_slices):
        m = jnp.full((1, tq), -jnp.inf, F32)
        l = jnp.zeros((1, tq), F32)
        acc = jnp.zeros((MLA_DV, tq), F32)
        for start, size in kv_slices:
            s = scores(start, size)
            m_new = jnp.maximum(m, jnp.max(s, axis=0, keepdims=True))
            alpha = jnp.exp2(m - m_new)
            p = jnp.exp2(s - m_new)
            l = alpha * l + jnp.sum(p, axis=0, keepdims=True)
            acc = alpha * acc + _dot(vt_ref[:, start:start + size], _mx(p))
            m = m_new
        store(acc, l)

    def attend_fixed_shift(kv_slices):
        start, size = kv_slices[0]
        s = scores(start, size)
        m = jnp.max(s, axis=0, keepdims=True)
        p = jnp.exp2(s - m)
        l = jnp.sum(p, axis=0, keepdims=True)
        acc = _dot(vt_ref[:, start:start + size], _mx(p))
        for start, size in kv_slices[1:]:
            p = jnp.exp2(scores(start, size) - m)
            l = l + jnp.sum(p, axis=0, keepdims=True)
            acc = acc + _dot(vt_ref[:, start:start + size], _mx(p))
        store(acc, l)
        n_bad = (jnp.sum(jnp.where(l < FIXED_SHIFT_LIMIT, 0.0, 1.0))
                 + jnp.sum(jnp.where(jnp.abs(acc) < FIXED_SHIFT_LIMIT, 0.0, 1.0)))
        return n_bad == 0.0

    n_kv = n_lat + n_ctx
    i = pl.program_id(1)
    lat_slices = [(s0, KV_TILE) for s0 in range(0, n_kv, KV_TILE)]

    @pl.when(i < n_lat_tiles)
    def _():
        ok = attend_fixed_shift(lat_slices)

        @pl.when(jnp.logical_not(ok))
        def _():
            attend_online(lat_slices)

    @pl.when(i >= n_lat_tiles)
    def _():
        attend_online([(n_lat, n_ctx)])


def _mla_attention(q, k, vt, *, n_lat, n_ctx, n_lat_tiles):
    rows = q.shape[0]
    n_kv = n_lat + n_ctx
    tm = ROW_TILE
    kern = functools.partial(_flash_kernel, n_lat_tiles=n_lat_tiles, n_lat=n_lat, n_ctx=n_ctx)
    return pl.pallas_call(
        kern,
        grid=(MLA_HEADS, rows // tm),
        in_specs=[pl.BlockSpec((tm, QK_PAD), lambda h, i: (i, h)),
                  pl.BlockSpec((n_kv, QK_PAD), lambda h, i: (0, h)),
                  pl.BlockSpec((MLA_DV, n_kv), lambda h, i: (h, 0))],
        out_specs=pl.BlockSpec((tm, MLA_DV), lambda h, i: (i, h)),
        out_shape=jax.ShapeDtypeStruct((rows, MLA_HEADS * MLA_DV), MXU_DTYPE),
        compiler_params=_cparams(2),
        name="mla_flash",
    )(q, k, vt)


def _prep_even(w_in, wg_f, bg_f, wg_b, bg_b):
    gqk, gv, gr = w_in[:, 0:1024], w_in[:, 1024:2048], w_in[:, 2048:3072]
    gd = w_in[:, 3072:3104]
    rqk, rv, rg = w_in[:, 3104:4128], w_in[:, 4128:5152], w_in[:, 5152:6176]
    d = w_in.shape[0]
    w = jnp.concatenate([gqk, gv, gr, rqk, rv, rg, gd, jnp.zeros((d, 256 - 32), w_in.dtype)],
                        axis=1)
    r = GLA_GATE_RANK
    hq = GLA_HEADS * GLA_DK
    wg = jnp.zeros((2, LANES, hq), F32)
    wg = wg.at[0, 0:r].set(wg_f).at[1, r:2 * r].set(wg_b)
    bg = jnp.stack([bg_f, bg_b]).reshape(2, 1, hq)
    return _mx(w), _mx(wg), bg


def _prep_mla(w_in, w_uq, w_ukv):
    d = w_in.shape[0]
    w_in_p = _mx(jnp.pad(w_in, ((0, 0), (0, 1152 - w_in.shape[1]))))
    r = w_uq.shape[0]
    uq = w_uq.reshape(r, MLA_HEADS, MLA_NOPE + MLA_ROPE)
    uq = jnp.pad(uq, ((0, 0), (0, 0), (0, QK_PAD - MLA_NOPE - MLA_ROPE)))
    w_uq_p = _mx(uq.reshape(r, MLA_HEADS * QK_PAD))
    ukv = w_ukv.reshape(w_ukv.shape[0], MLA_HEADS, MLA_NOPE + MLA_DV)
    w_uk = _mx(ukv[:, :, :MLA_NOPE].reshape(-1, MLA_HEADS * MLA_NOPE))
    w_uvt = _mx(ukv[:, :, MLA_NOPE:].reshape(-1, MLA_HEADS * MLA_DV).T)
    return w_in_p, w_uq_p, w_uk, w_uvt


def _rope_angles(pos, dim):
    inv = ROPE_BASE ** (-jnp.arange(0, dim, 2, dtype=F32) / dim)
    ang = pos.astype(F32)[:, None] * inv[None, :]
    return jnp.concatenate([ang, ang], axis=-1)


def _ret_tables(n_lat, rows):
    ang = _rope_angles(jnp.arange(n_lat), RET_DK)
    cos = jnp.ones((rows, RET_DK), F32).at[:n_lat].set(jnp.cos(ang))
    sin = jnp.sin(ang)
    sign = jnp.where(jnp.arange(RET_DK) < RET_DK // 2, -1.0, 1.0)
    sin_s = jnp.zeros((rows, RET_DK), F32).at[:n_lat].set(sin * sign)
    return cos, sin_s


def _axial_tables(n_lat, rows):
    n_rows = n_lat // GRID_W
    row = jnp.repeat(jnp.arange(n_rows), GRID_W)
    col = jnp.tile(jnp.arange(GRID_W), n_rows)
    half = MLA_ROPE // 2
    ang = jnp.concatenate([_rope_angles(row, half), _rope_angles(col, half)], axis=-1)
    lane = jnp.arange(MLA_ROPE)
    lo = (lane % half) < (half // 2)
    cos = jnp.ones((rows, LANES), F32).at[:n_lat, :MLA_ROPE].set(jnp.cos(ang))
    sin = jnp.sin(ang)
    sin_a = jnp.zeros((rows, LANES), F32).at[:n_lat, :MLA_ROPE].set(jnp.where(lo, -sin, 0.0))
    sin_b = jnp.zeros((rows, LANES), F32).at[:n_lat, :MLA_ROPE].set(jnp.where(lo, 0.0, sin))
    return cos, sin_a, sin_b


def kernel(x, c, ctx, c_ctx, w_ada, b_ada, ln_g, ln_b, w_ffn_in, w_ffn_out, ev_w_in, ev_gla_wg_f, ev_gla_bg_f, ev_gla_wg_b, ev_gla_bg_b, ev_gla_norm, ev_w_out, od_w_in, od_q_norm, od_kv_norm, od_w_uq, od_w_ukv, od_w_out):
    n_lat, d = x.shape[1], x.shape[2]
    n_ctx = ctx.shape[1]
    depth = w_ada.shape[0]
    assert x.shape[0] == 1 and n_lat % ROW_TILE == 0 and n_ctx == SCAN_ROWS
    assert (n_lat + n_ctx) % KV_TILE == 0
    alpha = (2 * depth) ** 0.25
    n_lat_tiles = n_lat // ROW_TILE
    n_tiles = n_lat_tiles + 1
    rows = n_tiles * ROW_TILE
    n_seq_blocks = (n_lat + n_ctx) // SCAN_ROWS

    x_all = jnp.concatenate(
        [x[0], ctx[0], jnp.zeros((rows - n_lat - n_ctx, d), x.dtype)], axis=0)
    c_rows = jnp.zeros((16, d), F32).at[0].set(c[0]).at[1].set(c_ctx)
    mods = _ada(c_rows, w_ada, b_ada)[:, :2].reshape(depth, 2, N_MOD, d)

    ret_cos, ret_sin = _ret_tables(n_lat, rows)
    ax_cos, ax_sina, ax_sinb = _axial_tables(n_lat, rows)

    wi, wo = _mx(w_ffn_in), _mx(w_ffn_out)
    for l in range(depth):
        last = l == depth - 1
        x_all, h_mix = _ffn(x_all, mods[l], wi, wo, ln_g[l, 0], ln_b[l, 0], layer=l, half=0,
                            mod_row=0, emit_row=3, alpha=alpha, n_lat_tiles=n_lat_tiles,
                            n_tiles=n_tiles)
        if l % 2 == 0:
            e = l // 2
            w_p, wg, bg = _prep_even(ev_w_in[e], ev_gla_wg_f[e], ev_gla_bg_f[e],
                                     ev_gla_wg_b[e], ev_gla_bg_b[e])
            z = _proj(h_mix, w_p, 1280)
            o_f, o_b = _even_scan(z, ret_cos, ret_sin, wg, bg, n_seq_blocks)
            a = _even_gate(o_f, o_b, z, ev_gla_norm[e])
            w_out = _mx(ev_w_out[e])
        else:
            o = l // 2
            w_in_p, w_uq_p, w_uk, w_uvt = _prep_mla(od_w_in[o], od_w_uq[o], od_w_ukv[o])
            q, k, vt = _mla_proj(h_mix, w_in_p, od_q_norm[o], od_kv_norm[o], w_uq_p, w_uk,
                                 w_uvt, ax_cos, ax_sina, ax_sinb)
            a = _mla_attention(q, k, vt, n_lat=n_lat, n_ctx=n_ctx, n_lat_tiles=n_lat_tiles)
            w_out = _mx(od_w_out[o])
        x_all = _outproj_ln(a, w_out, x_all, mods[l], ln_g[l, 1], ln_b[l, 1], gate_row=5,
                            alpha=alpha, n_lat_tiles=n_lat_tiles)
        x_all, _ = _ffn(x_all, mods[l], wi, wo, ln_g[l, 2], ln_b[l, 2], layer=l, half=1,
                        mod_row=6, emit_row=None, alpha=alpha, n_lat_tiles=n_lat_tiles,
                        n_tiles=n_lat_tiles if last else n_tiles)
    return x_all[None]
```

```python
import functools
import math

import numpy as np
import jax
import jax.numpy as jnp
from jax import lax
from jax.experimental import pallas as pl
from jax.experimental.pallas import tpu as pltpu

F32 = jnp.float32
MXU_DTYPE = jnp.bfloat16

N_MOD = 9
LN_EPS = 1e-5
RMS_EPS = 1e-6
CHUNK = 64
ROPE_BASE = 10000.0
GRID_W = 64
GLA_HEADS = 4
GLA_DK = 128
GLA_DV = 256
GLA_GATE_RANK = 16
GLA_GATE_NORM = 16.0
RET_HEADS = 4
RET_DK = 128
RET_DV = 256
MLA_HEADS = 16
MLA_Q_RANK = 512
MLA_KV_RANK = 512
MLA_NOPE = 128
MLA_ROPE = 64
MLA_DV = 128
MLA_SCALE = (MLA_NOPE + MLA_ROPE) ** -0.5
LOG2E = 1.4426950408889634
FIXED_SHIFT_LIMIT = 2.0 ** 100

LANES = 128
ROW_TILE = 512
FF_CHUNK = 1024
LN_SLICE = 128
SCAN_ROWS = 256
MLA_PROJ_ROWS = 256
KV_TILE = 1280
QK_PAD = 256
VMEM_LIMIT = 56 * 1024 * 1024


def _cparams(n_axes, vmem=VMEM_LIMIT):
    return pltpu.CompilerParams(
        dimension_semantics=("arbitrary",) * n_axes, vmem_limit_bytes=vmem)


def _mx(a):
    return a.astype(MXU_DTYPE)


def _dot(a, b):
    return jnp.dot(a, b, preferred_element_type=F32)


def _dot_nt(a, b):
    return lax.dot_general(a, b, (((1,), (1,)), ((), ())), preferred_element_type=F32)


def _dot_tn(a, b):
    return lax.dot_general(a, b, (((0,), (0,)), ((), ())), preferred_element_type=F32)


def _silu(v):
    return v / (1.0 + jnp.exp(-v))


def _layer_norm(v, g, b):
    mu = jnp.mean(v, axis=-1, keepdims=True)
    d = v - mu
    var = jnp.mean(d * d, axis=-1, keepdims=True)
    return d * lax.rsqrt(var + LN_EPS) * g + b


def _ada_kernel(c_ref, w_ref, b_ref, o_ref):
    s = _silu(c_ref[...])
    o_ref[...] = _dot(_mx(s), _mx(w_ref[...])) + b_ref[...]


def _ada(c_rows, w_ada, b_ada):
    depth, d, n = w_ada.shape
    rows = c_rows.shape[0]
    tn = 1024
    return pl.pallas_call(
        _ada_kernel,
        grid=(depth, n // tn),
        in_specs=[
            pl.BlockSpec((rows, d), lambda l, j: (0, 0)),
            pl.BlockSpec((None, d, tn), lambda l, j: (l, 0, j)),
            pl.BlockSpec((None, 1, tn), lambda l, j: (l, 0, j)),
        ],
        out_specs=pl.BlockSpec((None, rows, tn), lambda l, j: (l, 0, j)),
        out_shape=jax.ShapeDtypeStruct((depth, rows, n), F32),
        compiler_params=_cparams(2),
        name="ada_mod",
    )(c_rows, w_ada, b_ada.reshape(depth, 1, n))


def _ffn_kernel(x_ref, mod_ref, wg_ref, wu_ref, wout_ref, g_ref, b_ref, *rest,
                mod_row, emit_row, alpha, overlap, n_tiles):
    if emit_row is None:
        o_ref, xm_scr, acc_scr, v_scr, em_scr = rest
        h_ref = None
    else:
        o_ref, h_ref, xm_scr, acc_scr, v_scr, em_scr = rest
    i, k = pl.program_id(0), pl.program_id(1)
    last_k = pl.num_programs(1) - 1
    n_slices = v_scr.shape[0] // LN_SLICE

    def finish_rows():
        r0 = pl.multiple_of(jnp.minimum(k, n_slices - 1) * LN_SLICE, LN_SLICE)
        y = _layer_norm(v_scr[pl.ds(r0, LN_SLICE), :], g_ref[...], b_ref[...])
        o_ref[...] = y
        if h_ref is not None:
            h_ref[...] = _mx(y * (1.0 + em_scr[1:2, :]) + em_scr[0:1, :])

    def accumulate(skip):
        xm = xm_scr[...]
        act = _silu(_dot(xm, wg_ref[:, skip:])) * _dot(xm, wu_ref[:, skip:])
        acc_scr[...] += _dot(_mx(act), wout_ref[skip:, :])

    @pl.when((i == 0) & (k == 0))
    def _():
        v_scr[...] = jnp.zeros_like(v_scr)
        em_scr[...] = jnp.zeros_like(em_scr)

    @pl.when((i < n_tiles) & (k == 0))
    def _():
        shift = mod_ref[mod_row:mod_row + 1, :]
        scale = mod_ref[mod_row + 1:mod_row + 2, :]
        xm_scr[...] = _mx(x_ref[...] * (1.0 + scale) + shift)
        acc_scr[...] = jnp.zeros_like(acc_scr)

    @pl.when((i < n_tiles) & (k < last_k))
    def _():
        accumulate(0)
        finish_rows()

    @pl.when((i < n_tiles) & (k == last_k))
    def _():
        accumulate(overlap)
        finish_rows()
        gate = mod_ref[mod_row + 2:mod_row + 3, :]
        v_scr[...] = alpha * x_ref[...] + 0.5 * gate * acc_scr[...]
        if h_ref is not None:
            em_scr[...] = mod_ref[emit_row:emit_row + 2, :]

    @pl.when(i == n_tiles)
    def _():
        finish_rows()


def _ffn(x_all, mods, w_in, w_out, ln_g, ln_b, *, layer, half, mod_row, emit_row, alpha,
         n_lat_tiles, n_tiles):
    d = x_all.shape[1]
    f = w_out.shape[2]
    tm, tk = ROW_TILE, FF_CHUNK
    n_chunks = -(-f // tk)
    overlap = n_chunks * tk - f
    rows = n_tiles * tm

    assert f % LANES == 0 and tk % LANES == 0 and n_chunks >= tm // LN_SLICE

    def win(i, k, base=0):
        k = jnp.where(i < n_tiles, k, n_chunks - 1)
        return (jnp.minimum(k * (tk // LANES), (f - tk) // LANES) + base // LANES) * LANES

    def src(i):
        return jnp.minimum(i, n_tiles - 1)

    n_slices = tm // LN_SLICE

    def dst(i, k):
        return jnp.where(i == 0, 0, (i - 1) * n_slices + jnp.minimum(k, n_slices - 1))

    out_shape = [jax.ShapeDtypeStruct((rows, d), F32)]
    out_specs = [pl.BlockSpec((LN_SLICE, d), lambda i, k: (dst(i, k), 0))]
    if emit_row is not None:
        out_shape.append(jax.ShapeDtypeStruct((rows, d), MXU_DTYPE))
        out_specs.append(pl.BlockSpec((LN_SLICE, d), lambda i, k: (dst(i, k), 0)))
    kern = functools.partial(_ffn_kernel, mod_row=mod_row, emit_row=emit_row,
                             alpha=alpha, overlap=overlap, n_tiles=n_tiles)
    res = pl.pallas_call(
        kern,
        grid=(n_tiles + 1, n_chunks),
        in_specs=[
            pl.BlockSpec((tm, d), lambda i, k: (src(i), 0)),
            pl.BlockSpec((None, N_MOD, d), lambda i, k: (src(i) // n_lat_tiles, 0, 0)),
            pl.BlockSpec((None, None, pl.Element(d), pl.Element(tk)),
                         lambda i, k: (layer, half, 0, win(i, k))),
            pl.BlockSpec((None, None, pl.Element(d), pl.Element(tk)),
                         lambda i, k: (layer, half, 0, win(i, k, f))),
            pl.BlockSpec((None, None, pl.Element(tk), pl.Element(d)),
                         lambda i, k: (layer, half, win(i, k), 0)),
            pl.BlockSpec((1, d), lambda i, k: (0, 0)),
            pl.BlockSpec((1, d), lambda i, k: (0, 0)),
        ],
        out_specs=out_specs,
        out_shape=out_shape,
        scratch_shapes=[pltpu.VMEM((tm, d), MXU_DTYPE), pltpu.VMEM((tm, d), F32),
                        pltpu.VMEM((tm, d), F32), pltpu.VMEM((2, d), F32)],
        compiler_params=_cparams(2),
        name="ffn",
    )(x_all, mods, w_in, w_in, w_out, ln_g.reshape(1, d), ln_b.reshape(1, d))
    return res if emit_row is not None else (res[0], None)


def _proj_kernel(h_ref, w_ref, o_ref):
    o_ref[...] = _dot(h_ref[...], w_ref[...])


def _proj(h, w, tn):
    rows, kdim = h.shape
    n = w.shape[1]
    tm = ROW_TILE
    return pl.pallas_call(
        _proj_kernel,
        grid=(n // tn, rows // tm),
        in_specs=[
            pl.BlockSpec((tm, kdim), lambda j, i: (i, 0)),
            pl.BlockSpec((kdim, tn), lambda j, i: (0, j)),
        ],
        out_specs=pl.BlockSpec((tm, tn), lambda j, i: (i, j)),
        out_shape=jax.ShapeDtypeStruct((rows, n), F32),
        compiler_params=_cparams(2),
        name="even_proj",
    )(h, w)


def _outproj_kernel(a_ref, w_ref, x_ref, mod_ref, g_ref, b_ref, o_ref, *, gate_row, alpha):
    y = _dot(a_ref[...], w_ref[...])
    gate = mod_ref[gate_row:gate_row + 1, :]
    v = alpha * x_ref[...] + gate * y
    o_ref[...] = _layer_norm(v, g_ref[...], b_ref[...])


def _outproj_ln(a, w, x_all, mods, ln_g, ln_b, *, gate_row, alpha, n_lat_tiles):
    rows, d = x_all.shape
    kdim = a.shape[1]
    tm = ROW_TILE
    kern = functools.partial(_outproj_kernel, gate_row=gate_row, alpha=alpha)
    return pl.pallas_call(
        kern,
        grid=(rows // tm,),
        in_specs=[
            pl.BlockSpec((tm, kdim), lambda i: (i, 0)),
            pl.BlockSpec((kdim, d), lambda i: (0, 0)),
            pl.BlockSpec((tm, d), lambda i: (i, 0)),
            pl.BlockSpec((None, N_MOD, d), lambda i: (i // n_lat_tiles, 0, 0)),
            pl.BlockSpec((1, d), lambda i: (0, 0)),
            pl.BlockSpec((1, d), lambda i: (0, 0)),
        ],
        out_specs=pl.BlockSpec((tm, d), lambda i: (i, 0)),
        out_shape=jax.ShapeDtypeStruct((rows, d), F32),
        compiler_params=_cparams(1),
        name="mix_out",
    )(a, w, x_all, mods, ln_g.reshape(1, d), ln_b.reshape(1, d))


_GLA_LEVELS = (64, 32, 16, 8, 4, 2)


def _scan_positions(reverse):
    idx = np.arange(CHUNK)
    return idx[::-1].copy() if reverse else idx


def _gla_exponent_matrix(reverse):
    pos = _scan_positions(reverse)
    pi, pt = pos[:, None], pos[None, :]
    mats = [pt <= pi,
            pt > pi]
    for n in _GLA_LEVELS:
        half = n // 2
        ref = (pi // n) * n + half - 1
        q_side = (pi % n) >= half
        mats.append(np.where(q_side, (pt > ref) & (pt <= pi), (pt > pi) & (pt <= ref)))
    return np.concatenate(mats, axis=0).astype(np.float32)


def _ret_constants(reverse):
    scales = np.arange(RET_HEADS, dtype=np.float64)
    off = 5.5 if reverse else 5.0
    log_gamma = np.log1p(-np.exp2(-off - scales))
    idx = np.arange(CHUNK, dtype=np.float64)
    diff = idx[:, None] - idx[None, :]
    dmat = np.where(diff >= 0, np.exp(log_gamma[:, None, None] * np.maximum(diff, 0.0)), 0.0)
    q_dec = np.exp(log_gamma[:, None] * (idx + 1.0))
    k_dec = np.exp(log_gamma[:, None] * (CHUNK - 1.0 - idx))
    s_dec = np.exp(log_gamma * CHUNK)
    if reverse:
        dmat = dmat[:, ::-1, ::-1]
        q_dec = q_dec[:, ::-1]
        k_dec = k_dec[:, ::-1]
    expand = lambda a: np.repeat(a.T, RET_DK, axis=1)
    return (dmat.astype(np.float32), expand(q_dec).astype(np.float32),
            expand(k_dec).astype(np.float32),
            np.repeat(s_dec, RET_DK)[None, :].astype(np.float32))


def _log_sigmoid(v):
    return jnp.minimum(v, 0.0) - jnp.log1p(jnp.exp(-jnp.abs(v)))


def _gla_masks(reverse):
    c = CHUNK
    row = lax.broadcasted_iota(jnp.int32, (c, c), 0)
    col = lax.broadcasted_iota(jnp.int32, (c, c), 1)
    pr, pc = ((c - 1) - row, (c - 1) - col) if reverse else (row, col)
    levels = []
    for n in _GLA_LEVELS:
        sh, half = int(math.log2(n)), n // 2
        levels.append((lax.shift_right_logical(pr, sh) == lax.shift_right_logical(pc, sh))
                      & ((pr & (n - 1)) >= half) & ((pc & (n - 1)) < half))
    return levels, row == col


def _rope128(xh, cos, sin_signed):
    return xh * cos + pltpu.roll(xh, 64, 1) * sin_signed


def _scan_chunk_step(streams, consts, out_refs, sg_scr, sr_scr, lgs, masks, cc):
    wg_ref, bg_ref, mexp_ref, dmat_ref, qdec_ref, kdec_ref, sdec_ref = consts
    c = CHUNK
    n_chunks = SCAN_ROWS // CHUNK
    hq = GLA_HEADS * GLA_DK
    hr = RET_HEADS * RET_DK
    o_off = GLA_HEADS * GLA_DV
    tasks = []
    for d in (0, 1):
        zqk, zv, zrqk, zrv, _, cos_ref, sin_ref = streams[d]
        reverse = d == 1
        r0 = ((n_chunks - 1 - cc) if reverse else cc) * c
        rows = slice(r0, r0 + c)
        g = lgs[d][rows]
        g_hi = _mx(g)
        g_lo = _mx(g - g_hi.astype(F32))
        mexp = mexp_ref[d]
        w_all = jnp.exp(_dot(mexp, g_hi) + _dot(mexp, g_lo))
        cs, sn = cos_ref[rows, :], sin_ref[rows, :]
        for h in range(GLA_HEADS):
            lanes = slice(h * GLA_DK, (h + 1) * GLA_DK)
            tasks.append(dict(
                kind="gla", d=d, rows=rows, w=w_all[:, lanes],
                q=zqk[rows, lanes] * (GLA_DK ** -0.5),
                k=zqk[rows, hq + h * GLA_DK:hq + (h + 1) * GLA_DK],
                v=zv[rows, h * GLA_DV:(h + 1) * GLA_DV],
                st=sg_scr.at[d, h], cols=slice(h * GLA_DV, (h + 1) * GLA_DV)))
        for h in range(RET_HEADS):
            lanes = slice(h * RET_DK, (h + 1) * RET_DK)
            tasks.append(dict(
                kind="ret", d=d, rows=rows,
                q=_rope128(zrqk[rows, lanes], cs, sn),
                k=_rope128(zrqk[rows, hr + h * RET_DK:hr + (h + 1) * RET_DK]
                           * (RET_DK ** -0.5), cs, sn),
                v=zrv[rows, h * RET_DV:(h + 1) * RET_DV],
                dmat=dmat_ref[d, h], q_dec=qdec_ref[d][:, lanes], k_dec=kdec_ref[d][:, lanes],
                s_dec=sdec_ref[d][:, lanes], st=sr_scr.at[d, h],
                cols=slice(o_off + h * RET_DV, o_off + (h + 1) * RET_DV)))
    for t in tasks:
        t["s_t"] = t["st"][...]
        if t["kind"] == "gla":
            w, q, k = t["w"], t["q"], t["k"]
            t["inter"] = _dot_nt(_mx(q * w[0:c]), _mx(t["s_t"]))
            t["lvl"] = [_dot_nt(_mx(q * w[(2 + li) * c:(3 + li) * c]),
                                _mx(k * w[(2 + li) * c:(3 + li) * c]))
                        for li in range(len(_GLA_LEVELS))]
        else:
            t["inter"] = _dot_nt(_mx(t["q"] * t["q_dec"]), _mx(t["s_t"]))
            t["qk"] = _dot_nt(_mx(t["q"]), _mx(t["k"]))
    for t in tasks:
        if t["kind"] == "gla":
            level_masks, diag_mask = masks[t["d"]]
            a = jnp.where(diag_mask, jnp.sum(t["q"] * t["k"], axis=-1, keepdims=True), 0.0)
            for li in range(len(_GLA_LEVELS)):
                a = a + jnp.where(level_masks[li], t["lvl"][li], 0.0)
        else:
            a = t["qk"] * t["dmat"]
        t["a"] = _mx(a)
    for t in tasks:
        v = _mx(t["v"])
        out_refs[t["d"]][t["rows"], t["cols"]] = t["inter"] + _dot(t["a"], v)
        if t["kind"] == "gla":
            w = t["w"]
            decay = w[0:1, :] if t["d"] == 1 else w[c - 1:c, :]
            k_dec = t["k"] * w[c:2 * c]
        else:
            decay = t["s_dec"]
            k_dec = t["k"] * t["k_dec"]
        t["st"][...] = decay * t["s_t"] + _dot_tn(v, _mx(k_dec))


def _scan_kernel(*refs):
    streams, consts = (refs[0:7], refs[7:14]), refs[14:21]
    of_ref, ob_ref, sg_scr, sr_scr = refs[21:]
    wg_ref, bg_ref = consts[0], consts[1]

    @pl.when(pl.program_id(0) == 0)
    def _():
        sg_scr[...] = jnp.zeros_like(sg_scr)
        sr_scr[...] = jnp.zeros_like(sr_scr)

    lgs = [_log_sigmoid(_dot(_mx(streams[d][4][...]), wg_ref[d]) + bg_ref[d])
           * (1.0 / GLA_GATE_NORM) for d in (0, 1)]
    masks = [_gla_masks(False), _gla_masks(True)]
    for cc in range(SCAN_ROWS // CHUNK):
        _scan_chunk_step(streams, consts, (of_ref, ob_ref), sg_scr, sr_scr, lgs, masks, cc)


def _even_scan(z, cos_t, sin_t, wg, bg, n_seq_blocks):
    rows = z.shape[0]
    n_blocks = rows // SCAN_ROWS
    n_lat_blocks = n_seq_blocks - (256 // SCAN_ROWS)
    r = SCAN_ROWS

    def rf(i):
        return jnp.where(i < n_seq_blocks, (i + n_lat_blocks) % n_seq_blocks, i)

    def rb(i):
        return jnp.where(i < n_seq_blocks, n_seq_blocks - 1 - i, i)

    def stream_specs(rmap):
        return [
            pl.BlockSpec((r, 1024), lambda i: (rmap(i), 0)),
            pl.BlockSpec((r, 1024), lambda i: (rmap(i), 1)),
            pl.BlockSpec((r, 1024), lambda i: (rmap(i), 3)),
            pl.BlockSpec((r, 1024), lambda i: (rmap(i), 4)),
            pl.BlockSpec((r, LANES), lambda i: (rmap(i), 48)),
            pl.BlockSpec((r, LANES), lambda i: (rmap(i), 0)),
            pl.BlockSpec((r, LANES), lambda i: (rmap(i), 0)),
        ]

    mexp = jnp.asarray(np.stack([_gla_exponent_matrix(False), _gla_exponent_matrix(True)]),
                       MXU_DTYPE)
    rc = [_ret_constants(False), _ret_constants(True)]
    dmat = jnp.asarray(np.stack([rc[0][0], rc[1][0]]))
    qdec = jnp.asarray(np.stack([rc[0][1], rc[1][1]]))
    kdec = jnp.asarray(np.stack([rc[0][2], rc[1][2]]))
    sdec = jnp.asarray(np.stack([rc[0][3], rc[1][3]]))
    consts = [wg, bg, mexp, dmat, qdec, kdec, sdec]

    def full_spec(a):
        nd = a.ndim
        return pl.BlockSpec(a.shape, lambda i: (0,) * nd)

    d_out = GLA_HEADS * GLA_DV + RET_HEADS * RET_DV
    return pl.pallas_call(
        _scan_kernel,
        grid=(n_blocks,),
        in_specs=stream_specs(rf) + stream_specs(rb) + [full_spec(a) for a in consts],
        out_specs=[pl.BlockSpec((r, d_out), lambda i: (rf(i), 0)),
                   pl.BlockSpec((r, d_out), lambda i: (rb(i), 0))],
        out_shape=[jax.ShapeDtypeStruct((rows, d_out), F32)] * 2,
        scratch_shapes=[pltpu.VMEM((2, GLA_HEADS, GLA_DV, GLA_DK), F32),
                        pltpu.VMEM((2, RET_HEADS, RET_DV, RET_DK), F32)],
        compiler_params=_cparams(1),
        name="even_scan",
    )(z, z, z, z, z, cos_t, sin_t, z, z, z, z, z, cos_t, sin_t, *consts)


def _evgate_kernel(of_ref, ob_ref, gr_ref, rg_ref, nw_ref, a_ref):
    n_heads = GLA_HEADS + RET_HEADS
    for h in range(n_heads):
        lanes = slice(h * GLA_DV, (h + 1) * GLA_DV)
        oh = of_ref[:, lanes] + ob_ref[:, lanes]
        y = oh * lax.rsqrt(jnp.mean(oh * oh, axis=-1, keepdims=True) + RMS_EPS)
        if h < GLA_HEADS:
            y = y * nw_ref[...]
            gate = gr_ref[:, h * GLA_DV:(h + 1) * GLA_DV]
        else:
            hh = h - GLA_HEADS
            gate = rg_ref[:, hh * RET_DV:(hh + 1) * RET_DV]
        a_ref[:, lanes] = _mx(y * _silu(gate))


def _even_gate(o_f, o_b, z, norm_w):
    rows, d_out = o_f.shape
    tm = ROW_TILE
    return pl.pallas_call(
        _evgate_kernel,
        grid=(rows // tm,),
        in_specs=[
            pl.BlockSpec((tm, d_out), lambda i: (i, 0)),
            pl.BlockSpec((tm, d_out), lambda i: (i, 0)),
            pl.BlockSpec((tm, 1024), lambda i: (i, 2)),
            pl.BlockSpec((tm, 1024), lambda i: (i, 5)),
            pl.BlockSpec((1, GLA_DV), lambda i: (0, 0)),
        ],
        out_specs=pl.BlockSpec((tm, d_out), lambda i: (i, 0)),
        out_shape=jax.ShapeDtypeStruct((rows, d_out), MXU_DTYPE),
        compiler_params=_cparams(1),
        name="even_gate",
    )(o_f, o_b, z, z, norm_w.reshape(1, GLA_DV))


def _axial128(xh, cos, sin_a, sin_b):
    return xh * cos + pltpu.roll(xh, LANES - 16, 1) * sin_a + pltpu.roll(xh, 16, 1) * sin_b


def _mlaproj_kernel(h_ref, win_ref, qn_ref, kvn_ref, wuq_ref, wuk_ref, wuvt_ref,
                    cos_ref, sina_ref, sinb_ref, q_ref, k_ref, vt_ref):
    z = _dot(h_ref[...], win_ref[...])
    cq = z[:, :MLA_Q_RANK]
    ckv = z[:, MLA_Q_RANK:MLA_Q_RANK + MLA_KV_RANK]
    kr = z[:, MLA_Q_RANK + MLA_KV_RANK:]
    cos, sin_a, sin_b = cos_ref[...], sina_ref[...], sinb_ref[...]
    cqn = _mx(cq * lax.rsqrt(jnp.mean(cq * cq, axis=-1, keepdims=True) + RMS_EPS) * qn_ref[...])
    ckvn = _mx(ckv * lax.rsqrt(jnp.mean(ckv * ckv, axis=-1, keepdims=True) + RMS_EPS)
               * kvn_ref[...])
    q = _dot(cqn, wuq_ref[...])
    kn = _dot(ckvn, wuk_ref[...])
    vt_ref[...] = _mx(_dot_nt(wuvt_ref[...], ckvn))
    kr_rot = _mx(_axial128(kr, cos, sin_a, sin_b))
    qs = MLA_SCALE * LOG2E
    for h in range(MLA_HEADS):
        b0 = h * QK_PAD
        q_ref[:, b0:b0 + LANES] = _mx(q[:, b0:b0 + LANES] * qs)
        q_ref[:, b0 + LANES:b0 + QK_PAD] = _mx(
            _axial128(q[:, b0 + LANES:b0 + QK_PAD], cos, sin_a, sin_b) * qs)
        k_ref[:, b0:b0 + LANES] = _mx(kn[:, h * MLA_NOPE:(h + 1) * MLA_NOPE])
        k_ref[:, b0 + LANES:b0 + QK_PAD] = kr_rot


def _mla_proj(h, w_in_p, q_norm, kv_norm, w_uq_p, w_uk, w_uvt, cos_t, sina_t, sinb_t):
    rows, d = h.shape
    tm = MLA_PROJ_ROWS
    hq = MLA_HEADS * QK_PAD
    hv = MLA_HEADS * MLA_DV

    def const2(a):
        return pl.BlockSpec(a.shape, lambda i: (0, 0))

    qn = q_norm.reshape(1, -1)
    kvn = kv_norm.reshape(1, -1)
    return pl.pallas_call(
        _mlaproj_kernel,
        grid=(rows // tm,),
        in_specs=[pl.BlockSpec((tm, d), lambda i: (i, 0)),
                  const2(w_in_p), const2(qn), const2(kvn), const2(w_uq_p), const2(w_uk),
                  const2(w_uvt),
                  pl.BlockSpec((tm, LANES), lambda i: (i, 0)),
                  pl.BlockSpec((tm, LANES), lambda i: (i, 0)),
                  pl.BlockSpec((tm, LANES), lambda i: (i, 0))],
        out_specs=[pl.BlockSpec((tm, hq), lambda i: (i, 0)),
                   pl.BlockSpec((tm, hq), lambda i: (i, 0)),
                   pl.BlockSpec((hv, tm), lambda i: (0, i))],
        out_shape=[jax.ShapeDtypeStruct((rows, hq), MXU_DTYPE),
                   jax.ShapeDtypeStruct((rows, hq), MXU_DTYPE),
                   jax.ShapeDtypeStruct((hv, rows), MXU_DTYPE)],
        compiler_params=_cparams(1),
        name="mla_proj",
    )(h, w_in_p, qn, kvn, w_uq_p, w_uk, w_uvt, cos_t, sina_t, sinb_t)


def _flash_kernel(q_ref, k_ref, vt_ref, o_ref, *, n_lat_tiles, n_lat, n_ctx):
    tq = q_ref.shape[0]
    q = q_ref[...]

    def scores(start, size):
        return _dot_nt(k_ref[start:start + size, :], q)

    def store(acc, l):
        o_ref[...] = (acc / l).T.astype(o_ref.dtype)

    def attend_online(kv_slices):
        m = jnp.full((1, tq), -jnp.inf, F32)
        l = jnp.zeros((1, tq), F32)
        acc = jnp.zeros((MLA_DV, tq), F32)
        for start, size in kv_slices:
            s = scores(start, size)
            m_new = jnp.maximum(m, jnp.max(s, axis=0, keepdims=True))
            alpha = jnp.exp2(m - m_new)
            p = jnp.exp2(s - m_new)
            l = alpha * l + jnp.sum(p, axis=0, keepdims=True)
            acc = alpha * acc + _dot(vt_ref[:, start:start + size], _mx(p))
            m = m_new
        store(acc, l)

    def attend_fixed_shift(kv_slices):
        start, size = kv_slices[0]
        s = scores(start, size)
        m = jnp.max(s, axis=0, keepdims=True)
        p = jnp.exp2(s - m)
        l = jnp.sum(p, axis=0, keepdims=True)
        acc = _dot(vt_ref[:, start:start + size], _mx(p))
        for start, size in kv_slices[1:]:
            p = jnp.exp2(scores(start, size) - m)
            l = l + jnp.sum(p, axis=0, keepdims=True)
            acc = acc + _dot(vt_ref[:, start:start + size], _mx(p))
        store(acc, l)
        n_bad = (jnp.sum(jnp.where(l < FIXED_SHIFT_LIMIT, 0.0, 1.0))
                 + jnp.sum(jnp.where(jnp.abs(acc) < FIXED_SHIFT_LIMIT, 0.0, 1.0)))
        return n_bad == 0.0

    n_kv = n_lat + n_ctx
    i = pl.program_id(1)
    lat_slices = [(s0, KV_TILE) for s0 in range(0, n_kv, KV_TILE)]

    @pl.when(i < n_lat_tiles)
    def _():
        ok = attend_fixed_shift(lat_slices)

        @pl.when(jnp.logical_not(ok))
        def _():
            attend_online(lat_slices)

    @pl.when(i >= n_lat_tiles)
    def _():
        attend_online([(n_lat, n_ctx)])


def _mla_attention(q, k, vt, *, n_lat, n_ctx, n_lat_tiles):
    rows = q.shape[0]
    n_kv = n_lat + n_ctx
    tm = ROW_TILE
    kern = functools.partial(_flash_kernel, n_lat_tiles=n_lat_tiles, n_lat=n_lat, n_ctx=n_ctx)
    return pl.pallas_call(
        kern,
        grid=(MLA_HEADS, rows // tm),
        in_specs=[pl.BlockSpec((tm, QK_PAD), lambda h, i: (i, h)),
                  pl.BlockSpec((n_kv, QK_PAD), lambda h, i: (0, h)),
                  pl.BlockSpec((MLA_DV, n_kv), lambda h, i: (h, 0))],
        out_specs=pl.BlockSpec((tm, MLA_DV), lambda h, i: (i, h)),
        out_shape=jax.ShapeDtypeStruct((rows, MLA_HEADS * MLA_DV), MXU_DTYPE),
        compiler_params=_cparams(2),
        name="mla_flash",
    )(q, k, vt)


def _prep_even(w_in, wg_f, bg_f, wg_b, bg_b):
    gqk, gv, gr = w_in[:, 0:1024], w_in[:, 1024:2048], w_in[:, 2048:3072]
    gd = w_in[:, 3072:3104]
    rqk, rv, rg = w_in[:, 3104:4128], w_in[:, 4128:5152], w_in[:, 5152:6176]
    d = w_in.shape[0]
    w = jnp.concatenate([gqk, gv, gr, rqk, rv, rg, gd, jnp.zeros((d, 256 - 32), w_in.dtype)],
                        axis=1)
    r = GLA_GATE_RANK
    hq = GLA_HEADS * GLA_DK
    wg = jnp.zeros((2, LANES, hq), F32)
    wg = wg.at[0, 0:r].set(wg_f).at[1, r:2 * r].set(wg_b)
    bg = jnp.stack([bg_f, bg_b]).reshape(2, 1, hq)
    return _mx(w), _mx(wg), bg


def _prep_mla(w_in, w_uq, w_ukv):
    d = w_in.shape[0]
    w_in_p = _mx(jnp.pad(w_in, ((0, 0), (0, 1152 - w_in.shape[1]))))
    r = w_uq.shape[0]
    uq = w_uq.reshape(r, MLA_HEADS, MLA_NOPE + MLA_ROPE)
    uq = jnp.pad(uq, ((0, 0), (0, 0), (0, QK_PAD - MLA_NOPE - MLA_ROPE)))
    w_uq_p = _mx(uq.reshape(r, MLA_HEADS * QK_PAD))
    ukv = w_ukv.reshape(w_ukv.shape[0], MLA_HEADS, MLA_NOPE + MLA_DV)
    w_uk = _mx(ukv[:, :, :MLA_NOPE].reshape(-1, MLA_HEADS * MLA_NOPE))
    w_uvt = _mx(ukv[:, :, MLA_NOPE:].reshape(-1, MLA_HEADS * MLA_DV).T)
    return w_in_p, w_uq_p, w_uk, w_uvt


def _rope_angles(pos, dim):
    inv = ROPE_BASE ** (-jnp.arange(0, dim, 2, dtype=F32) / dim)
    ang = pos.astype(F32)[:, None] * inv[None, :]
    return jnp.concatenate([ang, ang], axis=-1)


def _ret_tables(n_lat, rows):
    ang = _rope_angles(jnp.arange(n_lat), RET_DK)
    cos = jnp.ones((rows, RET_DK), F32).at[:n_lat].set(jnp.cos(ang))
    sin = jnp.sin(ang)
    sign = jnp.where(jnp.arange(RET_DK) < RET_DK // 2, -1.0, 1.0)
    sin_s = jnp.zeros((rows, RET_DK), F32).at[:n_lat].set(sin * sign)
    return cos, sin_s


def _axial_tables(n_lat, rows):
    n_rows = n_lat // GRID_W
    row = jnp.repeat(jnp.arange(n_rows), GRID_W)
    col = jnp.tile(jnp.arange(GRID_W), n_rows)
    half = MLA_ROPE // 2
    ang = jnp.concatenate([_rope_angles(row, half), _rope_angles(col, half)], axis=-1)
    lane = jnp.arange(MLA_ROPE)
    lo = (lane % half) < (half // 2)
    cos = jnp.ones((rows, LANES), F32).at[:n_lat, :MLA_ROPE].set(jnp.cos(ang))
    sin = jnp.sin(ang)
    sin_a = jnp.zeros((rows, LANES), F32).at[:n_lat, :MLA_ROPE].set(jnp.where(lo, -sin, 0.0))
    sin_b = jnp.zeros((rows, LANES), F32).at[:n_lat, :MLA_ROPE].set(jnp.where(lo, 0.0, sin))
    return cos, sin_a, sin_b


def kernel(x, c, ctx, c_ctx, w_ada, b_ada, ln_g, ln_b, w_ffn_in, w_ffn_out, ev_w_in, ev_gla_wg_f, ev_gla_bg_f, ev_gla_wg_b, ev_gla_bg_b, ev_gla_norm, ev_w_out, od_w_in, od_q_norm, od_kv_norm, od_w_uq, od_w_ukv, od_w_out):
    n_lat, d = x.shape[1], x.shape[2]
    n_ctx = ctx.shape[1]
    depth = w_ada.shape[0]
    assert x.shape[0] == 1 and n_lat % ROW_TILE == 0 and n_ctx == SCAN_ROWS
    assert (n_lat + n_ctx) % KV_TILE == 0
    alpha = (2 * depth) ** 0.25
    n_lat_tiles = n_lat // ROW_TILE
    n_tiles = n_lat_tiles + 1
    rows = n_tiles * ROW_TILE
    n_seq_blocks = (n_lat + n_ctx) // SCAN_ROWS

    x_all = jnp.concatenate(
        [x[0], ctx[0], jnp.zeros((rows - n_lat - n_ctx, d), x.dtype)], axis=0)
    c_rows = jnp.zeros((16, d), F32).at[0].set(c[0]).at[1].set(c_ctx)
    mods = _ada(c_rows, w_ada, b_ada)[:, :2].reshape(depth, 2, N_MOD, d)

    ret_cos, ret_sin = _ret_tables(n_lat, rows)
    ax_cos, ax_sina, ax_sinb = _axial_tables(n_lat, rows)

    wi, wo = _mx(w_ffn_in), _mx(w_ffn_out)
    for l in range(depth):
        last = l == depth - 1
        x_all, h_mix = _ffn(x_all, mods[l], wi, wo, ln_g[l, 0], ln_b[l, 0], layer=l, half=0,
                            mod_row=0, emit_row=3, alpha=alpha, n_lat_tiles=n_lat_tiles,
                            n_tiles=n_tiles)
        if l % 2 == 0:
            e = l // 2
            w_p, wg, bg = _prep_even(ev_w_in[e], ev_gla_wg_f[e], ev_gla_bg_f[e],
                                     ev_gla_wg_b[e], ev_gla_bg_b[e])
            z = _proj(h_mix, w_p, 1280)
            o_f, o_b = _even_scan(z, ret_cos, ret_sin, wg, bg, n_seq_blocks)
            a = _even_gate(o_f, o_b, z, ev_gla_norm[e])
            w_out = _mx(ev_w_out[e])
        else:
            o = l // 2
            w_in_p, w_uq_p, w_uk, w_uvt = _prep_mla(od_w_in[o], od_w_uq[o], od_w_ukv[o])
            q, k, vt = _mla_proj(h_mix, w_in_p, od_q_norm[o], od_kv_norm[o], w_uq_p, w_uk,
                                 w_uvt, ax_cos, ax_sina, ax_sinb)
            a = _mla_attention(q, k, vt, n_lat=n_lat, n_ctx=n_ctx, n_lat_tiles=n_lat_tiles)
            w_out = _mx(od_w_out[o])
        x_all = _outproj_ln(a, w_out, x_all, mods[l], ln_g[l, 1], ln_b[l, 1], gate_row=5,
                            alpha=alpha, n_lat_tiles=n_lat_tiles)
        x_all, _ = _ffn(x_all, mods[l], wi, wo, ln_g[l, 2], ln_b[l, 2], layer=l, half=1,
                        mod_row=6, emit_row=None, alpha=alpha, n_lat_tiles=n_lat_tiles,
                        n_tiles=n_lat_tiles if last else n_tiles)
    return x_all[None]
```

```python
import functools
import math

import numpy as np
import jax
import jax.numpy as jnp
from jax import lax
from jax.experimental import pallas as pl
from jax.experimental.pallas import tpu as pltpu

F32 = jnp.float32
MXU_DTYPE = jnp.bfloat16

N_MOD = 9
LN_EPS = 1e-5
RMS_EPS = 1e-6
CHUNK = 64
ROPE_BASE = 10000.0
GRID_W = 64
GLA_HEADS = 4
GLA_DK = 128
GLA_DV = 256
GLA_GATE_RANK = 16
GLA_GATE_NORM = 16.0
RET_HEADS = 4
RET_DK = 128
RET_DV = 256
MLA_HEADS = 16
MLA_Q_RANK = 512
MLA_KV_RANK = 512
MLA_NOPE = 128
MLA_ROPE = 64
MLA_DV = 128
MLA_SCALE = (MLA_NOPE + MLA_ROPE) ** -0.5
LOG2E = 1.4426950408889634
FIXED_SHIFT_LIMIT = 2.0 ** 100

LANES = 128
ROW_TILE = 512
FF_CHUNK = 1024
LN_SLICE = 128
SCAN_ROWS = 256
MLA_PROJ_ROWS = 256
KV_TILE = 1280
SHIFT_KEYS = 256
QK_PAD = 256
VMEM_LIMIT = 56 * 1024 * 1024


def _cparams(n_axes, vmem=VMEM_LIMIT):
    return pltpu.CompilerParams(
        dimension_semantics=("arbitrary",) * n_axes, vmem_limit_bytes=vmem)


def _mx(a):
    return a.astype(MXU_DTYPE)


def _dot(a, b):
    return jnp.dot(a, b, preferred_element_type=F32)


def _dot_nt(a, b):
    return lax.dot_general(a, b, (((1,), (1,)), ((), ())), preferred_element_type=F32)


def _dot_tn(a, b):
    return lax.dot_general(a, b, (((0,), (0,)), ((), ())), preferred_element_type=F32)


def _silu(v):
    return v / (1.0 + jnp.exp(-v))


def _layer_norm(v, g, b):
    mu = jnp.mean(v, axis=-1, keepdims=True)
    d = v - mu
    var = jnp.mean(d * d, axis=-1, keepdims=True)
    return d * lax.rsqrt(var + LN_EPS) * g + b


def _ada_kernel(c_ref, w_ref, b_ref, o_ref):
    s = _silu(c_ref[...])
    o_ref[...] = _dot(_mx(s), _mx(w_ref[...])) + b_ref[...]


def _ada(c_rows, w_ada, b_ada):
    depth, d, n = w_ada.shape
    rows = c_rows.shape[0]
    tn = 1024
    return pl.pallas_call(
        _ada_kernel,
        grid=(depth, n // tn),
        in_specs=[
            pl.BlockSpec((rows, d), lambda l, j: (0, 0)),
            pl.BlockSpec((None, d, tn), lambda l, j: (l, 0, j)),
            pl.BlockSpec((None, 1, tn), lambda l, j: (l, 0, j)),
        ],
        out_specs=pl.BlockSpec((None, rows, tn), lambda l, j: (l, 0, j)),
        out_shape=jax.ShapeDtypeStruct((depth, rows, n), F32),
        compiler_params=_cparams(2),
        name="ada_mod",
    )(c_rows, w_ada, b_ada.reshape(depth, 1, n))


def _ffn_kernel(x_ref, mod_ref, wg_ref, wu_ref, wout_ref, g_ref, b_ref, *rest,
                mod_row, emit_row, alpha, overlap, n_tiles):
    if emit_row is None:
        o_ref, xm_scr, acc_scr, v_scr, em_scr = rest
        h_ref = None
    else:
        o_ref, h_ref, xm_scr, acc_scr, v_scr, em_scr = rest
    i, k = pl.program_id(0), pl.program_id(1)
    last_k = pl.num_programs(1) - 1
    n_slices = v_scr.shape[0] // LN_SLICE

    def finish_rows():
        r0 = pl.multiple_of(jnp.minimum(k, n_slices - 1) * LN_SLICE, LN_SLICE)
        y = _layer_norm(v_scr[pl.ds(r0, LN_SLICE), :], g_ref[...], b_ref[...])
        o_ref[...] = y
        if h_ref is not None:
            h_ref[...] = _mx(y * (1.0 + em_scr[1:2, :]) + em_scr[0:1, :])

    def accumulate(skip):
        xm = xm_scr[...]
        act = _silu(_dot(xm, wg_ref[:, skip:])) * _dot(xm, wu_ref[:, skip:])
        acc_scr[...] += _dot(_mx(act), wout_ref[skip:, :])

    @pl.when((i == 0) & (k == 0))
    def _():
        v_scr[...] = jnp.zeros_like(v_scr)
        em_scr[...] = jnp.zeros_like(em_scr)

    @pl.when((i < n_tiles) & (k == 0))
    def _():
        shift = mod_ref[mod_row:mod_row + 1, :]
        scale = mod_ref[mod_row + 1:mod_row + 2, :]
        xm_scr[...] = _mx(x_ref[...] * (1.0 + scale) + shift)
        acc_scr[...] = jnp.zeros_like(acc_scr)

    @pl.when((i < n_tiles) & (k < last_k))
    def _():
        accumulate(0)
        finish_rows()

    @pl.when((i < n_tiles) & (k == last_k))
    def _():
        accumulate(overlap)
        finish_rows()
        gate = mod_ref[mod_row + 2:mod_row + 3, :]
        v_scr[...] = alpha * x_ref[...] + 0.5 * gate * acc_scr[...]
        if h_ref is not None:
            em_scr[...] = mod_ref[emit_row:emit_row + 2, :]

    @pl.when(i == n_tiles)
    def _():
        finish_rows()


def _ffn(x_all, mods, w_in, w_out, ln_g, ln_b, *, layer, half, mod_row, emit_row, alpha,
         n_lat_tiles, n_tiles):
    d = x_all.shape[1]
    f = w_out.shape[2]
    tm, tk = ROW_TILE, FF_CHUNK
    n_chunks = -(-f // tk)
    overlap = n_chunks * tk - f
    rows = n_tiles * tm

    assert f % LANES == 0 and tk % LANES == 0 and n_chunks >= tm // LN_SLICE

    def win(i, k, base=0):
        k = jnp.where(i < n_tiles, k, n_chunks - 1)
        return (jnp.minimum(k * (tk // LANES), (f - tk) // LANES) + base // LANES) * LANES

    def src(i):
        return jnp.minimum(i, n_tiles - 1)

    n_slices = tm // LN_SLICE

    def dst(i, k):
        return jnp.where(i == 0, 0, (i - 1) * n_slices + jnp.minimum(k, n_slices - 1))

    out_shape = [jax.ShapeDtypeStruct((rows, d), F32)]
    out_specs = [pl.BlockSpec((LN_SLICE, d), lambda i, k: (dst(i, k), 0))]
    if emit_row is not None:
        out_shape.append(jax.ShapeDtypeStruct((rows, d), MXU_DTYPE))
        out_specs.append(pl.BlockSpec((LN_SLICE, d), lambda i, k: (dst(i, k), 0)))
    kern = functools.partial(_ffn_kernel, mod_row=mod_row, emit_row=emit_row,
                             alpha=alpha, overlap=overlap, n_tiles=n_tiles)
    res = pl.pallas_call(
        kern,
        grid=(n_tiles + 1, n_chunks),
        in_specs=[
            pl.BlockSpec((tm, d), lambda i, k: (src(i), 0)),
            pl.BlockSpec((None, N_MOD, d), lambda i, k: (src(i) // n_lat_tiles, 0, 0)),
            pl.BlockSpec((None, None, pl.Element(d), pl.Element(tk)),
                         lambda i, k: (layer, half, 0, win(i, k))),
            pl.BlockSpec((None, None, pl.Element(d), pl.Element(tk)),
                         lambda i, k: (layer, half, 0, win(i, k, f))),
            pl.BlockSpec((None, None, pl.Element(tk), pl.Element(d)),
                         lambda i, k: (layer, half, win(i, k), 0)),
            pl.BlockSpec((1, d), lambda i, k: (0, 0)),
            pl.BlockSpec((1, d), lambda i, k: (0, 0)),
        ],
        out_specs=out_specs,
        out_shape=out_shape,
        scratch_shapes=[pltpu.VMEM((tm, d), MXU_DTYPE), pltpu.VMEM((tm, d), F32),
                        pltpu.VMEM((tm, d), F32), pltpu.VMEM((2, d), F32)],
        compiler_params=_cparams(2),
        name="ffn",
    )(x_all, mods, w_in, w_in, w_out, ln_g.reshape(1, d), ln_b.reshape(1, d))
    return res if emit_row is not None else (res[0], None)


def _proj_kernel(h_ref, w_ref, o_ref):
    o_ref[...] = _dot(h_ref[...], w_ref[...])


def _proj(h, w, tn):
    rows, kdim = h.shape
    n = w.shape[1]
    tm = ROW_TILE
    return pl.pallas_call(
        _proj_kernel,
        grid=(n // tn, rows // tm),
        in_specs=[
            pl.BlockSpec((tm, kdim), lambda j, i: (i, 0)),
            pl.BlockSpec((kdim, tn), lambda j, i: (0, j)),
        ],
        out_specs=pl.BlockSpec((tm, tn), lambda j, i: (i, j)),
        out_shape=jax.ShapeDtypeStruct((rows, n), F32),
        compiler_params=_cparams(2),
        name="even_proj",
    )(h, w)


def _outproj_kernel(a_ref, w_ref, x_ref, mod_ref, g_ref, b_ref, o_ref, *, gate_row, alpha):
    y = _dot(a_ref[...], w_ref[...])
    gate = mod_ref[gate_row:gate_row + 1, :]
    v = alpha * x_ref[...] + gate * y
    o_ref[...] = _layer_norm(v, g_ref[...], b_ref[...])


def _outproj_ln(a, w, x_all, mods, ln_g, ln_b, *, gate_row, alpha, n_lat_tiles):
    rows, d = x_all.shape
    kdim = a.shape[1]
    tm = ROW_TILE
    kern = functools.partial(_outproj_kernel, gate_row=gate_row, alpha=alpha)
    return pl.pallas_call(
        kern,
        grid=(rows // tm,),
        in_specs=[
            pl.BlockSpec((tm, kdim), lambda i: (i, 0)),
            pl.BlockSpec((kdim, d), lambda i: (0, 0)),
            pl.BlockSpec((tm, d), lambda i: (i, 0)),
            pl.BlockSpec((None, N_MOD, d), lambda i: (i // n_lat_tiles, 0, 0)),
            pl.BlockSpec((1, d), lambda i: (0, 0)),
            pl.BlockSpec((1, d), lambda i: (0, 0)),
        ],
        out_specs=pl.BlockSpec((tm, d), lambda i: (i, 0)),
        out_shape=jax.ShapeDtypeStruct((rows, d), F32),
        compiler_params=_cparams(1),
        name="mix_out",
    )(a, w, x_all, mods, ln_g.reshape(1, d), ln_b.reshape(1, d))


_GLA_LEVELS = (64, 32, 16, 8, 4, 2)


def _scan_positions(reverse):
    idx = np.arange(CHUNK)
    return idx[::-1].copy() if reverse else idx


def _gla_exponent_matrix(reverse):
    pos = _scan_positions(reverse)
    pi, pt = pos[:, None], pos[None, :]
    mats = [pt <= pi,
            pt > pi]
    for n in _GLA_LEVELS:
        half = n // 2
        ref = (pi // n) * n + half - 1
        q_side = (pi % n) >= half
        mats.append(np.where(q_side, (pt > ref) & (pt <= pi), (pt > pi) & (pt <= ref)))
    return np.concatenate(mats, axis=0).astype(np.float32)


def _ret_constants(reverse):
    scales = np.arange(RET_HEADS, dtype=np.float64)
    off = 5.5 if reverse else 5.0
    log_gamma = np.log1p(-np.exp2(-off - scales))
    idx = np.arange(CHUNK, dtype=np.float64)
    diff = idx[:, None] - idx[None, :]
    dmat = np.where(diff >= 0, np.exp(log_gamma[:, None, None] * np.maximum(diff, 0.0)), 0.0)
    q_dec = np.exp(log_gamma[:, None] * (idx + 1.0))
    k_dec = np.exp(log_gamma[:, None] * (CHUNK - 1.0 - idx))
    s_dec = np.exp(log_gamma * CHUNK)
    if reverse:
        dmat = dmat[:, ::-1, ::-1]
        q_dec = q_dec[:, ::-1]
        k_dec = k_dec[:, ::-1]
    expand = lambda a: np.repeat(a.T, RET_DK, axis=1)
    return (dmat.astype(np.float32), expand(q_dec).astype(np.float32),
            expand(k_dec).astype(np.float32),
            np.repeat(s_dec, RET_DK)[None, :].astype(np.float32))


def _log_sigmoid(v):
    return jnp.minimum(v, 0.0) - jnp.log1p(jnp.exp(-jnp.abs(v)))


def _gla_masks(reverse):
    c = CHUNK
    row = lax.broadcasted_iota(jnp.int32, (c, c), 0)
    col = lax.broadcasted_iota(jnp.int32, (c, c), 1)
    pr, pc = ((c - 1) - row, (c - 1) - col) if reverse else (row, col)
    levels = []
    for n in _GLA_LEVELS:
        sh, half = int(math.log2(n)), n // 2
        levels.append((lax.shift_right_logical(pr, sh) == lax.shift_right_logical(pc, sh))
                      & ((pr & (n - 1)) >= half) & ((pc & (n - 1)) < half))
    return levels, row == col


def _rope128(xh, cos, sin_signed):
    return xh * cos + pltpu.roll(xh, 64, 1) * sin_signed


def _scan_chunk_step(streams, consts, out_refs, sg_scr, sr_scr, lgs, masks, cc):
    wg_ref, bg_ref, mexp_ref, dmat_ref, qdec_ref, kdec_ref, sdec_ref = consts
    c = CHUNK
    n_chunks = SCAN_ROWS // CHUNK
    hq = GLA_HEADS * GLA_DK
    hr = RET_HEADS * RET_DK
    o_off = GLA_HEADS * GLA_DV
    tasks = []
    for d in (0, 1):
        zqk, zv, zrqk, zrv, _, cos_ref, sin_ref = streams[d]
        reverse = d == 1
        r0 = ((n_chunks - 1 - cc) if reverse else cc) * c
        rows = slice(r0, r0 + c)
        g = lgs[d][rows]
        g_hi = _mx(g)
        g_lo = _mx(g - g_hi.astype(F32))
        mexp = mexp_ref[d]
        w_all = jnp.exp(_dot(mexp, g_hi) + _dot(mexp, g_lo))
        cs, sn = cos_ref[rows, :], sin_ref[rows, :]
        for h in range(GLA_HEADS):
            lanes = slice(h * GLA_DK, (h + 1) * GLA_DK)
            tasks.append(dict(
                kind="gla", d=d, rows=rows, w=w_all[:, lanes],
                q=zqk[rows, lanes] * (GLA_DK ** -0.5),
                k=zqk[rows, hq + h * GLA_DK:hq + (h + 1) * GLA_DK],
                v=zv[rows, h * GLA_DV:(h + 1) * GLA_DV],
                st=sg_scr.at[d, h], cols=slice(h * GLA_DV, (h + 1) * GLA_DV)))
        for h in range(RET_HEADS):
            lanes = slice(h * RET_DK, (h + 1) * RET_DK)
            tasks.append(dict(
                kind="ret", d=d, rows=rows,
                q=_rope128(zrqk[rows, lanes], cs, sn),
                k=_rope128(zrqk[rows, hr + h * RET_DK:hr + (h + 1) * RET_DK]
                           * (RET_DK ** -0.5), cs, sn),
                v=zrv[rows, h * RET_DV:(h + 1) * RET_DV],
                dmat=dmat_ref[d, h], q_dec=qdec_ref[d][:, lanes], k_dec=kdec_ref[d][:, lanes],
                s_dec=sdec_ref[d][:, lanes], st=sr_scr.at[d, h],
                cols=slice(o_off + h * RET_DV, o_off + (h + 1) * RET_DV)))
    for t in tasks:
        t["s_t"] = t["st"][...]
        if t["kind"] == "gla":
            w, q, k = t["w"], t["q"], t["k"]
            t["inter"] = _dot_nt(_mx(q * w[0:c]), _mx(t["s_t"]))
            t["lvl"] = [_dot_nt(_mx(q * w[(2 + li) * c:(3 + li) * c]),
                                _mx(k * w[(2 + li) * c:(3 + li) * c]))
                        for li in range(len(_GLA_LEVELS))]
        else:
            t["inter"] = _dot_nt(_mx(t["q"] * t["q_dec"]), _mx(t["s_t"]))
            t["qk"] = _dot_nt(_mx(t["q"]), _mx(t["k"]))
    for t in tasks:
        if t["kind"] == "gla":
            level_masks, diag_mask = masks[t["d"]]
            a = jnp.where(diag_mask, jnp.sum(t["q"] * t["k"], axis=-1, keepdims=True), 0.0)
            for li in range(len(_GLA_LEVELS)):
                a = a + jnp.where(level_masks[li], t["lvl"][li], 0.0)
        else:
            a = t["qk"] * t["dmat"]
        t["a"] = _mx(a)
    for t in tasks:
        v = _mx(t["v"])
        out_refs[t["d"]][t["rows"], t["cols"]] = t["inter"] + _dot(t["a"], v)
        if t["kind"] == "gla":
            w = t["w"]
            decay = w[0:1, :] if t["d"] == 1 else w[c - 1:c, :]
            k_dec = t["k"] * w[c:2 * c]
        else:
            decay = t["s_dec"]
            k_dec = t["k"] * t["k_dec"]
        t["st"][...] = decay * t["s_t"] + _dot_tn(v, _mx(k_dec))


def _scan_kernel(*refs):
    streams, consts = (refs[0:7], refs[7:14]), refs[14:21]
    of_ref, ob_ref, sg_scr, sr_scr = refs[21:]
    wg_ref, bg_ref = consts[0], consts[1]

    @pl.when(pl.program_id(0) == 0)
    def _():
        sg_scr[...] = jnp.zeros_like(sg_scr)
        sr_scr[...] = jnp.zeros_like(sr_scr)

    lgs = [_log_sigmoid(_dot(_mx(streams[d][4][...]), wg_ref[d]) + bg_ref[d])
           * (1.0 / GLA_GATE_NORM) for d in (0, 1)]
    masks = [_gla_masks(False), _gla_masks(True)]
    for cc in range(SCAN_ROWS // CHUNK):
        _scan_chunk_step(streams, consts, (of_ref, ob_ref), sg_scr, sr_scr, lgs, masks, cc)


def _even_scan(z, cos_t, sin_t, wg, bg, n_seq_blocks):
    rows = z.shape[0]
    n_blocks = rows // SCAN_ROWS
    n_lat_blocks = n_seq_blocks - (256 // SCAN_ROWS)
    r = SCAN_ROWS

    def rf(i):
        return jnp.where(i < n_seq_blocks, (i + n_lat_blocks) % n_seq_blocks, i)

    def rb(i):
        return jnp.where(i < n_seq_blocks, n_seq_blocks - 1 - i, i)

    def stream_specs(rmap):
        return [
            pl.BlockSpec((r, 1024), lambda i: (rmap(i), 0)),
            pl.BlockSpec((r, 1024), lambda i: (rmap(i), 1)),
            pl.BlockSpec((r, 1024), lambda i: (rmap(i), 3)),
            pl.BlockSpec((r, 1024), lambda i: (rmap(i), 4)),
            pl.BlockSpec((r, LANES), lambda i: (rmap(i), 48)),
            pl.BlockSpec((r, LANES), lambda i: (rmap(i), 0)),
            pl.BlockSpec((r, LANES), lambda i: (rmap(i), 0)),
        ]

    mexp = jnp.asarray(np.stack([_gla_exponent_matrix(False), _gla_exponent_matrix(True)]),
                       MXU_DTYPE)
    rc = [_ret_constants(False), _ret_constants(True)]
    dmat = jnp.asarray(np.stack([rc[0][0], rc[1][0]]))
    qdec = jnp.asarray(np.stack([rc[0][1], rc[1][1]]))
    kdec = jnp.asarray(np.stack([rc[0][2], rc[1][2]]))
    sdec = jnp.asarray(np.stack([rc[0][3], rc[1][3]]))
    consts = [wg, bg, mexp, dmat, qdec, kdec, sdec]

    def full_spec(a):
        nd = a.ndim
        return pl.BlockSpec(a.shape, lambda i: (0,) * nd)

    d_out = GLA_HEADS * GLA_DV + RET_HEADS * RET_DV
    return pl.pallas_call(
        _scan_kernel,
        grid=(n_blocks,),
        in_specs=stream_specs(rf) + stream_specs(rb) + [full_spec(a) for a in consts],
        out_specs=[pl.BlockSpec((r, d_out), lambda i: (rf(i), 0)),
                   pl.BlockSpec((r, d_out), lambda i: (rb(i), 0))],
        out_shape=[jax.ShapeDtypeStruct((rows, d_out), F32)] * 2,
        scratch_shapes=[pltpu.VMEM((2, GLA_HEADS, GLA_DV, GLA_DK), F32),
                        pltpu.VMEM((2, RET_HEADS, RET_DV, RET_DK), F32)],
        compiler_params=_cparams(1),
        name="even_scan",
    )(z, z, z, z, z, cos_t, sin_t, z, z, z, z, z, cos_t, sin_t, *consts)


def _evgate_kernel(of_ref, ob_ref, gr_ref, rg_ref, nw_ref, a_ref):
    n_heads = GLA_HEADS + RET_HEADS
    for h in range(n_heads):
        lanes = slice(h * GLA_DV, (h + 1) * GLA_DV)
        oh = of_ref[:, lanes] + ob_ref[:, lanes]
        y = oh * lax.rsqrt(jnp.mean(oh * oh, axis=-1, keepdims=True) + RMS_EPS)
        if h < GLA_HEADS:
            y = y * nw_ref[...]
            gate = gr_ref[:, h * GLA_DV:(h + 1) * GLA_DV]
        else:
            hh = h - GLA_HEADS
            gate = rg_ref[:, hh * RET_DV:(hh + 1) * RET_DV]
        a_ref[:, lanes] = _mx(y * _silu(gate))


def _even_gate(o_f, o_b, z, norm_w):
    rows, d_out = o_f.shape
    tm = ROW_TILE
    return pl.pallas_call(
        _evgate_kernel,
        grid=(rows // tm,),
        in_specs=[
            pl.BlockSpec((tm, d_out), lambda i: (i, 0)),
            pl.BlockSpec((tm, d_out), lambda i: (i, 0)),
            pl.BlockSpec((tm, 1024), lambda i: (i, 2)),
            pl.BlockSpec((tm, 1024), lambda i: (i, 5)),
            pl.BlockSpec((1, GLA_DV), lambda i: (0, 0)),
        ],
        out_specs=pl.BlockSpec((tm, d_out), lambda i: (i, 0)),
        out_shape=jax.ShapeDtypeStruct((rows, d_out), MXU_DTYPE),
        compiler_params=_cparams(1),
        name="even_gate",
    )(o_f, o_b, z, z, norm_w.reshape(1, GLA_DV))


def _axial128(xh, cos, sin_a, sin_b):
    return xh * cos + pltpu.roll(xh, LANES - 16, 1) * sin_a + pltpu.roll(xh, 16, 1) * sin_b


def _axial_rows(xt, cos_t, sin_t):
    rot = jnp.concatenate([-xt[16:32], xt[0:16], -xt[48:64], xt[32:48]], axis=0)
    return xt * cos_t + rot * sin_t


def _mlaproj_kernel(h_ref, win_ref, qn_ref, kvn_ref, wuqt_ref, wuk_ref, wuvt_ref,
                    cos_ref, sina_ref, sinb_ref, cost_ref, sint_ref, qt_ref, k_ref, vt_ref):
    z = _dot(h_ref[...], win_ref[...])
    cq = z[:, :MLA_Q_RANK]
    ckv = z[:, MLA_Q_RANK:MLA_Q_RANK + MLA_KV_RANK]
    kr = z[:, MLA_Q_RANK + MLA_KV_RANK:]
    cos, sin_a, sin_b = cos_ref[...], sina_ref[...], sinb_ref[...]
    cqn = _mx(cq * lax.rsqrt(jnp.mean(cq * cq, axis=-1, keepdims=True) + RMS_EPS) * qn_ref[...])
    ckvn = _mx(ckv * lax.rsqrt(jnp.mean(ckv * ckv, axis=-1, keepdims=True) + RMS_EPS)
               * kvn_ref[...])
    qt = _dot_nt(wuqt_ref[...], cqn)
    kn = _dot(ckvn, wuk_ref[...])
    vt_ref[...] = _mx(_dot_nt(wuvt_ref[...], ckvn))
    kr_rot = _mx(_axial128(kr, cos, sin_a, sin_b))
    cos_t, sin_t = cost_ref[...], sint_ref[...]
    qs = MLA_SCALE * LOG2E
    r0, r1 = MLA_NOPE, MLA_NOPE + MLA_ROPE
    for h in range(MLA_HEADS):
        b0 = h * QK_PAD
        qt_ref[b0:b0 + r0, :] = _mx(qt[b0:b0 + r0] * qs)
        qt_ref[b0 + r0:b0 + r1, :] = _mx(_axial_rows(qt[b0 + r0:b0 + r1], cos_t, sin_t) * qs)
        qt_ref[b0 + r1:b0 + QK_PAD, :] = _mx(qt[b0 + r1:b0 + QK_PAD])
        k_ref[:, b0:b0 + LANES] = _mx(kn[:, h * MLA_NOPE:(h + 1) * MLA_NOPE])
        k_ref[:, b0 + LANES:b0 + QK_PAD] = kr_rot


def _mla_proj(h, w_in_p, q_norm, kv_norm, w_uqt, w_uk, w_uvt, tables):
    cos_t, sina_t, sinb_t, cos_rows, sin_rows = tables
    rows, d = h.shape
    tm = MLA_PROJ_ROWS
    hq = MLA_HEADS * QK_PAD
    hv = MLA_HEADS * MLA_DV

    def const2(a):
        return pl.BlockSpec(a.shape, lambda i: (0, 0))

    qn = q_norm.reshape(1, -1)
    kvn = kv_norm.reshape(1, -1)
    return pl.pallas_call(
        _mlaproj_kernel,
        grid=(rows // tm,),
        in_specs=[pl.BlockSpec((tm, d), lambda i: (i, 0)),
                  const2(w_in_p), const2(qn), const2(kvn), const2(w_uqt), const2(w_uk),
                  const2(w_uvt),
                  pl.BlockSpec((tm, LANES), lambda i: (i, 0)),
                  pl.BlockSpec((tm, LANES), lambda i: (i, 0)),
                  pl.BlockSpec((tm, LANES), lambda i: (i, 0)),
                  pl.BlockSpec((MLA_ROPE, tm), lambda i: (0, i)),
                  pl.BlockSpec((MLA_ROPE, tm), lambda i: (0, i))],
        out_specs=[pl.BlockSpec((hq, tm), lambda i: (0, i)),
                   pl.BlockSpec((tm, hq), lambda i: (i, 0)),
                   pl.BlockSpec((hv, tm), lambda i: (0, i))],
        out_shape=[jax.ShapeDtypeStruct((hq, rows), MXU_DTYPE),
                   jax.ShapeDtypeStruct((rows, hq), MXU_DTYPE),
                   jax.ShapeDtypeStruct((hv, rows), MXU_DTYPE)],
        compiler_params=_cparams(1),
        name="mla_proj",
    )(h, w_in_p, qn, kvn, w_uqt, w_uk, w_uvt, cos_t, sina_t, sinb_t, cos_rows, sin_rows)


def _flash_kernel(qt_ref, k_ref, vt_ref, o_ref, *, n_lat_tiles, n_lat, n_ctx):
    tq = qt_ref.shape[1]
    qt = qt_ref[...]

    def scores(start, size):
        return _dot(k_ref[start:start + size, :], qt)

    def store(acc, l):
        o_ref[...] = (acc / l).T.astype(o_ref.dtype)

    def attend_online(kv_slices):
        m = jnp.full((1, tq), -jnp.inf, F32)
        l = jnp.zeros((1, tq), F32)
        acc = jnp.zeros((MLA_DV, tq), F32)
        for start, size in kv_slices:
            s = scores(start, size)
            m_new = jnp.maximum(m, jnp.max(s, axis=0, keepdims=True))
            alpha = jnp.exp2(m - m_new)
            p = jnp.exp2(s - m_new)
            l = alpha * l + jnp.sum(p, axis=0, keepdims=True)
            acc = alpha * acc + _dot(vt_ref[:, start:start + size], _mx(p))
            m = m_new
        store(acc, l)

    def attend_fixed_shift(kv_slices):
        m = jnp.max(scores(kv_slices[0][0], SHIFT_KEYS), axis=0, keepdims=True)
        l = jnp.zeros((1, tq), F32)
        acc = jnp.zeros((MLA_DV, tq), F32)
        for start, size in kv_slices:
            p = jnp.exp2(scores(start, size) - m)
            l = l + jnp.sum(p, axis=0, keepdims=True)
            acc = acc + _dot(vt_ref[:, start:start + size], _mx(p))
        store(acc, l)
        n_bad = (jnp.sum(jnp.where(l < FIXED_SHIFT_LIMIT, 0.0, 1.0))
                 + jnp.sum(jnp.where(jnp.abs(acc) < FIXED_SHIFT_LIMIT, 0.0, 1.0)))
        return n_bad == 0.0

    n_kv = n_lat + n_ctx
    i = pl.program_id(1)
    lat_slices = [(s0, KV_TILE) for s0 in range(0, n_kv, KV_TILE)]

    @pl.when(i < n_lat_tiles)
    def _():
        ok = attend_fixed_shift(lat_slices)

        @pl.when(jnp.logical_not(ok))
        def _():
            attend_online(lat_slices)

    @pl.when(i >= n_lat_tiles)
    def _():
        attend_online([(n_lat, n_ctx)])


def _mla_attention(qt, k, vt, *, n_lat, n_ctx, n_lat_tiles):
    rows = k.shape[0]
    n_kv = n_lat + n_ctx
    tm = ROW_TILE
    kern = functools.partial(_flash_kernel, n_lat_tiles=n_lat_tiles, n_lat=n_lat, n_ctx=n_ctx)
    return pl.pallas_call(
        kern,
        grid=(MLA_HEADS, rows // tm),
        in_specs=[pl.BlockSpec((QK_PAD, tm), lambda h, i: (h, i)),
                  pl.BlockSpec((n_kv, QK_PAD), lambda h, i: (0, h)),
                  pl.BlockSpec((MLA_DV, n_kv), lambda h, i: (h, 0))],
        out_specs=pl.BlockSpec((tm, MLA_DV), lambda h, i: (i, h)),
        out_shape=jax.ShapeDtypeStruct((rows, MLA_HEADS * MLA_DV), MXU_DTYPE),
        compiler_params=_cparams(2),
        name="mla_flash",
    )(qt, k, vt)


def _prep_even(w_in, wg_f, bg_f, wg_b, bg_b):
    gqk, gv, gr = w_in[:, 0:1024], w_in[:, 1024:2048], w_in[:, 2048:3072]
    gd = w_in[:, 3072:3104]
    rqk, rv, rg = w_in[:, 3104:4128], w_in[:, 4128:5152], w_in[:, 5152:6176]
    d = w_in.shape[0]
    w = jnp.concatenate([gqk, gv, gr, rqk, rv, rg, gd, jnp.zeros((d, 256 - 32), w_in.dtype)],
                        axis=1)
    r = GLA_GATE_RANK
    hq = GLA_HEADS * GLA_DK
    wg = jnp.zeros((2, LANES, hq), F32)
    wg = wg.at[0, 0:r].set(wg_f).at[1, r:2 * r].set(wg_b)
    bg = jnp.stack([bg_f, bg_b]).reshape(2, 1, hq)
    return _mx(w), _mx(wg), bg


def _prep_mla(w_in, w_uq, w_ukv):
    d = w_in.shape[0]
    w_in_p = _mx(jnp.pad(w_in, ((0, 0), (0, 1152 - w_in.shape[1]))))
    r = w_uq.shape[0]
    uq = w_uq.reshape(r, MLA_HEADS, MLA_NOPE + MLA_ROPE)
    uq = jnp.pad(uq, ((0, 0), (0, 0), (0, QK_PAD - MLA_NOPE - MLA_ROPE)))
    w_uqt = _mx(uq.reshape(r, MLA_HEADS * QK_PAD).T)
    ukv = w_ukv.reshape(w_ukv.shape[0], MLA_HEADS, MLA_NOPE + MLA_DV)
    w_uk = _mx(ukv[:, :, :MLA_NOPE].reshape(-1, MLA_HEADS * MLA_NOPE))
    w_uvt = _mx(ukv[:, :, MLA_NOPE:].reshape(-1, MLA_HEADS * MLA_DV).T)
    return w_in_p, w_uqt, w_uk, w_uvt


def _rope_angles(pos, dim):
    inv = ROPE_BASE ** (-jnp.arange(0, dim, 2, dtype=F32) / dim)
    ang = pos.astype(F32)[:, None] * inv[None, :]
    return jnp.concatenate([ang, ang], axis=-1)


def _ret_tables(n_lat, rows):
    ang = _rope_angles(jnp.arange(n_lat), RET_DK)
    cos = jnp.ones((rows, RET_DK), F32).at[:n_lat].set(jnp.cos(ang))
    sin = jnp.sin(ang)
    sign = jnp.where(jnp.arange(RET_DK) < RET_DK // 2, -1.0, 1.0)
    sin_s = jnp.zeros((rows, RET_DK), F32).at[:n_lat].set(sin * sign)
    return cos, sin_s


def _axial_tables(n_lat, rows):
    n_rows = n_lat // GRID_W
    row = jnp.repeat(jnp.arange(n_rows), GRID_W)
    col = jnp.tile(jnp.arange(GRID_W), n_rows)
    half = MLA_ROPE // 2
    ang = jnp.concatenate([_rope_angles(row, half), _rope_angles(col, half)], axis=-1)
    lane = jnp.arange(MLA_ROPE)
    lo = (lane % half) < (half // 2)
    cos = jnp.ones((rows, LANES), F32).at[:n_lat, :MLA_ROPE].set(jnp.cos(ang))
    sin = jnp.sin(ang)
    sin_a = jnp.zeros((rows, LANES), F32).at[:n_lat, :MLA_ROPE].set(jnp.where(lo, -sin, 0.0))
    sin_b = jnp.zeros((rows, LANES), F32).at[:n_lat, :MLA_ROPE].set(jnp.where(lo, 0.0, sin))
    cos_rows = jnp.ones((MLA_ROPE, rows), F32).at[:, :n_lat].set(jnp.cos(ang).T)
    sin_rows = jnp.zeros((MLA_ROPE, rows), F32).at[:, :n_lat].set(sin.T)
    return cos, sin_a, sin_b, cos_rows, sin_rows


def kernel(x, c, ctx, c_ctx, w_ada, b_ada, ln_g, ln_b, w_ffn_in, w_ffn_out, ev_w_in, ev_gla_wg_f, ev_gla_bg_f, ev_gla_wg_b, ev_gla_bg_b, ev_gla_norm, ev_w_out, od_w_in, od_q_norm, od_kv_norm, od_w_uq, od_w_ukv, od_w_out):
    n_lat, d = x.shape[1], x.shape[2]
    n_ctx = ctx.shape[1]
    depth = w_ada.shape[0]
    assert x.shape[0] == 1 and n_lat % ROW_TILE == 0 and n_ctx == SCAN_ROWS
    assert (n_lat + n_ctx) % KV_TILE == 0
    alpha = (2 * depth) ** 0.25
    n_lat_tiles = n_lat // ROW_TILE
    n_tiles = n_lat_tiles + 1
    rows = n_tiles * ROW_TILE
    n_seq_blocks = (n_lat + n_ctx) // SCAN_ROWS

    x_all = jnp.concatenate(
        [x[0], ctx[0], jnp.zeros((rows - n_lat - n_ctx, d), x.dtype)], axis=0)
    c_rows = jnp.zeros((16, d), F32).at[0].set(c[0]).at[1].set(c_ctx)
    mods = _ada(c_rows, w_ada, b_ada)[:, :2].reshape(depth, 2, N_MOD, d)

    ret_cos, ret_sin = _ret_tables(n_lat, rows)
    ax_tables = _axial_tables(n_lat, rows)

    wi, wo = _mx(w_ffn_in), _mx(w_ffn_out)
    for l in range(depth):
        last = l == depth - 1
        x_all, h_mix = _ffn(x_all, mods[l], wi, wo, ln_g[l, 0], ln_b[l, 0], layer=l, half=0,
                            mod_row=0, emit_row=3, alpha=alpha, n_lat_tiles=n_lat_tiles,
                            n_tiles=n_tiles)
        if l % 2 == 0:
            e = l // 2
            w_p, wg, bg = _prep_even(ev_w_in[e], ev_gla_wg_f[e], ev_gla_bg_f[e],
                                     ev_gla_wg_b[e], ev_gla_bg_b[e])
            z = _proj(h_mix, w_p, 1280)
            o_f, o_b = _even_scan(z, ret_cos, ret_sin, wg, bg, n_seq_blocks)
            a = _even_gate(o_f, o_b, z, ev_gla_norm[e])
            w_out = _mx(ev_w_out[e])
        else:
            o = l // 2
            w_in_p, w_uqt, w_uk, w_uvt = _prep_mla(od_w_in[o], od_w_uq[o], od_w_ukv[o])
            qt, k, vt = _mla_proj(h_mix, w_in_p, od_q_norm[o], od_kv_norm[o], w_uqt, w_uk,
                                  w_uvt, ax_tables)
            a = _mla_attention(qt, k, vt, n_lat=n_lat, n_ctx=n_ctx, n_lat_tiles=n_lat_tiles)
            w_out = _mx(od_w_out[o])
        x_all = _outproj_ln(a, w_out, x_all, mods[l], ln_g[l, 1], ln_b[l, 1], gate_row=5,
                            alpha=alpha, n_lat_tiles=n_lat_tiles)
        x_all, _ = _ffn(x_all, mods[l], wi, wo, ln_g[l, 2], ln_b[l, 2], layer=l, half=1,
                        mod_row=6, emit_row=None, alpha=alpha, n_lat_tiles=n_lat_tiles,
                        n_tiles=n_lat_tiles if last else n_tiles)
    return x_all[None]
```

```python
import functools
import math

import numpy as np
import jax
import jax.numpy as jnp
from jax import lax
from jax.experimental import pallas as pl
from jax.experimental.pallas import tpu as pltpu

F32 = jnp.float32
MXU_DTYPE = jnp.bfloat16

N_MOD = 9
LN_EPS = 1e-5
RMS_EPS = 1e-6
CHUNK = 64
ROPE_BASE = 10000.0
GRID_W = 64
GLA_HEADS = 4
GLA_DK = 128
GLA_DV = 256
GLA_GATE_RANK = 16
GLA_GATE_NORM = 16.0
RET_HEADS = 4
RET_DK = 128
RET_DV = 256
MLA_HEADS = 16
MLA_Q_RANK = 512
MLA_KV_RANK = 512
MLA_NOPE = 128
MLA_ROPE = 64
MLA_DV = 128
MLA_SCALE = (MLA_NOPE + MLA_ROPE) ** -0.5
LOG2E = 1.4426950408889634
FIXED_SHIFT_LIMIT = 2.0 ** 100

LANES = 128
ROW_TILE = 512
FF_CHUNK = 1024
LN_SLICE = 128
SCAN_ROWS = 256
MLA_PROJ_ROWS = 256
KV_TILE = 1280
SHIFT_KEYS = 256
QK_PAD = 256
VMEM_LIMIT = 56 * 1024 * 1024


def _cparams(n_axes, vmem=VMEM_LIMIT):
    return pltpu.CompilerParams(
        dimension_semantics=("arbitrary",) * n_axes, vmem_limit_bytes=vmem)


def _mx(a):
    return a.astype(MXU_DTYPE)


def _dot(a, b):
    return jnp.dot(a, b, preferred_element_type=F32)


def _dot_nt(a, b):
    return lax.dot_general(a, b, (((1,), (1,)), ((), ())), preferred_element_type=F32)


def _dot_tn(a, b):
    return lax.dot_general(a, b, (((0,), (0,)), ((), ())), preferred_element_type=F32)


def _silu(v):
    return v / (1.0 + jnp.exp(-v))


def _layer_norm(v, g, b):
    mu = jnp.mean(v, axis=-1, keepdims=True)
    d = v - mu
    var = jnp.mean(d * d, axis=-1, keepdims=True)
    return d * lax.rsqrt(var + LN_EPS) * g + b


def _ada_kernel(c_ref, w_ref, b_ref, o_ref):
    s = _silu(c_ref[...])
    o_ref[...] = _dot(_mx(s), _mx(w_ref[...])) + b_ref[...]


def _ada(c_rows, w_ada, b_ada):
    depth, d, n = w_ada.shape
    rows = c_rows.shape[0]
    tn = 1024
    return pl.pallas_call(
        _ada_kernel,
        grid=(depth, n // tn),
        in_specs=[
            pl.BlockSpec((rows, d), lambda l, j: (0, 0)),
            pl.BlockSpec((None, d, tn), lambda l, j: (l, 0, j)),
            pl.BlockSpec((None, 1, tn), lambda l, j: (l, 0, j)),
        ],
        out_specs=pl.BlockSpec((None, rows, tn), lambda l, j: (l, 0, j)),
        out_shape=jax.ShapeDtypeStruct((depth, rows, n), F32),
        compiler_params=_cparams(2),
        name="ada_mod",
    )(c_rows, w_ada, b_ada.reshape(depth, 1, n))


def _ffn_kernel(*refs, mod_row, emit_row, alpha, overlap, n_tiles, tail_tile):
    if tail_tile is None:
        x_ref, xt_ref = refs[0], None
        mod_ref, wg_ref, wu_ref, wout_ref, g_ref, b_ref = refs[1:7]
        rest = refs[7:]
    else:
        x_ref, xt_ref, mod_ref, wg_ref, wu_ref, wout_ref, g_ref, b_ref = refs[:8]
        rest = refs[8:]
    if emit_row is None:
        o_ref, xm_scr, acc_scr, v_scr, em_scr = rest
        h_ref = None
    else:
        o_ref, h_ref, xm_scr, acc_scr, v_scr, em_scr = rest
    i, k = pl.program_id(0), pl.program_id(1)
    last_k = pl.num_programs(1) - 1
    n_slices = v_scr.shape[0] // LN_SLICE

    def read_x():
        if xt_ref is None:
            return x_ref[...]
        return jnp.where(i == tail_tile, xt_ref[...], x_ref[...])

    def finish_rows():
        r0 = pl.multiple_of(jnp.minimum(k, n_slices - 1) * LN_SLICE, LN_SLICE)
        y = _layer_norm(v_scr[pl.ds(r0, LN_SLICE), :], g_ref[...], b_ref[...])
        o_ref[...] = y
        if h_ref is not None:
            h_ref[...] = _mx(y * (1.0 + em_scr[1:2, :]) + em_scr[0:1, :])

    def accumulate(skip):
        xm = xm_scr[...]
        act = _silu(_dot(xm, wg_ref[:, skip:])) * _dot(xm, wu_ref[:, skip:])
        acc_scr[...] += _dot(_mx(act), wout_ref[skip:, :])

    @pl.when((i == 0) & (k == 0))
    def _():
        v_scr[...] = jnp.zeros_like(v_scr)
        em_scr[...] = jnp.zeros_like(em_scr)

    @pl.when((i < n_tiles) & (k == 0))
    def _():
        shift = mod_ref[mod_row:mod_row + 1, :]
        scale = mod_ref[mod_row + 1:mod_row + 2, :]
        xm_scr[...] = _mx(read_x() * (1.0 + scale) + shift)
        acc_scr[...] = jnp.zeros_like(acc_scr)

    @pl.when((i < n_tiles) & (k < last_k))
    def _():
        accumulate(0)
        finish_rows()

    @pl.when((i < n_tiles) & (k == last_k))
    def _():
        accumulate(overlap)
        finish_rows()
        gate = mod_ref[mod_row + 2:mod_row + 3, :]
        v_scr[...] = alpha * read_x() + 0.5 * gate * acc_scr[...]
        if h_ref is not None:
            em_scr[...] = mod_ref[emit_row:emit_row + 2, :]

    @pl.when(i == n_tiles)
    def _():
        finish_rows()


def _ffn(x_all, mods, w_in, w_out, ln_g, ln_b, *, layer, half, mod_row, emit_row, alpha,
         n_lat_tiles, n_tiles, x_tail=None):
    d = x_all.shape[1]
    f = w_out.shape[2]
    tm, tk = ROW_TILE, FF_CHUNK
    n_chunks = -(-f // tk)
    overlap = n_chunks * tk - f
    rows = n_tiles * tm

    assert f % LANES == 0 and tk % LANES == 0 and n_chunks >= tm // LN_SLICE

    def win(i, k, base=0):
        k = jnp.where(i < n_tiles, k, n_chunks - 1)
        return (jnp.minimum(k * (tk // LANES), (f - tk) // LANES) + base // LANES) * LANES

    def src(i):
        return jnp.minimum(i, n_tiles - 1)

    n_slices = tm // LN_SLICE

    def dst(i, k):
        return jnp.where(i == 0, 0, (i - 1) * n_slices + jnp.minimum(k, n_slices - 1))

    out_shape = [jax.ShapeDtypeStruct((rows, d), F32)]
    out_specs = [pl.BlockSpec((LN_SLICE, d), lambda i, k: (dst(i, k), 0))]
    if emit_row is not None:
        out_shape.append(jax.ShapeDtypeStruct((rows, d), MXU_DTYPE))
        out_specs.append(pl.BlockSpec((LN_SLICE, d), lambda i, k: (dst(i, k), 0)))
    tail_tile = None if x_tail is None else n_tiles - 1
    kern = functools.partial(_ffn_kernel, mod_row=mod_row, emit_row=emit_row, alpha=alpha,
                             overlap=overlap, n_tiles=n_tiles, tail_tile=tail_tile)
    if x_tail is None:
        x_args = [x_all]
        x_specs = [pl.BlockSpec((tm, d), lambda i, k: (src(i), 0))]
    else:
        x_args = [x_all, x_tail]
        x_specs = [pl.BlockSpec((tm, d), lambda i, k: (jnp.minimum(i, tail_tile - 1), 0)),
                   pl.BlockSpec((tm, d), lambda i, k: (0, 0))]
    res = pl.pallas_call(
        kern,
        grid=(n_tiles + 1, n_chunks),
        in_specs=x_specs + [
            pl.BlockSpec((None, N_MOD, d), lambda i, k: (src(i) // n_lat_tiles, 0, 0)),
            pl.BlockSpec((None, None, pl.Element(d), pl.Element(tk)),
                         lambda i, k: (layer, half, 0, win(i, k))),
            pl.BlockSpec((None, None, pl.Element(d), pl.Element(tk)),
                         lambda i, k: (layer, half, 0, win(i, k, f))),
            pl.BlockSpec((None, None, pl.Element(tk), pl.Element(d)),
                         lambda i, k: (layer, half, win(i, k), 0)),
            pl.BlockSpec((1, d), lambda i, k: (0, 0)),
            pl.BlockSpec((1, d), lambda i, k: (0, 0)),
        ],
        out_specs=out_specs,
        out_shape=out_shape,
        scratch_shapes=[pltpu.VMEM((tm, d), MXU_DTYPE), pltpu.VMEM((tm, d), F32),
                        pltpu.VMEM((tm, d), F32), pltpu.VMEM((2, d), F32)],
        compiler_params=_cparams(2),
        name="ffn",
    )(*x_args, mods, w_in, w_in, w_out, ln_g.reshape(1, d), ln_b.reshape(1, d))
    return res if emit_row is not None else (res[0], None)


def _proj_kernel(h_ref, w_ref, o_ref):
    o_ref[...] = _dot(h_ref[...], w_ref[...])


def _proj(h, w, tn):
    rows, kdim = h.shape
    n = w.shape[1]
    tm = ROW_TILE
    return pl.pallas_call(
        _proj_kernel,
        grid=(n // tn, rows // tm),
        in_specs=[
            pl.BlockSpec((tm, kdim), lambda j, i: (i, 0)),
            pl.BlockSpec((kdim, tn), lambda j, i: (0, j)),
        ],
        out_specs=pl.BlockSpec((tm, tn), lambda j, i: (i, j)),
        out_shape=jax.ShapeDtypeStruct((rows, n), F32),
        compiler_params=_cparams(2),
        name="even_proj",
    )(h, w)


def _outproj_kernel(a_ref, w_ref, x_ref, mod_ref, g_ref, b_ref, o_ref, *, gate_row, alpha):
    y = _dot(a_ref[...], w_ref[...])
    gate = mod_ref[gate_row:gate_row + 1, :]
    v = alpha * x_ref[...] + gate * y
    o_ref[...] = _layer_norm(v, g_ref[...], b_ref[...])


def _outproj_ln(a, w, x_all, mods, ln_g, ln_b, *, gate_row, alpha, n_lat_tiles):
    rows, d = x_all.shape
    kdim = a.shape[1]
    tm = ROW_TILE
    kern = functools.partial(_outproj_kernel, gate_row=gate_row, alpha=alpha)
    return pl.pallas_call(
        kern,
        grid=(rows // tm,),
        in_specs=[
            pl.BlockSpec((tm, kdim), lambda i: (i, 0)),
            pl.BlockSpec((kdim, d), lambda i: (0, 0)),
            pl.BlockSpec((tm, d), lambda i: (i, 0)),
            pl.BlockSpec((None, N_MOD, d), lambda i: (i // n_lat_tiles, 0, 0)),
            pl.BlockSpec((1, d), lambda i: (0, 0)),
            pl.BlockSpec((1, d), lambda i: (0, 0)),
        ],
        out_specs=pl.BlockSpec((tm, d), lambda i: (i, 0)),
        out_shape=jax.ShapeDtypeStruct((rows, d), F32),
        compiler_params=_cparams(1),
        name="mix_out",
    )(a, w, x_all, mods, ln_g.reshape(1, d), ln_b.reshape(1, d))


_GLA_LEVELS = (64, 32, 16, 8, 4, 2)


def _scan_positions(reverse):
    idx = np.arange(CHUNK)
    return idx[::-1].copy() if reverse else idx


def _gla_exponent_matrix(reverse):
    pos = _scan_positions(reverse)
    pi, pt = pos[:, None], pos[None, :]
    mats = [pt <= pi,
            pt > pi]
    for n in _GLA_LEVELS:
        half = n // 2
        ref = (pi // n) * n + half - 1
        q_side = (pi % n) >= half
        mats.append(np.where(q_side, (pt > ref) & (pt <= pi), (pt > pi) & (pt <= ref)))
    return np.concatenate(mats, axis=0).astype(np.float32)


def _ret_constants(reverse):
    scales = np.arange(RET_HEADS, dtype=np.float64)
    off = 5.5 if reverse else 5.0
    log_gamma = np.log1p(-np.exp2(-off - scales))
    idx = np.arange(CHUNK, dtype=np.float64)
    diff = idx[:, None] - idx[None, :]
    dmat = np.where(diff >= 0, np.exp(log_gamma[:, None, None] * np.maximum(diff, 0.0)), 0.0)
    q_dec = np.exp(log_gamma[:, None] * (idx + 1.0))
    k_dec = np.exp(log_gamma[:, None] * (CHUNK - 1.0 - idx))
    s_dec = np.exp(log_gamma * CHUNK)
    if reverse:
        dmat = dmat[:, ::-1, ::-1]
        q_dec = q_dec[:, ::-1]
        k_dec = k_dec[:, ::-1]
    expand = lambda a: np.repeat(a.T, RET_DK, axis=1)
    return (dmat.astype(np.float32), expand(q_dec).astype(np.float32),
            expand(k_dec).astype(np.float32),
            np.repeat(s_dec, RET_DK)[None, :].astype(np.float32))


def _log_sigmoid(v):
    return jnp.minimum(v, 0.0) - jnp.log1p(jnp.exp(-jnp.abs(v)))


def _gla_masks(reverse):
    c = CHUNK
    row = lax.broadcasted_iota(jnp.int32, (c, c), 0)
    col = lax.broadcasted_iota(jnp.int32, (c, c), 1)
    pr, pc = ((c - 1) - row, (c - 1) - col) if reverse else (row, col)
    levels = []
    for n in _GLA_LEVELS:
        sh, half = int(math.log2(n)), n // 2
        levels.append((lax.shift_right_logical(pr, sh) == lax.shift_right_logical(pc, sh))
                      & ((pr & (n - 1)) >= half) & ((pc & (n - 1)) < half))
    return levels, row == col


def _rope128(xh, cos, sin_signed):
    return xh * cos + pltpu.roll(xh, 64, 1) * sin_signed


def _scan_chunk_step(streams, consts, out_refs, sg_scr, sr_scr, lgs, masks, cc):
    wg_ref, bg_ref, mexp_ref, dmat_ref, qdec_ref, kdec_ref, sdec_ref = consts
    c = CHUNK
    n_chunks = SCAN_ROWS // CHUNK
    hq = GLA_HEADS * GLA_DK
    hr = RET_HEADS * RET_DK
    o_off = GLA_HEADS * GLA_DV
    tasks = []
    for d in (0, 1):
        zqk, zv, zrqk, zrv, _, cos_ref, sin_ref = streams[d]
        reverse = d == 1
        r0 = ((n_chunks - 1 - cc) if reverse else cc) * c
        rows = slice(r0, r0 + c)
        g = lgs[d][rows]
        g_hi = _mx(g)
        g_lo = _mx(g - g_hi.astype(F32))
        mexp = mexp_ref[d]
        w_all = jnp.exp(_dot(mexp, g_hi) + _dot(mexp, g_lo))
        cs, sn = cos_ref[rows, :], sin_ref[rows, :]
        for h in range(GLA_HEADS):
            lanes = slice(h * GLA_DK, (h + 1) * GLA_DK)
            tasks.append(dict(
                kind="gla", d=d, rows=rows, w=w_all[:, lanes],
                q=zqk[rows, lanes] * (GLA_DK ** -0.5),
                k=zqk[rows, hq + h * GLA_DK:hq + (h + 1) * GLA_DK],
                v=zv[rows, h * GLA_DV:(h + 1) * GLA_DV],
                st=sg_scr.at[d, h], cols=slice(h * GLA_DV, (h + 1) * GLA_DV)))
        for h in range(RET_HEADS):
            lanes = slice(h * RET_DK, (h + 1) * RET_DK)
            tasks.append(dict(
                kind="ret", d=d, rows=rows,
                q=_rope128(zrqk[rows, lanes], cs, sn),
                k=_rope128(zrqk[rows, hr + h * RET_DK:hr + (h + 1) * RET_DK]
                           * (RET_DK ** -0.5), cs, sn),
                v=zrv[rows, h * RET_DV:(h + 1) * RET_DV],
                dmat=dmat_ref[d, h], q_dec=qdec_ref[d][:, lanes], k_dec=kdec_ref[d][:, lanes],
                s_dec=sdec_ref[d][:, lanes], st=sr_scr.at[d, h],
                cols=slice(o_off + h * RET_DV, o_off + (h + 1) * RET_DV)))
    for t in tasks:
        t["s_t"] = t["st"][...]
        if t["kind"] == "gla":
            w, q, k = t["w"], t["q"], t["k"]
            t["inter"] = _dot_nt(_mx(q * w[0:c]), _mx(t["s_t"]))
            t["lvl"] = [_dot_nt(_mx(q * w[(2 + li) * c:(3 + li) * c]),
                                _mx(k * w[(2 + li) * c:(3 + li) * c]))
                        for li in range(len(_GLA_LEVELS))]
        else:
            t["inter"] = _dot_nt(_mx(t["q"] * t["q_dec"]), _mx(t["s_t"]))
            t["qk"] = _dot_nt(_mx(t["q"]), _mx(t["k"]))
    for t in tasks:
        if t["kind"] == "gla":
            level_masks, diag_mask = masks[t["d"]]
            a = jnp.where(diag_mask, jnp.sum(t["q"] * t["k"], axis=-1, keepdims=True), 0.0)
            for li in range(len(_GLA_LEVELS)):
                a = a + jnp.where(level_masks[li], t["lvl"][li], 0.0)
        else:
            a = t["qk"] * t["dmat"]
        t["a"] = _mx(a)
    for t in tasks:
        v = _mx(t["v"])
        out_refs[t["d"]][t["rows"], t["cols"]] = t["inter"] + _dot(t["a"], v)
        if t["kind"] == "gla":
            w = t["w"]
            decay = w[0:1, :] if t["d"] == 1 else w[c - 1:c, :]
            k_dec = t["k"] * w[c:2 * c]
        else:
            decay = t["s_dec"]
            k_dec = t["k"] * t["k_dec"]
        t["st"][...] = decay * t["s_t"] + _dot_tn(v, _mx(k_dec))


def _scan_kernel(*refs):
    streams, consts = (refs[0:7], refs[7:14]), refs[14:21]
    of_ref, ob_ref, sg_scr, sr_scr = refs[21:]
    wg_ref, bg_ref = consts[0], consts[1]

    @pl.when(pl.program_id(0) == 0)
    def _():
        sg_scr[...] = jnp.zeros_like(sg_scr)
        sr_scr[...] = jnp.zeros_like(sr_scr)

    lgs = [_log_sigmoid(_dot(_mx(streams[d][4][...]), wg_ref[d]) + bg_ref[d])
           * (1.0 / GLA_GATE_NORM) for d in (0, 1)]
    masks = [_gla_masks(False), _gla_masks(True)]
    for cc in range(SCAN_ROWS // CHUNK):
        _scan_chunk_step(streams, consts, (of_ref, ob_ref), sg_scr, sr_scr, lgs, masks, cc)


def _even_scan(z, cos_t, sin_t, wg, bg, n_seq_blocks):
    rows = z.shape[0]
    n_blocks = rows // SCAN_ROWS
    n_lat_blocks = n_seq_blocks - (256 // SCAN_ROWS)
    r = SCAN_ROWS

    def rf(i):
        return jnp.where(i < n_seq_blocks, (i + n_lat_blocks) % n_seq_blocks, i)

    def rb(i):
        return jnp.where(i < n_seq_blocks, n_seq_blocks - 1 - i, i)

    def stream_specs(rmap):
        return [
            pl.BlockSpec((r, 1024), lambda i: (rmap(i), 0)),
            pl.BlockSpec((r, 1024), lambda i: (rmap(i), 1)),
            pl.BlockSpec((r, 1024), lambda i: (rmap(i), 3)),
            pl.BlockSpec((r, 1024), lambda i: (rmap(i), 4)),
            pl.BlockSpec((r, LANES), lambda i: (rmap(i), 48)),
            pl.BlockSpec((r, LANES), lambda i: (rmap(i), 0)),
            pl.BlockSpec((r, LANES), lambda i: (rmap(i), 0)),
        ]

    mexp = jnp.asarray(np.stack([_gla_exponent_matrix(False), _gla_exponent_matrix(True)]),
                       MXU_DTYPE)
    rc = [_ret_constants(False), _ret_constants(True)]
    dmat = jnp.asarray(np.stack([rc[0][0], rc[1][0]]))
    qdec = jnp.asarray(np.stack([rc[0][1], rc[1][1]]))
    kdec = jnp.asarray(np.stack([rc[0][2], rc[1][2]]))
    sdec = jnp.asarray(np.stack([rc[0][3], rc[1][3]]))
    consts = [wg, bg, mexp, dmat, qdec, kdec, sdec]

    def full_spec(a):
        nd = a.ndim
        return pl.BlockSpec(a.shape, lambda i: (0,) * nd)

    d_out = GLA_HEADS * GLA_DV + RET_HEADS * RET_DV
    return pl.pallas_call(
        _scan_kernel,
        grid=(n_blocks,),
        in_specs=stream_specs(rf) + stream_specs(rb) + [full_spec(a) for a in consts],
        out_specs=[pl.BlockSpec((r, d_out), lambda i: (rf(i), 0)),
                   pl.BlockSpec((r, d_out), lambda i: (rb(i), 0))],
        out_shape=[jax.ShapeDtypeStruct((rows, d_out), F32)] * 2,
        scratch_shapes=[pltpu.VMEM((2, GLA_HEADS, GLA_DV, GLA_DK), F32),
                        pltpu.VMEM((2, RET_HEADS, RET_DV, RET_DK), F32)],
        compiler_params=_cparams(1),
        name="even_scan",
    )(z, z, z, z, z, cos_t, sin_t, z, z, z, z, z, cos_t, sin_t, *consts)


def _evgate_kernel(of_ref, ob_ref, gr_ref, rg_ref, nw_ref, a_ref):
    n_heads = GLA_HEADS + RET_HEADS
    for h in range(n_heads):
        lanes = slice(h * GLA_DV, (h + 1) * GLA_DV)
        oh = of_ref[:, lanes] + ob_ref[:, lanes]
        y = oh * lax.rsqrt(jnp.mean(oh * oh, axis=-1, keepdims=True) + RMS_EPS)
        if h < GLA_HEADS:
            y = y * nw_ref[...]
            gate = gr_ref[:, h * GLA_DV:(h + 1) * GLA_DV]
        else:
            hh = h - GLA_HEADS
            gate = rg_ref[:, hh * RET_DV:(hh + 1) * RET_DV]
        a_ref[:, lanes] = _mx(y * _silu(gate))


def _even_gate(o_f, o_b, z, norm_w):
    rows, d_out = o_f.shape
    tm = ROW_TILE
    return pl.pallas_call(
        _evgate_kernel,
        grid=(rows // tm,),
        in_specs=[
            pl.BlockSpec((tm, d_out), lambda i: (i, 0)),
            pl.BlockSpec((tm, d_out), lambda i: (i, 0)),
            pl.BlockSpec((tm, 1024), lambda i: (i, 2)),
            pl.BlockSpec((tm, 1024), lambda i: (i, 5)),
            pl.BlockSpec((1, GLA_DV), lambda i: (0, 0)),
        ],
        out_specs=pl.BlockSpec((tm, d_out), lambda i: (i, 0)),
        out_shape=jax.ShapeDtypeStruct((rows, d_out), MXU_DTYPE),
        compiler_params=_cparams(1),
        name="even_gate",
    )(o_f, o_b, z, z, norm_w.reshape(1, GLA_DV))


def _axial128(xh, cos, sin_a, sin_b):
    return xh * cos + pltpu.roll(xh, LANES - 16, 1) * sin_a + pltpu.roll(xh, 16, 1) * sin_b


def _axial_rows(xt, cos_t, sin_t):
    rot = jnp.concatenate([-xt[16:32], xt[0:16], -xt[48:64], xt[32:48]], axis=0)
    return xt * cos_t + rot * sin_t


def _mlaproj_kernel(h_ref, win_ref, qn_ref, kvn_ref, wuqt_ref, wuk_ref, wuvt_ref,
                    cos_ref, sina_ref, sinb_ref, cost_ref, sint_ref, qt_ref, k_ref, vt_ref):
    z = _dot(h_ref[...], win_ref[...])
    cq = z[:, :MLA_Q_RANK]
    ckv = z[:, MLA_Q_RANK:MLA_Q_RANK + MLA_KV_RANK]
    kr = z[:, MLA_Q_RANK + MLA_KV_RANK:]
    cos, sin_a, sin_b = cos_ref[...], sina_ref[...], sinb_ref[...]
    cqn = _mx(cq * lax.rsqrt(jnp.mean(cq * cq, axis=-1, keepdims=True) + RMS_EPS) * qn_ref[...])
    ckvn = _mx(ckv * lax.rsqrt(jnp.mean(ckv * ckv, axis=-1, keepdims=True) + RMS_EPS)
               * kvn_ref[...])
    qt = _dot_nt(wuqt_ref[...], cqn)
    kn = _dot(ckvn, wuk_ref[...])
    vt_ref[...] = _mx(_dot_nt(wuvt_ref[...], ckvn))
    kr_rot = _mx(_axial128(kr, cos, sin_a, sin_b))
    cos_t, sin_t = cost_ref[...], sint_ref[...]
    qs = MLA_SCALE * LOG2E
    r0, r1 = MLA_NOPE, MLA_NOPE + MLA_ROPE
    for h in range(MLA_HEADS):
        b0 = h * QK_PAD
        qt_ref[b0:b0 + r0, :] = _mx(qt[b0:b0 + r0] * qs)
        qt_ref[b0 + r0:b0 + r1, :] = _mx(_axial_rows(qt[b0 + r0:b0 + r1], cos_t, sin_t) * qs)
        qt_ref[b0 + r1:b0 + QK_PAD, :] = _mx(qt[b0 + r1:b0 + QK_PAD])
        k_ref[:, b0:b0 + LANES] = _mx(kn[:, h * MLA_NOPE:(h + 1) * MLA_NOPE])
        k_ref[:, b0 + LANES:b0 + QK_PAD] = kr_rot


def _mla_proj(h, w_in_p, q_norm, kv_norm, w_uqt, w_uk, w_uvt, tables):
    cos_t, sina_t, sinb_t, cos_rows, sin_rows = tables
    rows, d = h.shape
    tm = MLA_PROJ_ROWS
    hq = MLA_HEADS * QK_PAD
    hv = MLA_HEADS * MLA_DV

    def const2(a):
        return pl.BlockSpec(a.shape, lambda i: (0, 0))

    qn = q_norm.reshape(1, -1)
    kvn = kv_norm.reshape(1, -1)
    return pl.pallas_call(
        _mlaproj_kernel,
        grid=(rows // tm,),
        in_specs=[pl.BlockSpec((tm, d), lambda i: (i, 0)),
                  const2(w_in_p), const2(qn), const2(kvn), const2(w_uqt), const2(w_uk),
                  const2(w_uvt),
                  pl.BlockSpec((tm, LANES), lambda i: (i, 0)),
                  pl.BlockSpec((tm, LANES), lambda i: (i, 0)),
                  pl.BlockSpec((tm, LANES), lambda i: (i, 0)),
                  pl.BlockSpec((MLA_ROPE, tm), lambda i: (0, i)),
                  pl.BlockSpec((MLA_ROPE, tm), lambda i: (0, i))],
        out_specs=[pl.BlockSpec((hq, tm), lambda i: (0, i)),
                   pl.BlockSpec((tm, hq), lambda i: (i, 0)),
                   pl.BlockSpec((hv, tm), lambda i: (0, i))],
        out_shape=[jax.ShapeDtypeStruct((hq, rows), MXU_DTYPE),
                   jax.ShapeDtypeStruct((rows, hq), MXU_DTYPE),
                   jax.ShapeDtypeStruct((hv, rows), MXU_DTYPE)],
        compiler_params=_cparams(1),
        name="mla_proj",
    )(h, w_in_p, qn, kvn, w_uqt, w_uk, w_uvt, cos_t, sina_t, sinb_t, cos_rows, sin_rows)


def _flash_kernel(qt_ref, k_ref, vt_ref, o_ref, *, n_lat, n_ctx):
    tq = qt_ref.shape[1]
    qt = qt_ref[...]

    def scores(start, size):
        return _dot(k_ref[start:start + size, :], qt)

    def store(acc, l):
        o_ref[...] = (acc / l).T.astype(o_ref.dtype)

    def attend_online(kv_slices):
        m = jnp.full((1, tq), -jnp.inf, F32)
        l = jnp.zeros((1, tq), F32)
        acc = jnp.zeros((MLA_DV, tq), F32)
        for start, size in kv_slices:
            s = scores(start, size)
            m_new = jnp.maximum(m, jnp.max(s, axis=0, keepdims=True))
            alpha = jnp.exp2(m - m_new)
            p = jnp.exp2(s - m_new)
            l = alpha * l + jnp.sum(p, axis=0, keepdims=True)
            acc = alpha * acc + _dot(vt_ref[:, start:start + size], _mx(p))
            m = m_new
        store(acc, l)

    def attend_fixed_shift(kv_slices):
        m = jnp.max(scores(kv_slices[0][0], SHIFT_KEYS), axis=0, keepdims=True)
        l = jnp.zeros((1, tq), F32)
        acc = jnp.zeros((MLA_DV, tq), F32)
        for start, size in kv_slices:
            p = jnp.exp2(scores(start, size) - m)
            l = l + jnp.sum(p, axis=0, keepdims=True)
            acc = acc + _dot(vt_ref[:, start:start + size], _mx(p))
        store(acc, l)
        n_bad = (jnp.sum(jnp.where(l < FIXED_SHIFT_LIMIT, 0.0, 1.0))
                 + jnp.sum(jnp.where(jnp.abs(acc) < FIXED_SHIFT_LIMIT, 0.0, 1.0)))
        return n_bad == 0.0

    n_kv = n_lat + n_ctx
    i = pl.program_id(1)
    lat_slices = [(s0, KV_TILE) for s0 in range(0, n_kv, KV_TILE)]

    @pl.when(i > 0)
    def _():
        ok = attend_fixed_shift(lat_slices)

        @pl.when(jnp.logical_not(ok))
        def _():
            attend_online(lat_slices)

    @pl.when(i == 0)
    def _():
        attend_online([(n_lat, n_ctx)])


def _mla_attention(qt, k, vt, *, n_lat, n_ctx, n_lat_tiles):
    rows = k.shape[0]
    n_kv = n_lat + n_ctx
    tm = ROW_TILE
    kern = functools.partial(_flash_kernel, n_lat=n_lat, n_ctx=n_ctx)
    n_tiles = rows // tm
    assert n_tiles == n_lat_tiles + 1

    def tile(i):
        return (i + n_lat_tiles) % n_tiles

    return pl.pallas_call(
        kern,
        grid=(MLA_HEADS, n_tiles),
        in_specs=[pl.BlockSpec((QK_PAD, tm), lambda h, i: (h, tile(i))),
                  pl.BlockSpec((n_kv, QK_PAD), lambda h, i: (0, h)),
                  pl.BlockSpec((MLA_DV, n_kv), lambda h, i: (h, 0))],
        out_specs=pl.BlockSpec((tm, MLA_DV), lambda h, i: (tile(i), h)),
        out_shape=jax.ShapeDtypeStruct((rows, MLA_HEADS * MLA_DV), MXU_DTYPE),
        compiler_params=_cparams(2),
        name="mla_flash",
    )(qt, k, vt)


def _prep_even(w_in, wg_f, bg_f, wg_b, bg_b):
    gqk, gv, gr = w_in[:, 0:1024], w_in[:, 1024:2048], w_in[:, 2048:3072]
    gd = w_in[:, 3072:3104]
    rqk, rv, rg = w_in[:, 3104:4128], w_in[:, 4128:5152], w_in[:, 5152:6176]
    d = w_in.shape[0]
    w = jnp.concatenate([gqk, gv, gr, rqk, rv, rg, gd, jnp.zeros((d, 256 - 32), w_in.dtype)],
                        axis=1)
    r = GLA_GATE_RANK
    hq = GLA_HEADS * GLA_DK
    wg = jnp.zeros((2, LANES, hq), F32)
    wg = wg.at[0, 0:r].set(wg_f).at[1, r:2 * r].set(wg_b)
    bg = jnp.stack([bg_f, bg_b]).reshape(2, 1, hq)
    return _mx(w), _mx(wg), bg


def _prep_mla(w_in, w_uq, w_ukv):
    d = w_in.shape[0]
    w_in_p = _mx(jnp.pad(w_in, ((0, 0), (0, 1152 - w_in.shape[1]))))
    r = w_uq.shape[0]
    uq = w_uq.reshape(r, MLA_HEADS, MLA_NOPE + MLA_ROPE)
    uq = jnp.pad(uq, ((0, 0), (0, 0), (0, QK_PAD - MLA_NOPE - MLA_ROPE)))
    w_uqt = _mx(uq.reshape(r, MLA_HEADS * QK_PAD).T)
    ukv = w_ukv.reshape(w_ukv.shape[0], MLA_HEADS, MLA_NOPE + MLA_DV)
    w_uk = _mx(ukv[:, :, :MLA_NOPE].reshape(-1, MLA_HEADS * MLA_NOPE))
    w_uvt = _mx(ukv[:, :, MLA_NOPE:].reshape(-1, MLA_HEADS * MLA_DV).T)
    return w_in_p, w_uqt, w_uk, w_uvt


def _rope_angles(pos, dim):
    inv = ROPE_BASE ** (-jnp.arange(0, dim, 2, dtype=F32) / dim)
    ang = pos.astype(F32)[:, None] * inv[None, :]
    return jnp.concatenate([ang, ang], axis=-1)


def _ret_tables(n_lat, rows):
    ang = _rope_angles(jnp.arange(n_lat), RET_DK)
    cos = jnp.ones((rows, RET_DK), F32).at[:n_lat].set(jnp.cos(ang))
    sin = jnp.sin(ang)
    sign = jnp.where(jnp.arange(RET_DK) < RET_DK // 2, -1.0, 1.0)
    sin_s = jnp.zeros((rows, RET_DK), F32).at[:n_lat].set(sin * sign)
    return cos, sin_s


def _axial_tables(n_lat, rows):
    n_rows = n_lat // GRID_W
    row = jnp.repeat(jnp.arange(n_rows), GRID_W)
    col = jnp.tile(jnp.arange(GRID_W), n_rows)
    half = MLA_ROPE // 2
    ang = jnp.concatenate([_rope_angles(row, half), _rope_angles(col, half)], axis=-1)
    lane = jnp.arange(MLA_ROPE)
    lo = (lane % half) < (half // 2)
    cos = jnp.ones((rows, LANES), F32).at[:n_lat, :MLA_ROPE].set(jnp.cos(ang))
    sin = jnp.sin(ang)
    sin_a = jnp.zeros((rows, LANES), F32).at[:n_lat, :MLA_ROPE].set(jnp.where(lo, -sin, 0.0))
    sin_b = jnp.zeros((rows, LANES), F32).at[:n_lat, :MLA_ROPE].set(jnp.where(lo, 0.0, sin))
    cos_rows = jnp.ones((MLA_ROPE, rows), F32).at[:, :n_lat].set(jnp.cos(ang).T)
    sin_rows = jnp.zeros((MLA_ROPE, rows), F32).at[:, :n_lat].set(sin.T)
    return cos, sin_a, sin_b, cos_rows, sin_rows


def kernel(x, c, ctx, c_ctx, w_ada, b_ada, ln_g, ln_b, w_ffn_in, w_ffn_out, ev_w_in, ev_gla_wg_f, ev_gla_bg_f, ev_gla_wg_b, ev_gla_bg_b, ev_gla_norm, ev_w_out, od_w_in, od_q_norm, od_kv_norm, od_w_uq, od_w_ukv, od_w_out):
    n_lat, d = x.shape[1], x.shape[2]
    n_ctx = ctx.shape[1]
    depth = w_ada.shape[0]
    assert x.shape[0] == 1 and n_lat % ROW_TILE == 0 and n_ctx == SCAN_ROWS
    assert (n_lat + n_ctx) % KV_TILE == 0
    alpha = (2 * depth) ** 0.25
    n_lat_tiles = n_lat // ROW_TILE
    n_tiles = n_lat_tiles + 1
    rows = n_tiles * ROW_TILE
    n_seq_blocks = (n_lat + n_ctx) // SCAN_ROWS

    x_all = x[0]
    x_tail = jnp.concatenate([ctx[0], jnp.zeros((ROW_TILE - n_ctx, d), x.dtype)], axis=0)
    c_rows = jnp.zeros((16, d), F32).at[0].set(c[0]).at[1].set(c_ctx)
    mods = _ada(c_rows, w_ada, b_ada)[:, :2].reshape(depth, 2, N_MOD, d)

    ret_cos, ret_sin = _ret_tables(n_lat, rows)
    ax_tables = _axial_tables(n_lat, rows)

    wi, wo = _mx(w_ffn_in), _mx(w_ffn_out)
    for l in range(depth):
        last = l == depth - 1
        x_all, h_mix = _ffn(x_all, mods[l], wi, wo, ln_g[l, 0], ln_b[l, 0], layer=l, half=0,
                            mod_row=0, emit_row=3, alpha=alpha, n_lat_tiles=n_lat_tiles,
                            n_tiles=n_tiles, x_tail=x_tail if l == 0 else None)
        if l % 2 == 0:
            e = l // 2
            w_p, wg, bg = _prep_even(ev_w_in[e], ev_gla_wg_f[e], ev_gla_bg_f[e],
                                     ev_gla_wg_b[e], ev_gla_bg_b[e])
            z = _proj(h_mix, w_p, 1280)
            o_f, o_b = _even_scan(z, ret_cos, ret_sin, wg, bg, n_seq_blocks)
            a = _even_gate(o_f, o_b, z, ev_gla_norm[e])
            w_out = _mx(ev_w_out[e])
        else:
            o = l // 2
            w_in_p, w_uqt, w_uk, w_uvt = _prep_mla(od_w_in[o], od_w_uq[o], od_w_ukv[o])
            qt, k, vt = _mla_proj(h_mix, w_in_p, od_q_norm[o], od_kv_norm[o], w_uqt, w_uk,
                                  w_uvt, ax_tables)
            a = _mla_attention(qt, k, vt, n_lat=n_lat, n_ctx=n_ctx, n_lat_tiles=n_lat_tiles)
            w_out = _mx(od_w_out[o])
        x_all = _outproj_ln(a, w_out, x_all, mods[l], ln_g[l, 1], ln_b[l, 1], gate_row=5,
                            alpha=alpha, n_lat_tiles=n_lat_tiles)
        x_all, _ = _ffn(x_all, mods[l], wi, wo, ln_g[l, 2], ln_b[l, 2], layer=l, half=1,
                        mod_row=6, emit_row=None, alpha=alpha, n_lat_tiles=n_lat_tiles,
                        n_tiles=n_lat_tiles if last else n_tiles)
    return x_all[None]
```

```python
import functools
import math

import numpy as np
import jax
import jax.numpy as jnp
from jax import lax
from jax.experimental import pallas as pl
from jax.experimental.pallas import tpu as pltpu

F32 = jnp.float32
MXU_DTYPE = jnp.bfloat16

N_MOD = 9
LN_EPS = 1e-5
RMS_EPS = 1e-6
CHUNK = 64
ROPE_BASE = 10000.0
GRID_W = 64
GLA_HEADS = 4
GLA_DK = 128
GLA_DV = 256
GLA_GATE_RANK = 16
GLA_GATE_NORM = 16.0
RET_HEADS = 4
RET_DK = 128
RET_DV = 256
MLA_HEADS = 16
MLA_Q_RANK = 512
MLA_KV_RANK = 512
MLA_NOPE = 128
MLA_ROPE = 64
MLA_DV = 128
MLA_SCALE = (MLA_NOPE + MLA_ROPE) ** -0.5
LOG2E = 1.4426950408889634
FIXED_SHIFT_LIMIT = 2.0 ** 100

LANES = 128
ROW_TILE = 512
FF_CHUNK = 1024
LN_SLICE = 128
SCAN_ROWS = 256
MLA_PROJ_ROWS = 256
KV_TILE = 1280
SHIFT_KEYS = 256
QK_PAD = 256
VMEM_LIMIT = 56 * 1024 * 1024


def _cparams(n_axes, vmem=VMEM_LIMIT):
    return pltpu.CompilerParams(
        dimension_semantics=("arbitrary",) * n_axes, vmem_limit_bytes=vmem)


def _mx(a):
    return a.astype(MXU_DTYPE)


def _dot(a, b):
    return jnp.dot(a, b, preferred_element_type=F32)


def _dot_nt(a, b):
    return lax.dot_general(a, b, (((1,), (1,)), ((), ())), preferred_element_type=F32)


def _dot_tn(a, b):
    return lax.dot_general(a, b, (((0,), (0,)), ((), ())), preferred_element_type=F32)


def _silu(v):
    return v / (1.0 + jnp.exp(-v))


def _layer_norm(v, g, b):
    mu = jnp.mean(v, axis=-1, keepdims=True)
    d = v - mu
    var = jnp.mean(d * d, axis=-1, keepdims=True)
    return d * lax.rsqrt(var + LN_EPS) * g + b


def _ada_kernel(c_ref, w_ref, b_ref, o_ref):
    s = _silu(c_ref[...])
    o_ref[...] = _dot(_mx(s), _mx(w_ref[...])) + b_ref[...]


def _ada(c_rows, w_ada, b_ada):
    depth, d, n = w_ada.shape
    rows = c_rows.shape[0]
    tn = 1024
    return pl.pallas_call(
        _ada_kernel,
        grid=(depth, n // tn),
        in_specs=[
            pl.BlockSpec((rows, d), lambda l, j: (0, 0)),
            pl.BlockSpec((None, d, tn), lambda l, j: (l, 0, j)),
            pl.BlockSpec((None, 1, tn), lambda l, j: (l, 0, j)),
        ],
        out_specs=pl.BlockSpec((None, rows, tn), lambda l, j: (l, 0, j)),
        out_shape=jax.ShapeDtypeStruct((depth, rows, n), F32),
        compiler_params=_cparams(2),
        name="ada_mod",
    )(c_rows, w_ada, b_ada.reshape(depth, 1, n))


def _ffn_kernel(*refs, mod_row, emit_row, alpha, overlap, n_tiles, tail_tile):
    if tail_tile is None:
        x_ref, xt_ref = refs[0], None
        mod_ref, wg_ref, wu_ref, wout_ref, g_ref, b_ref = refs[1:7]
        rest = refs[7:]
    else:
        x_ref, xt_ref, mod_ref, wg_ref, wu_ref, wout_ref, g_ref, b_ref = refs[:8]
        rest = refs[8:]
    if emit_row is None:
        o_ref, xm_scr, acc_scr, v_scr, em_scr = rest
        h_ref = None
    else:
        o_ref, h_ref, xm_scr, acc_scr, v_scr, em_scr = rest
    i, k = pl.program_id(0), pl.program_id(1)
    last_k = pl.num_programs(1) - 1
    n_slices = v_scr.shape[0] // LN_SLICE

    def read_x():
        if xt_ref is None:
            return x_ref[...]
        return jnp.where(i == tail_tile, xt_ref[...], x_ref[...])

    def finish_rows():
        r0 = pl.multiple_of(jnp.minimum(k, n_slices - 1) * LN_SLICE, LN_SLICE)
        y = _layer_norm(v_scr[pl.ds(r0, LN_SLICE), :], g_ref[...], b_ref[...])
        o_ref[...] = y
        if h_ref is not None:
            h_ref[...] = _mx(y * (1.0 + em_scr[1:2, :]) + em_scr[0:1, :])

    def accumulate(skip, first=False):
        xm = xm_scr[...]
        act = _silu(_dot(xm, wg_ref[:, skip:])) * _dot(xm, wu_ref[:, skip:])
        y = _dot(_mx(act), wout_ref[skip:, :])
        if first:
            acc_scr[...] = y
        else:
            acc_scr[...] += y

    @pl.when((i == 0) & (k == 0))
    def _():
        v_scr[...] = jnp.zeros_like(v_scr)
        em_scr[...] = jnp.zeros_like(em_scr)

    @pl.when((i < n_tiles) & (k == 0))
    def _():
        shift = mod_ref[mod_row:mod_row + 1, :]
        scale = mod_ref[mod_row + 1:mod_row + 2, :]
        xm_scr[...] = _mx(read_x() * (1.0 + scale) + shift)
        accumulate(0, first=True)
        finish_rows()

    @pl.when((i < n_tiles) & (k > 0) & (k < last_k))
    def _():
        accumulate(0)
        finish_rows()

    @pl.when((i < n_tiles) & (k == last_k))
    def _():
        accumulate(overlap)
        finish_rows()
        gate = mod_ref[mod_row + 2:mod_row + 3, :]
        v_scr[...] = alpha * read_x() + 0.5 * gate * acc_scr[...]
        if h_ref is not None:
            em_scr[...] = mod_ref[emit_row:emit_row + 2, :]

    @pl.when(i == n_tiles)
    def _():
        finish_rows()


def _ffn(x_all, mods, w_in, w_out, ln_g, ln_b, *, layer, half, mod_row, emit_row, alpha,
         n_lat_tiles, n_tiles, x_tail=None):
    d = x_all.shape[1]
    f = w_out.shape[2]
    tm, tk = ROW_TILE, FF_CHUNK
    n_chunks = -(-f // tk)
    overlap = n_chunks * tk - f
    rows = n_tiles * tm

    assert f % LANES == 0 and tk % LANES == 0 and n_chunks >= tm // LN_SLICE

    def win(i, k, base=0):
        k = jnp.where(i < n_tiles, k, n_chunks - 1)
        return (jnp.minimum(k * (tk // LANES), (f - tk) // LANES) + base // LANES) * LANES

    def src(i):
        return jnp.minimum(i, n_tiles - 1)

    n_slices = tm // LN_SLICE

    def dst(i, k):
        return jnp.where(i == 0, 0, (i - 1) * n_slices + jnp.minimum(k, n_slices - 1))

    out_shape = [jax.ShapeDtypeStruct((rows, d), F32)]
    out_specs = [pl.BlockSpec((LN_SLICE, d), lambda i, k: (dst(i, k), 0))]
    if emit_row is not None:
        out_shape.append(jax.ShapeDtypeStruct((rows, d), MXU_DTYPE))
        out_specs.append(pl.BlockSpec((LN_SLICE, d), lambda i, k: (dst(i, k), 0)))
    tail_tile = None if x_tail is None else n_tiles - 1
    kern = functools.partial(_ffn_kernel, mod_row=mod_row, emit_row=emit_row, alpha=alpha,
                             overlap=overlap, n_tiles=n_tiles, tail_tile=tail_tile)
    if x_tail is None:
        x_args = [x_all]
        x_specs = [pl.BlockSpec((tm, d), lambda i, k: (src(i), 0))]
    else:
        x_args = [x_all, x_tail]
        x_specs = [pl.BlockSpec((tm, d), lambda i, k: (jnp.minimum(i, tail_tile - 1), 0)),
                   pl.BlockSpec((tm, d), lambda i, k: (0, 0))]
    res = pl.pallas_call(
        kern,
        grid=(n_tiles + 1, n_chunks),
        in_specs=x_specs + [
            pl.BlockSpec((None, N_MOD, d), lambda i, k: (src(i) // n_lat_tiles, 0, 0)),
            pl.BlockSpec((None, None, pl.Element(d), pl.Element(tk)),
                         lambda i, k: (layer, half, 0, win(i, k))),
            pl.BlockSpec((None, None, pl.Element(d), pl.Element(tk)),
                         lambda i, k: (layer, half, 0, win(i, k, f))),
            pl.BlockSpec((None, None, pl.Element(tk), pl.Element(d)),
                         lambda i, k: (layer, half, win(i, k), 0)),
            pl.BlockSpec((1, d), lambda i, k: (0, 0)),
            pl.BlockSpec((1, d), lambda i, k: (0, 0)),
        ],
        out_specs=out_specs,
        out_shape=out_shape,
        scratch_shapes=[pltpu.VMEM((tm, d), MXU_DTYPE), pltpu.VMEM((tm, d), F32),
                        pltpu.VMEM((tm, d), F32), pltpu.VMEM((2, d), F32)],
        compiler_params=_cparams(2),
        name="ffn",
    )(*x_args, mods, w_in, w_in, w_out, ln_g.reshape(1, d), ln_b.reshape(1, d))
    return res if emit_row is not None else (res[0], None)


def _proj_kernel(h_ref, w_ref, o_ref):
    o_ref[...] = _dot(h_ref[...], w_ref[...])


def _proj(h, w, tn):
    rows, kdim = h.shape
    n = w.shape[1]
    tm = ROW_TILE
    return pl.pallas_call(
        _proj_kernel,
        grid=(n // tn, rows // tm),
        in_specs=[
            pl.BlockSpec((tm, kdim), lambda j, i: (i, 0)),
            pl.BlockSpec((kdim, tn), lambda j, i: (0, j)),
        ],
        out_specs=pl.BlockSpec((tm, tn), lambda j, i: (i, j)),
        out_shape=jax.ShapeDtypeStruct((rows, n), F32),
        compiler_params=_cparams(2),
        name="even_proj",
    )(h, w)


def _outproj_kernel(a_ref, w_ref, x_ref, mod_ref, g_ref, b_ref, o_ref, *, gate_row, alpha):
    y = _dot(a_ref[...], w_ref[...])
    gate = mod_ref[gate_row:gate_row + 1, :]
    v = alpha * x_ref[...] + gate * y
    o_ref[...] = _layer_norm(v, g_ref[...], b_ref[...])


def _outproj_ln(a, w, x_all, mods, ln_g, ln_b, *, gate_row, alpha, n_lat_tiles):
    rows, d = x_all.shape
    kdim = a.shape[1]
    tm = ROW_TILE
    kern = functools.partial(_outproj_kernel, gate_row=gate_row, alpha=alpha)
    return pl.pallas_call(
        kern,
        grid=(rows // tm,),
        in_specs=[
            pl.BlockSpec((tm, kdim), lambda i: (i, 0)),
            pl.BlockSpec((kdim, d), lambda i: (0, 0)),
            pl.BlockSpec((tm, d), lambda i: (i, 0)),
            pl.BlockSpec((None, N_MOD, d), lambda i: (i // n_lat_tiles, 0, 0)),
            pl.BlockSpec((1, d), lambda i: (0, 0)),
            pl.BlockSpec((1, d), lambda i: (0, 0)),
        ],
        out_specs=pl.BlockSpec((tm, d), lambda i: (i, 0)),
        out_shape=jax.ShapeDtypeStruct((rows, d), F32),
        compiler_params=_cparams(1),
        name="mix_out",
    )(a, w, x_all, mods, ln_g.reshape(1, d), ln_b.reshape(1, d))


_GLA_LEVELS = (64, 32, 16, 8, 4, 2)


def _scan_positions(reverse):
    idx = np.arange(CHUNK)
    return idx[::-1].copy() if reverse else idx


def _gla_exponent_matrix(reverse):
    pos = _scan_positions(reverse)
    pi, pt = pos[:, None], pos[None, :]
    mats = [pt <= pi,
            pt > pi]
    for n in _GLA_LEVELS:
        half = n // 2
        ref = (pi // n) * n + half - 1
        q_side = (pi % n) >= half
        mats.append(np.where(q_side, (pt > ref) & (pt <= pi), (pt > pi) & (pt <= ref)))
    return np.concatenate(mats, axis=0).astype(np.float32)


def _ret_constants(reverse):
    scales = np.arange(RET_HEADS, dtype=np.float64)
    off = 5.5 if reverse else 5.0
    log_gamma = np.log1p(-np.exp2(-off - scales))
    idx = np.arange(CHUNK, dtype=np.float64)
    diff = idx[:, None] - idx[None, :]
    dmat = np.where(diff >= 0, np.exp(log_gamma[:, None, None] * np.maximum(diff, 0.0)), 0.0)
    q_dec = np.exp(log_gamma[:, None] * (idx + 1.0))
    k_dec = np.exp(log_gamma[:, None] * (CHUNK - 1.0 - idx))
    s_dec = np.exp(log_gamma * CHUNK)
    if reverse:
        dmat = dmat[:, ::-1, ::-1]
        q_dec = q_dec[:, ::-1]
        k_dec = k_dec[:, ::-1]
    expand = lambda a: np.repeat(a.T, RET_DK, axis=1)
    return (dmat.astype(np.float32), expand(q_dec).astype(np.float32),
            expand(k_dec).astype(np.float32),
            np.repeat(s_dec, RET_DK)[None, :].astype(np.float32))


def _log_sigmoid(v):
    return jnp.minimum(v, 0.0) - jnp.log1p(jnp.exp(-jnp.abs(v)))


def _gla_masks(reverse):
    c = CHUNK
    row = lax.broadcasted_iota(jnp.int32, (c, c), 0)
    col = lax.broadcasted_iota(jnp.int32, (c, c), 1)
    pr, pc = ((c - 1) - row, (c - 1) - col) if reverse else (row, col)
    levels = []
    for n in _GLA_LEVELS:
        sh, half = int(math.log2(n)), n // 2
        levels.append((lax.shift_right_logical(pr, sh) == lax.shift_right_logical(pc, sh))
                      & ((pr & (n - 1)) >= half) & ((pc & (n - 1)) < half))
    return levels, row == col


def _rope128(xh, cos, sin_signed):
    return xh * cos + pltpu.roll(xh, 64, 1) * sin_signed


def _scan_chunk_step(streams, consts, out_refs, sg_scr, sr_scr, lgs, masks, cc):
    wg_ref, bg_ref, mexp_ref, dmat_ref, qdec_ref, kdec_ref, sdec_ref = consts
    c = CHUNK
    n_chunks = SCAN_ROWS // CHUNK
    hq = GLA_HEADS * GLA_DK
    hr = RET_HEADS * RET_DK
    o_off = GLA_HEADS * GLA_DV
    tasks = []
    for d in (0, 1):
        zqk, zv, zrqk, zrv, _, cos_ref, sin_ref = streams[d]
        reverse = d == 1
        r0 = ((n_chunks - 1 - cc) if reverse else cc) * c
        rows = slice(r0, r0 + c)
        g = lgs[d][rows]
        g_hi = _mx(g)
        g_lo = _mx(g - g_hi.astype(F32))
        mexp = mexp_ref[d]
        w_all = jnp.exp(_dot(mexp, g_hi) + _dot(mexp, g_lo))
        cs, sn = cos_ref[rows, :], sin_ref[rows, :]
        for h in range(GLA_HEADS):
            lanes = slice(h * GLA_DK, (h + 1) * GLA_DK)
            tasks.append(dict(
                kind="gla", d=d, rows=rows, w=w_all[:, lanes],
                q=zqk[rows, lanes] * (GLA_DK ** -0.5),
                k=zqk[rows, hq + h * GLA_DK:hq + (h + 1) * GLA_DK],
                v=zv[rows, h * GLA_DV:(h + 1) * GLA_DV],
                st=sg_scr.at[d, h], cols=slice(h * GLA_DV, (h + 1) * GLA_DV)))
        for h in range(RET_HEADS):
            lanes = slice(h * RET_DK, (h + 1) * RET_DK)
            tasks.append(dict(
                kind="ret", d=d, rows=rows,
                q=_rope128(zrqk[rows, lanes], cs, sn),
                k=_rope128(zrqk[rows, hr + h * RET_DK:hr + (h + 1) * RET_DK]
                           * (RET_DK ** -0.5), cs, sn),
                v=zrv[rows, h * RET_DV:(h + 1) * RET_DV],
                dmat=dmat_ref[d, h], q_dec=qdec_ref[d][:, lanes], k_dec=kdec_ref[d][:, lanes],
                s_dec=sdec_ref[d][:, lanes], st=sr_scr.at[d, h],
                cols=slice(o_off + h * RET_DV, o_off + (h + 1) * RET_DV)))
    for t in tasks:
        t["s_t"] = t["st"][...]
        if t["kind"] == "gla":
            w, q, k = t["w"], t["q"], t["k"]
            t["inter"] = _dot_nt(_mx(q * w[0:c]), _mx(t["s_t"]))
            t["lvl"] = [_dot_nt(_mx(q * w[(2 + li) * c:(3 + li) * c]),
                                _mx(k * w[(2 + li) * c:(3 + li) * c]))
                        for li in range(len(_GLA_LEVELS))]
        else:
            t["inter"] = _dot_nt(_mx(t["q"] * t["q_dec"]), _mx(t["s_t"]))
            t["qk"] = _dot_nt(_mx(t["q"]), _mx(t["k"]))
    for t in tasks:
        if t["kind"] == "gla":
            level_masks, diag_mask = masks[t["d"]]
            a = jnp.where(diag_mask, jnp.sum(t["q"] * t["k"], axis=-1, keepdims=True), 0.0)
            for li in range(len(_GLA_LEVELS)):
                a = a + jnp.where(level_masks[li], t["lvl"][li], 0.0)
        else:
            a = t["qk"] * t["dmat"]
        t["a"] = _mx(a)
    for t in tasks:
        v = _mx(t["v"])
        out_refs[t["d"]][t["rows"], t["cols"]] = t["inter"] + _dot(t["a"], v)
        if t["kind"] == "gla":
            w = t["w"]
            decay = w[0:1, :] if t["d"] == 1 else w[c - 1:c, :]
            k_dec = t["k"] * w[c:2 * c]
        else:
            decay = t["s_dec"]
            k_dec = t["k"] * t["k_dec"]
        t["st"][...] = decay * t["s_t"] + _dot_tn(v, _mx(k_dec))


def _scan_kernel(*refs):
    streams, consts = (refs[0:7], refs[7:14]), refs[14:21]
    of_ref, ob_ref, sg_scr, sr_scr = refs[21:]
    wg_ref, bg_ref = consts[0], consts[1]

    @pl.when(pl.program_id(0) == 0)
    def _():
        sg_scr[...] = jnp.zeros_like(sg_scr)
        sr_scr[...] = jnp.zeros_like(sr_scr)

    lgs = [_log_sigmoid(_dot(_mx(streams[d][4][...]), wg_ref[d]) + bg_ref[d])
           * (1.0 / GLA_GATE_NORM) for d in (0, 1)]
    masks = [_gla_masks(False), _gla_masks(True)]
    for cc in range(SCAN_ROWS // CHUNK):
        _scan_chunk_step(streams, consts, (of_ref, ob_ref), sg_scr, sr_scr, lgs, masks, cc)


def _even_scan(z, cos_t, sin_t, wg, bg, n_seq_blocks):
    rows = z.shape[0]
    n_blocks = rows // SCAN_ROWS
    n_lat_blocks = n_seq_blocks - (256 // SCAN_ROWS)
    r = SCAN_ROWS

    def rf(i):
        return jnp.where(i < n_seq_blocks, (i + n_lat_blocks) % n_seq_blocks, i)

    def rb(i):
        return jnp.where(i < n_seq_blocks, n_seq_blocks - 1 - i, i)

    def stream_specs(rmap):
        return [
            pl.BlockSpec((r, 1024), lambda i: (rmap(i), 0)),
            pl.BlockSpec((r, 1024), lambda i: (rmap(i), 1)),
            pl.BlockSpec((r, 1024), lambda i: (rmap(i), 3)),
            pl.BlockSpec((r, 1024), lambda i: (rmap(i), 4)),
            pl.BlockSpec((r, LANES), lambda i: (rmap(i), 48)),
            pl.BlockSpec((r, LANES), lambda i: (rmap(i), 0)),
            pl.BlockSpec((r, LANES), lambda i: (rmap(i), 0)),
        ]

    mexp = jnp.asarray(np.stack([_gla_exponent_matrix(False), _gla_exponent_matrix(True)]),
                       MXU_DTYPE)
    rc = [_ret_constants(False), _ret_constants(True)]
    dmat = jnp.asarray(np.stack([rc[0][0], rc[1][0]]))
    qdec = jnp.asarray(np.stack([rc[0][1], rc[1][1]]))
    kdec = jnp.asarray(np.stack([rc[0][2], rc[1][2]]))
    sdec = jnp.asarray(np.stack([rc[0][3], rc[1][3]]))
    consts = [wg, bg, mexp, dmat, qdec, kdec, sdec]

    def full_spec(a):
        nd = a.ndim
        return pl.BlockSpec(a.shape, lambda i: (0,) * nd)

    d_out = GLA_HEADS * GLA_DV + RET_HEADS * RET_DV
    return pl.pallas_call(
        _scan_kernel,
        grid=(n_blocks,),
        in_specs=stream_specs(rf) + stream_specs(rb) + [full_spec(a) for a in consts],
        out_specs=[pl.BlockSpec((r, d_out), lambda i: (rf(i), 0)),
                   pl.BlockSpec((r, d_out), lambda i: (rb(i), 0))],
        out_shape=[jax.ShapeDtypeStruct((rows, d_out), F32)] * 2,
        scratch_shapes=[pltpu.VMEM((2, GLA_HEADS, GLA_DV, GLA_DK), F32),
                        pltpu.VMEM((2, RET_HEADS, RET_DV, RET_DK), F32)],
        compiler_params=_cparams(1),
        name="even_scan",
    )(z, z, z, z, z, cos_t, sin_t, z, z, z, z, z, cos_t, sin_t, *consts)


def _evout_kernel(of_ref, ob_ref, gr_ref, rg_ref, nw_ref, w_ref, x_ref, mod_ref, g_ref, b_ref,
                  o_ref, *, gate_row, alpha):
    parts = []
    for h in range(GLA_HEADS + RET_HEADS):
        lanes = slice(h * GLA_DV, (h + 1) * GLA_DV)
        oh = of_ref[:, lanes] + ob_ref[:, lanes]
        y = oh * lax.rsqrt(jnp.mean(oh * oh, axis=-1, keepdims=True) + RMS_EPS)
        if h < GLA_HEADS:
            y = y * nw_ref[...]
            gate = gr_ref[:, h * GLA_DV:(h + 1) * GLA_DV]
        else:
            hh = h - GLA_HEADS
            gate = rg_ref[:, hh * RET_DV:(hh + 1) * RET_DV]
        parts.append(_mx(y * _silu(gate)))
    y = _dot(jnp.concatenate(parts, axis=1), w_ref[...])
    v = alpha * x_ref[...] + mod_ref[gate_row:gate_row + 1, :] * y
    o_ref[...] = _layer_norm(v, g_ref[...], b_ref[...])


def _even_out(o_f, o_b, z, norm_w, w, x_all, mods, ln_g, ln_b, *, gate_row, alpha, n_lat_tiles):
    rows, d = x_all.shape
    d_out = o_f.shape[1]
    tm = ROW_TILE
    kern = functools.partial(_evout_kernel, gate_row=gate_row, alpha=alpha)
    return pl.pallas_call(
        kern,
        grid=(rows // tm,),
        in_specs=[
            pl.BlockSpec((tm, d_out), lambda i: (i, 0)),
            pl.BlockSpec((tm, d_out), lambda i: (i, 0)),
            pl.BlockSpec((tm, 1024), lambda i: (i, 2)),
            pl.BlockSpec((tm, 1024), lambda i: (i, 5)),
            pl.BlockSpec((1, GLA_DV), lambda i: (0, 0)),
            pl.BlockSpec((d_out, d), lambda i: (0, 0), pipeline_mode=pl.Buffered(1)),
            pl.BlockSpec((tm, d), lambda i: (i, 0)),
            pl.BlockSpec((None, N_MOD, d), lambda i: (i // n_lat_tiles, 0, 0)),
            pl.BlockSpec((1, d), lambda i: (0, 0)),
            pl.BlockSpec((1, d), lambda i: (0, 0)),
        ],
        out_specs=pl.BlockSpec((tm, d), lambda i: (i, 0)),
        out_shape=jax.ShapeDtypeStruct((rows, d), F32),
        compiler_params=_cparams(1),
        name="even_out",
    )(o_f, o_b, z, z, norm_w.reshape(1, GLA_DV), w, x_all, mods,
      ln_g.reshape(1, d), ln_b.reshape(1, d))


def _axial128(xh, cos, sin_a, sin_b):
    return xh * cos + pltpu.roll(xh, LANES - 16, 1) * sin_a + pltpu.roll(xh, 16, 1) * sin_b


def _axial_rows(xt, cos_t, sin_t):
    rot = jnp.concatenate([-xt[16:32], xt[0:16], -xt[48:64], xt[32:48]], axis=0)
    return xt * cos_t + rot * sin_t


def _mlaproj_kernel(h_ref, win_ref, qn_ref, kvn_ref, wuqt_ref, wuk_ref, wuvt_ref,
                    cos_ref, sina_ref, sinb_ref, cost_ref, sint_ref, qt_ref, k_ref, vt_ref):
    z = _dot(h_ref[...], win_ref[...])
    cq = z[:, :MLA_Q_RANK]
    ckv = z[:, MLA_Q_RANK:MLA_Q_RANK + MLA_KV_RANK]
    kr = z[:, MLA_Q_RANK + MLA_KV_RANK:]
    cos, sin_a, sin_b = cos_ref[...], sina_ref[...], sinb_ref[...]
    cqn = _mx(cq * lax.rsqrt(jnp.mean(cq * cq, axis=-1, keepdims=True) + RMS_EPS) * qn_ref[...])
    ckvn = _mx(ckv * lax.rsqrt(jnp.mean(ckv * ckv, axis=-1, keepdims=True) + RMS_EPS)
               * kvn_ref[...])
    qt = _dot_nt(wuqt_ref[...], cqn)
    kn = _dot(ckvn, wuk_ref[...])
    vt_ref[...] = _mx(_dot_nt(wuvt_ref[...], ckvn))
    kr_rot = _mx(_axial128(kr, cos, sin_a, sin_b))
    cos_t, sin_t = cost_ref[...], sint_ref[...]
    qs = MLA_SCALE * LOG2E
    r0, r1 = MLA_NOPE, MLA_NOPE + MLA_ROPE
    for h in range(MLA_HEADS):
        b0 = h * QK_PAD
        qt_ref[b0:b0 + r0, :] = _mx(qt[b0:b0 + r0] * qs)
        qt_ref[b0 + r0:b0 + r1, :] = _mx(_axial_rows(qt[b0 + r0:b0 + r1], cos_t, sin_t) * qs)
        qt_ref[b0 + r1:b0 + QK_PAD, :] = _mx(qt[b0 + r1:b0 + QK_PAD])
        k_ref[:, b0:b0 + LANES] = _mx(kn[:, h * MLA_NOPE:(h + 1) * MLA_NOPE])
        k_ref[:, b0 + LANES:b0 + QK_PAD] = kr_rot


def _mla_proj(h, w_in_p, q_norm, kv_norm, w_uqt, w_uk, w_uvt, tables):
    cos_t, sina_t, sinb_t, cos_rows, sin_rows = tables
    rows, d = h.shape
    tm = MLA_PROJ_ROWS
    hq = MLA_HEADS * QK_PAD
    hv = MLA_HEADS * MLA_DV

    def const2(a):
        return pl.BlockSpec(a.shape, lambda i: (0, 0))

    qn = q_norm.reshape(1, -1)
    kvn = kv_norm.reshape(1, -1)
    return pl.pallas_call(
        _mlaproj_kernel,
        grid=(rows // tm,),
        in_specs=[pl.BlockSpec((tm, d), lambda i: (i, 0)),
                  const2(w_in_p), const2(qn), const2(kvn), const2(w_uqt), const2(w_uk),
                  const2(w_uvt),
                  pl.BlockSpec((tm, LANES), lambda i: (i, 0)),
                  pl.BlockSpec((tm, LANES), lambda i: (i, 0)),
                  pl.BlockSpec((tm, LANES), lambda i: (i, 0)),
                  pl.BlockSpec((MLA_ROPE, tm), lambda i: (0, i)),
                  pl.BlockSpec((MLA_ROPE, tm), lambda i: (0, i))],
        out_specs=[pl.BlockSpec((hq, tm), lambda i: (0, i)),
                   pl.BlockSpec((tm, hq), lambda i: (i, 0)),
                   pl.BlockSpec((hv, tm), lambda i: (0, i))],
        out_shape=[jax.ShapeDtypeStruct((hq, rows), MXU_DTYPE),
                   jax.ShapeDtypeStruct((rows, hq), MXU_DTYPE),
                   jax.ShapeDtypeStruct((hv, rows), MXU_DTYPE)],
        compiler_params=_cparams(1),
        name="mla_proj",
    )(h, w_in_p, qn, kvn, w_uqt, w_uk, w_uvt, cos_t, sina_t, sinb_t, cos_rows, sin_rows)


def _flash_kernel(qt_ref, k_ref, vt_ref, o_ref, *, n_lat, n_ctx):
    tq = qt_ref.shape[1]
    qt = qt_ref[...]

    def scores(start, size):
        return _dot(k_ref[start:start + size, :], qt)

    def store(acc, l):
        o_ref[...] = (acc / l).T.astype(o_ref.dtype)

    def attend_online(kv_slices):
        m = jnp.full((1, tq), -jnp.inf, F32)
        l = jnp.zeros((1, tq), F32)
        acc = jnp.zeros((MLA_DV, tq), F32)
        for start, size in kv_slices:
            s = scores(start, size)
            m_new = jnp.maximum(m, jnp.max(s, axis=0, keepdims=True))
            alpha = jnp.exp2(m - m_new)
            p = jnp.exp2(s - m_new)
            l = alpha * l + jnp.sum(p, axis=0, keepdims=True)
            acc = alpha * acc + _dot(vt_ref[:, start:start + size], _mx(p))
            m = m_new
        store(acc, l)

    def attend_fixed_shift(kv_slices):
        m = jnp.max(scores(kv_slices[0][0], SHIFT_KEYS), axis=0, keepdims=True)
        l = jnp.zeros((1, tq), F32)
        acc = jnp.zeros((MLA_DV, tq), F32)
        for start, size in kv_slices:
            p = jnp.exp2(scores(start, size) - m)
            l = l + jnp.sum(p, axis=0, keepdims=True)
            acc = acc + _dot(vt_ref[:, start:start + size], _mx(p))
        store(acc, l)
        n_bad = (jnp.sum(jnp.where(l < FIXED_SHIFT_LIMIT, 0.0, 1.0))
                 + jnp.sum(jnp.where(jnp.abs(acc) < FIXED_SHIFT_LIMIT, 0.0, 1.0)))
        return n_bad == 0.0

    n_kv = n_lat + n_ctx
    i = pl.program_id(1)
    lat_slices = [(s0, KV_TILE) for s0 in range(0, n_kv, KV_TILE)]

    @pl.when(i > 0)
    def _():
        ok = attend_fixed_shift(lat_slices)

        @pl.when(jnp.logical_not(ok))
        def _():
            attend_online(lat_slices)

    @pl.when(i == 0)
    def _():
        attend_online([(n_lat, n_ctx)])


def _mla_attention(qt, k, vt, *, n_lat, n_ctx, n_lat_tiles):
    rows = k.shape[0]
    n_kv = n_lat + n_ctx
    tm = ROW_TILE
    kern = functools.partial(_flash_kernel, n_lat=n_lat, n_ctx=n_ctx)
    n_tiles = rows // tm
    assert n_tiles == n_lat_tiles + 1

    def tile(i):
        return (i + n_lat_tiles) % n_tiles

    return pl.pallas_call(
        kern,
        grid=(MLA_HEADS, n_tiles),
        in_specs=[pl.BlockSpec((QK_PAD, tm), lambda h, i: (h, tile(i))),
                  pl.BlockSpec((n_kv, QK_PAD), lambda h, i: (0, h)),
                  pl.BlockSpec((MLA_DV, n_kv), lambda h, i: (h, 0))],
        out_specs=pl.BlockSpec((tm, MLA_DV), lambda h, i: (tile(i), h)),
        out_shape=jax.ShapeDtypeStruct((rows, MLA_HEADS * MLA_DV), MXU_DTYPE),
        compiler_params=_cparams(2),
        name="mla_flash",
    )(qt, k, vt)


def _prep_even(w_in, wg_f, bg_f, wg_b, bg_b):
    gqk, gv, gr = w_in[:, 0:1024], w_in[:, 1024:2048], w_in[:, 2048:3072]
    gd = w_in[:, 3072:3104]
    rqk, rv, rg = w_in[:, 3104:4128], w_in[:, 4128:5152], w_in[:, 5152:6176]
    d = w_in.shape[0]
    w = jnp.concatenate([gqk, gv, gr, rqk, rv, rg, gd, jnp.zeros((d, 256 - 32), w_in.dtype)],
                        axis=1)
    r = GLA_GATE_RANK
    hq = GLA_HEADS * GLA_DK
    wg = jnp.zeros((2, LANES, hq), F32)
    wg = wg.at[0, 0:r].set(wg_f).at[1, r:2 * r].set(wg_b)
    bg = jnp.stack([bg_f, bg_b]).reshape(2, 1, hq)
    return _mx(w), _mx(wg), bg


def _prep_mla(w_in, w_uq, w_ukv):
    d = w_in.shape[0]
    w_in_p = _mx(jnp.pad(w_in, ((0, 0), (0, 1152 - w_in.shape[1]))))
    r = w_uq.shape[0]
    uq = w_uq.reshape(r, MLA_HEADS, MLA_NOPE + MLA_ROPE)
    uq = jnp.pad(uq, ((0, 0), (0, 0), (0, QK_PAD - MLA_NOPE - MLA_ROPE)))
    w_uqt = _mx(uq.reshape(r, MLA_HEADS * QK_PAD).T)
    ukv = w_ukv.reshape(w_ukv.shape[0], MLA_HEADS, MLA_NOPE + MLA_DV)
    w_uk = _mx(ukv[:, :, :MLA_NOPE].reshape(-1, MLA_HEADS * MLA_NOPE))
    w_uvt = _mx(ukv[:, :, MLA_NOPE:].reshape(-1, MLA_HEADS * MLA_DV).T)
    return w_in_p, w_uqt, w_uk, w_uvt


def _rope_angles(pos, dim):
    inv = ROPE_BASE ** (-jnp.arange(0, dim, 2, dtype=F32) / dim)
    ang = pos.astype(F32)[:, None] * inv[None, :]
    return jnp.concatenate([ang, ang], axis=-1)


def _ret_tables(n_lat, rows):
    ang = _rope_angles(jnp.arange(n_lat), RET_DK)
    cos = jnp.ones((rows, RET_DK), F32).at[:n_lat].set(jnp.cos(ang))
    sin = jnp.sin(ang)
    sign = jnp.where(jnp.arange(RET_DK) < RET_DK // 2, -1.0, 1.0)
    sin_s = jnp.zeros((rows, RET_DK), F32).at[:n_lat].set(sin * sign)
    return cos, sin_s


def _axial_tables(n_lat, rows):
    n_rows = n_lat // GRID_W
    row = jnp.repeat(jnp.arange(n_rows), GRID_W)
    col = jnp.tile(jnp.arange(GRID_W), n_rows)
    half = MLA_ROPE // 2
    ang = jnp.concatenate([_rope_angles(row, half), _rope_angles(col, half)], axis=-1)
    lane = jnp.arange(MLA_ROPE)
    lo = (lane % half) < (half // 2)
    cos = jnp.ones((rows, LANES), F32).at[:n_lat, :MLA_ROPE].set(jnp.cos(ang))
    sin = jnp.sin(ang)
    sin_a = jnp.zeros((rows, LANES), F32).at[:n_lat, :MLA_ROPE].set(jnp.where(lo, -sin, 0.0))
    sin_b = jnp.zeros((rows, LANES), F32).at[:n_lat, :MLA_ROPE].set(jnp.where(lo, 0.0, sin))
    cos_rows = jnp.ones((MLA_ROPE, rows), F32).at[:, :n_lat].set(jnp.cos(ang).T)
    sin_rows = jnp.zeros((MLA_ROPE, rows), F32).at[:, :n_lat].set(sin.T)
    return cos, sin_a, sin_b, cos_rows, sin_rows


def kernel(x, c, ctx, c_ctx, w_ada, b_ada, ln_g, ln_b, w_ffn_in, w_ffn_out, ev_w_in, ev_gla_wg_f, ev_gla_bg_f, ev_gla_wg_b, ev_gla_bg_b, ev_gla_norm, ev_w_out, od_w_in, od_q_norm, od_kv_norm, od_w_uq, od_w_ukv, od_w_out):
    n_lat, d = x.shape[1], x.shape[2]
    n_ctx = ctx.shape[1]
    depth = w_ada.shape[0]
    assert x.shape[0] == 1 and n_lat % ROW_TILE == 0 and n_ctx == SCAN_ROWS
    assert (n_lat + n_ctx) % KV_TILE == 0
    alpha = (2 * depth) ** 0.25
    n_lat_tiles = n_lat // ROW_TILE
    n_tiles = n_lat_tiles + 1
    rows = n_tiles * ROW_TILE
    n_seq_blocks = (n_lat + n_ctx) // SCAN_ROWS

    x_all = x[0]
    x_tail = jnp.concatenate([ctx[0], jnp.zeros((ROW_TILE - n_ctx, d), x.dtype)], axis=0)
    c_rows = jnp.zeros((16, d), F32).at[0].set(c[0]).at[1].set(c_ctx)
    mods = _ada(c_rows, w_ada, b_ada)[:, :2].reshape(depth, 2, N_MOD, d)

    ret_cos, ret_sin = _ret_tables(n_lat, rows)
    ax_tables = _axial_tables(n_lat, rows)

    wi, wo = _mx(w_ffn_in), _mx(w_ffn_out)
    for l in range(depth):
        last = l == depth - 1
        x_all, h_mix = _ffn(x_all, mods[l], wi, wo, ln_g[l, 0], ln_b[l, 0], layer=l, half=0,
                            mod_row=0, emit_row=3, alpha=alpha, n_lat_tiles=n_lat_tiles,
                            n_tiles=n_tiles, x_tail=x_tail if l == 0 else None)
        if l % 2 == 0:
            e = l // 2
            w_p, wg, bg = _prep_even(ev_w_in[e], ev_gla_wg_f[e], ev_gla_bg_f[e],
                                     ev_gla_wg_b[e], ev_gla_bg_b[e])
            z = _proj(h_mix, w_p, 1280)
            o_f, o_b = _even_scan(z, ret_cos, ret_sin, wg, bg, n_seq_blocks)
            x_all = _even_out(o_f, o_b, z, ev_gla_norm[e], _mx(ev_w_out[e]), x_all, mods[l],
                              ln_g[l, 1], ln_b[l, 1], gate_row=5, alpha=alpha,
                              n_lat_tiles=n_lat_tiles)
        else:
            o = l // 2
            w_in_p, w_uqt, w_uk, w_uvt = _prep_mla(od_w_in[o], od_w_uq[o], od_w_ukv[o])
            qt, k, vt = _mla_proj(h_mix, w_in_p, od_q_norm[o], od_kv_norm[o], w_uqt, w_uk,
                                  w_uvt, ax_tables)
            a = _mla_attention(qt, k, vt, n_lat=n_lat, n_ctx=n_ctx, n_lat_tiles=n_lat_tiles)
            x_all = _outproj_ln(a, _mx(od_w_out[o]), x_all, mods[l], ln_g[l, 1], ln_b[l, 1],
                                gate_row=5, alpha=alpha, n_lat_tiles=n_lat_tiles)
        x_all, _ = _ffn(x_all, mods[l], wi, wo, ln_g[l, 2], ln_b[l, 2], layer=l, half=1,
                        mod_row=6, emit_row=None, alpha=alpha, n_lat_tiles=n_lat_tiles,
                        n_tiles=n_lat_tiles if last else n_tiles)
    return x_all[None]
```

```python
import functools
import math

import numpy as np
import jax
import jax.numpy as jnp
from jax import lax
from jax.experimental import pallas as pl
from jax.experimental.pallas import tpu as pltpu

F32 = jnp.float32
MXU_DTYPE = jnp.bfloat16

N_MOD = 9
LN_EPS = 1e-5
RMS_EPS = 1e-6
CHUNK = 64
ROPE_BASE = 10000.0
GRID_W = 64
GLA_HEADS = 4
GLA_DK = 128
GLA_DV = 256
GLA_GATE_RANK = 16
GLA_GATE_NORM = 16.0
RET_HEADS = 4
RET_DK = 128
RET_DV = 256
MLA_HEADS = 16
MLA_Q_RANK = 512
MLA_KV_RANK = 512
MLA_NOPE = 128
MLA_ROPE = 64
MLA_DV = 128
MLA_SCALE = (MLA_NOPE + MLA_ROPE) ** -0.5
LOG2E = 1.4426950408889634
FIXED_SHIFT_LIMIT = 2.0 ** 100

LANES = 128
BF16_SUBLANES = 16
ROW_TILE = 512
FF_CHUNK = 1024
LN_SLICE = 128
SCAN_ROWS = 256
MLA_PROJ_ROWS = 256
KV_TILE = 1280
SHIFT_KEYS = 256
QK_PAD = 256
ADA_COLS = 2048
VMEM_LIMIT = 56 * 1024 * 1024

EV_BLOCK = 1024
EV_GLA_QK, EV_GLA_V, EV_GLA_GATE, EV_RET_QK, EV_RET_V, EV_RET_GATE = range(6)
EV_DECAY_LANE_BLOCK = 6 * EV_BLOCK // LANES
EV_COLS = 6 * EV_BLOCK + 2 * LANES
EV_PROJ_COLS = EV_COLS // 2
MLA_IN_PAD = MLA_Q_RANK + MLA_KV_RANK + LANES


def _cparams(n_axes, vmem=VMEM_LIMIT):
    return pltpu.CompilerParams(
        dimension_semantics=("arbitrary",) * n_axes, vmem_limit_bytes=vmem)


def _mx(a):
    return a.astype(MXU_DTYPE)


def _dot(a, b):
    return jnp.dot(a, b, preferred_element_type=F32)


def _dot_nt(a, b):
    return lax.dot_general(a, b, (((1,), (1,)), ((), ())), preferred_element_type=F32)


def _dot_tn(a, b):
    return lax.dot_general(a, b, (((0,), (0,)), ((), ())), preferred_element_type=F32)


def _silu(v):
    return v / (1.0 + jnp.exp(-v))


def _layer_norm(v, g, b):
    mu = jnp.mean(v, axis=-1, keepdims=True)
    d = v - mu
    var = jnp.mean(d * d, axis=-1, keepdims=True)
    return d * lax.rsqrt(var + LN_EPS) * g + b


def _ada_kernel(c_ref, w_ref, b_ref, o_ref):
    s = _silu(c_ref[...])
    o_ref[...] = _dot(_mx(s), _mx(w_ref[...])) + b_ref[...]


def _ada(c_rows, w_ada, b_ada):
    depth, d, n = w_ada.shape
    rows = c_rows.shape[0]
    tn = ADA_COLS
    return pl.pallas_call(
        _ada_kernel,
        grid=(depth, n // tn),
        in_specs=[
            pl.BlockSpec((rows, d), lambda l, j: (0, 0)),
            pl.BlockSpec((None, d, tn), lambda l, j: (l, 0, j)),
            pl.BlockSpec((None, 1, tn), lambda l, j: (l, 0, j)),
        ],
        out_specs=pl.BlockSpec((None, rows, tn), lambda l, j: (l, 0, j)),
        out_shape=jax.ShapeDtypeStruct((depth, rows, n), F32),
        compiler_params=_cparams(2),
        name="ada_mod",
    )(c_rows, w_ada, b_ada.reshape(depth, 1, n))


def _ffn_kernel(*refs, mod_row, emit_row, alpha, overlap, n_tiles, tail_tile):
    if tail_tile is None:
        x_ref, xt_ref = refs[0], None
        mod_ref, wg_ref, wu_ref, wout_ref, g_ref, b_ref = refs[1:7]
        rest = refs[7:]
    else:
        x_ref, xt_ref, mod_ref, wg_ref, wu_ref, wout_ref, g_ref, b_ref = refs[:8]
        rest = refs[8:]
    if emit_row is None:
        o_ref, xm_scr, acc_scr, v_scr, em_scr = rest
        h_ref = None
    else:
        o_ref, h_ref, xm_scr, acc_scr, v_scr, em_scr = rest
    i, k = pl.program_id(0), pl.program_id(1)
    last_k = pl.num_programs(1) - 1
    n_slices = v_scr.shape[0] // LN_SLICE

    def read_x():
        if xt_ref is None:
            return x_ref[...]
        return jnp.where(i == tail_tile, xt_ref[...], x_ref[...])

    def finish_rows():
        r0 = pl.multiple_of(jnp.minimum(k, n_slices - 1) * LN_SLICE, LN_SLICE)
        y = _layer_norm(v_scr[pl.ds(r0, LN_SLICE), :], g_ref[...], b_ref[...])
        o_ref[...] = y
        if h_ref is not None:
            h_ref[...] = _mx(y * (1.0 + em_scr[1:2, :]) + em_scr[0:1, :])

    def accumulate(skip, first=False):
        xm = xm_scr[...]
        act = _silu(_dot(xm, wg_ref[:, skip:])) * _dot(xm, wu_ref[:, skip:])
        y = _dot(_mx(act), wout_ref[skip:, :])
        if first:
            acc_scr[...] = y
        else:
            acc_scr[...] += y

    @pl.when((i == 0) & (k == 0))
    def _():
        v_scr[...] = jnp.zeros_like(v_scr)
        em_scr[...] = jnp.zeros_like(em_scr)

    @pl.when((i < n_tiles) & (k == 0))
    def _():
        shift = mod_ref[mod_row:mod_row + 1, :]
        scale = mod_ref[mod_row + 1:mod_row + 2, :]
        xm_scr[...] = _mx(read_x() * (1.0 + scale) + shift)
        accumulate(0, first=True)
        finish_rows()

    @pl.when((i < n_tiles) & (k > 0) & (k < last_k))
    def _():
        accumulate(0)
        finish_rows()

    @pl.when((i < n_tiles) & (k == last_k))
    def _():
        accumulate(overlap)
        finish_rows()
        gate = mod_ref[mod_row + 2:mod_row + 3, :]
        v_scr[...] = alpha * read_x() + 0.5 * gate * acc_scr[...]
        if h_ref is not None:
            em_scr[...] = mod_ref[emit_row:emit_row + 2, :]

    @pl.when(i == n_tiles)
    def _():
        finish_rows()


def _ffn(x_all, mods, w_in, w_out, ln_g, ln_b, *, layer, half, mod_row, emit_row, alpha,
         n_lat_tiles, n_tiles, x_tail=None):
    d = x_all.shape[1]
    f = w_out.shape[2]
    tm, tk = ROW_TILE, FF_CHUNK
    n_chunks = -(-f // tk)
    overlap = n_chunks * tk - f
    rows = n_tiles * tm

    assert f % LANES == 0 and tk % LANES == 0 and n_chunks >= tm // LN_SLICE

    def win(i, k, base=0):
        k = jnp.where(i < n_tiles, k, n_chunks - 1)
        return (jnp.minimum(k * (tk // LANES), (f - tk) // LANES) + base // LANES) * LANES

    def src(i):
        return jnp.minimum(i, n_tiles - 1)

    n_slices = tm // LN_SLICE

    def dst(i, k):
        return jnp.where(i == 0, 0, (i - 1) * n_slices + jnp.minimum(k, n_slices - 1))

    out_shape = [jax.ShapeDtypeStruct((rows, d), F32)]
    out_specs = [pl.BlockSpec((LN_SLICE, d), lambda i, k: (dst(i, k), 0))]
    if emit_row is not None:
        out_shape.append(jax.ShapeDtypeStruct((rows, d), MXU_DTYPE))
        out_specs.append(pl.BlockSpec((LN_SLICE, d), lambda i, k: (dst(i, k), 0)))
    tail_tile = None if x_tail is None else n_tiles - 1
    kern = functools.partial(_ffn_kernel, mod_row=mod_row, emit_row=emit_row, alpha=alpha,
                             overlap=overlap, n_tiles=n_tiles, tail_tile=tail_tile)
    if x_tail is None:
        x_args = [x_all]
        x_specs = [pl.BlockSpec((tm, d), lambda i, k: (src(i), 0))]
    else:
        x_args = [x_all, x_tail]
        x_specs = [pl.BlockSpec((tm, d), lambda i, k: (jnp.minimum(i, tail_tile - 1), 0)),
                   pl.BlockSpec((tm, d), lambda i, k: (0, 0))]
    res = pl.pallas_call(
        kern,
        grid=(n_tiles + 1, n_chunks),
        in_specs=x_specs + [
            pl.BlockSpec((None, N_MOD, d), lambda i, k: (src(i) // n_lat_tiles, 0, 0)),
            pl.BlockSpec((None, None, pl.Element(d), pl.Element(tk)),
                         lambda i, k: (layer, half, 0, win(i, k))),
            pl.BlockSpec((None, None, pl.Element(d), pl.Element(tk)),
                         lambda i, k: (layer, half, 0, win(i, k, f))),
            pl.BlockSpec((None, None, pl.Element(tk), pl.Element(d)),
                         lambda i, k: (layer, half, win(i, k), 0)),
            pl.BlockSpec((1, d), lambda i, k: (0, 0)),
            pl.BlockSpec((1, d), lambda i, k: (0, 0)),
        ],
        out_specs=out_specs,
        out_shape=out_shape,
        scratch_shapes=[pltpu.VMEM((tm, d), MXU_DTYPE), pltpu.VMEM((tm, d), F32),
                        pltpu.VMEM((tm, d), F32), pltpu.VMEM((2, d), F32)],
        compiler_params=_cparams(2),
        name="ffn",
    )(*x_args, mods, w_in, w_in, w_out, ln_g.reshape(1, d), ln_b.reshape(1, d))
    return res if emit_row is not None else (res[0], None)


def _proj_kernel(h_ref, w_ref, o_ref):
    o_ref[...] = _dot(h_ref[...], w_ref[...])


def _proj(h, w, tn):
    rows, kdim = h.shape
    n = w.shape[1]
    tm = ROW_TILE
    return pl.pallas_call(
        _proj_kernel,
        grid=(n // tn, rows // tm),
        in_specs=[
            pl.BlockSpec((tm, kdim), lambda j, i: (i, 0)),
            pl.BlockSpec((kdim, tn), lambda j, i: (0, j)),
        ],
        out_specs=pl.BlockSpec((tm, tn), lambda j, i: (i, j)),
        out_shape=jax.ShapeDtypeStruct((rows, n), F32),
        compiler_params=_cparams(2),
        name="even_proj",
    )(h, w)


def _outproj_kernel(a_ref, w_ref, x_ref, mod_ref, g_ref, b_ref, o_ref, *, gate_row, alpha):
    y = _dot(a_ref[...], w_ref[...])
    gate = mod_ref[gate_row:gate_row + 1, :]
    v = alpha * x_ref[...] + gate * y
    o_ref[...] = _layer_norm(v, g_ref[...], b_ref[...])


def _outproj_ln(a, w, x_all, mods, ln_g, ln_b, *, gate_row, alpha, n_lat_tiles):
    rows, d = x_all.shape
    kdim = a.shape[1]
    tm = ROW_TILE
    kern = functools.partial(_outproj_kernel, gate_row=gate_row, alpha=alpha)
    return pl.pallas_call(
        kern,
        grid=(rows // tm,),
        in_specs=[
            pl.BlockSpec((tm, kdim), lambda i: (i, 0)),
            pl.BlockSpec((kdim, d), lambda i: (0, 0)),
            pl.BlockSpec((tm, d), lambda i: (i, 0)),
            pl.BlockSpec((None, N_MOD, d), lambda i: (i // n_lat_tiles, 0, 0)),
            pl.BlockSpec((1, d), lambda i: (0, 0)),
            pl.BlockSpec((1, d), lambda i: (0, 0)),
        ],
        out_specs=pl.BlockSpec((tm, d), lambda i: (i, 0)),
        out_shape=jax.ShapeDtypeStruct((rows, d), F32),
        compiler_params=_cparams(1),
        name="mix_out",
    )(a, w, x_all, mods, ln_g.reshape(1, d), ln_b.reshape(1, d))


_GLA_LEVELS = (64, 32, 16, 8, 4, 2)


def _scan_positions(reverse):
    idx = np.arange(CHUNK)
    return idx[::-1].copy() if reverse else idx


def _gla_exponent_matrix(reverse):
    pos = _scan_positions(reverse)
    pi, pt = pos[:, None], pos[None, :]
    mats = [pt <= pi,
            pt > pi]
    for n in _GLA_LEVELS:
        half = n // 2
        ref = (pi // n) * n + half - 1
        q_side = (pi % n) >= half
        mats.append(np.where(q_side, (pt > ref) & (pt <= pi), (pt > pi) & (pt <= ref)))
    return np.concatenate(mats, axis=0).astype(np.float32)


def _ret_constants(reverse):
    scales = np.arange(RET_HEADS, dtype=np.float64)
    off = 5.5 if reverse else 5.0
    log_gamma = np.log1p(-np.exp2(-off - scales))
    idx = np.arange(CHUNK, dtype=np.float64)
    diff = idx[:, None] - idx[None, :]
    dmat = np.where(diff >= 0, np.exp(log_gamma[:, None, None] * np.maximum(diff, 0.0)), 0.0)
    q_dec = np.exp(log_gamma[:, None] * (idx + 1.0))
    k_dec = np.exp(log_gamma[:, None] * (CHUNK - 1.0 - idx))
    s_dec = np.exp(log_gamma * CHUNK)
    if reverse:
        dmat = dmat[:, ::-1, ::-1]
        q_dec = q_dec[:, ::-1]
        k_dec = k_dec[:, ::-1]
    expand = lambda a: np.repeat(a.T, RET_DK, axis=1)
    return (dmat.astype(np.float32), expand(q_dec).astype(np.float32),
            expand(k_dec).astype(np.float32),
            np.repeat(s_dec, RET_DK)[None, :].astype(np.float32))


def _log_sigmoid(v):
    return jnp.minimum(v, 0.0) - jnp.log1p(jnp.exp(-jnp.abs(v)))


def _gla_masks(reverse):
    c = CHUNK
    row = lax.broadcasted_iota(jnp.int32, (c, c), 0)
    col = lax.broadcasted_iota(jnp.int32, (c, c), 1)
    pr, pc = ((c - 1) - row, (c - 1) - col) if reverse else (row, col)
    levels = []
    for n in _GLA_LEVELS:
        sh, half = int(math.log2(n)), n // 2
        levels.append((lax.shift_right_logical(pr, sh) == lax.shift_right_logical(pc, sh))
                      & ((pr & (n - 1)) >= half) & ((pc & (n - 1)) < half))
    return levels, row == col


def _rope128(xh, cos, sin_signed):
    return xh * cos + pltpu.roll(xh, 64, 1) * sin_signed


def _scan_chunk_step(streams, consts, out_refs, sg_scr, sr_scr, lgs, masks, cc):
    wg_ref, bg_ref, mexp_ref, dmat_ref, qdec_ref, kdec_ref, sdec_ref = consts
    c = CHUNK
    n_chunks = SCAN_ROWS // CHUNK
    hq = GLA_HEADS * GLA_DK
    hr = RET_HEADS * RET_DK
    o_off = GLA_HEADS * GLA_DV
    tasks = []
    for d in (0, 1):
        zqk, zv, zrqk, zrv, _, cos_ref, sin_ref = streams[d]
        reverse = d == 1
        r0 = ((n_chunks - 1 - cc) if reverse else cc) * c
        rows = slice(r0, r0 + c)
        g = lgs[d][rows]
        g_hi = _mx(g)
        g_lo = _mx(g - g_hi.astype(F32))
        mexp = mexp_ref[d]
        w_all = jnp.exp(_dot(mexp, g_hi) + _dot(mexp, g_lo))
        cs, sn = cos_ref[rows, :], sin_ref[rows, :]
        for h in range(GLA_HEADS):
            lanes = slice(h * GLA_DK, (h + 1) * GLA_DK)
            tasks.append(dict(
                kind="gla", d=d, rows=rows, w=w_all[:, lanes],
                q=zqk[rows, lanes] * (GLA_DK ** -0.5),
                k=zqk[rows, hq + h * GLA_DK:hq + (h + 1) * GLA_DK],
                v=zv[rows, h * GLA_DV:(h + 1) * GLA_DV],
                st=sg_scr.at[d, h], cols=slice(h * GLA_DV, (h + 1) * GLA_DV)))
        for h in range(RET_HEADS):
            lanes = slice(h * RET_DK, (h + 1) * RET_DK)
            tasks.append(dict(
                kind="ret", d=d, rows=rows,
                q=_rope128(zrqk[rows, lanes], cs, sn),
                k=_rope128(zrqk[rows, hr + h * RET_DK:hr + (h + 1) * RET_DK]
                           * (RET_DK ** -0.5), cs, sn),
                v=zrv[rows, h * RET_DV:(h + 1) * RET_DV],
                dmat=dmat_ref[d, h], q_dec=qdec_ref[d][:, lanes], k_dec=kdec_ref[d][:, lanes],
                s_dec=sdec_ref[d][:, lanes], st=sr_scr.at[d, h],
                cols=slice(o_off + h * RET_DV, o_off + (h + 1) * RET_DV)))
    for t in tasks:
        t["s_t"] = t["st"][...]
        if t["kind"] == "gla":
            w, q, k = t["w"], t["q"], t["k"]
            t["inter"] = _dot_nt(_mx(q * w[0:c]), _mx(t["s_t"]))
            t["lvl"] = [_dot_nt(_mx(q * w[(2 + li) * c:(3 + li) * c]),
                                _mx(k * w[(2 + li) * c:(3 + li) * c]))
                        for li in range(len(_GLA_LEVELS))]
        else:
            t["inter"] = _dot_nt(_mx(t["q"] * t["q_dec"]), _mx(t["s_t"]))
            t["qk"] = _dot_nt(_mx(t["q"]), _mx(t["k"]))
    for t in tasks:
        if t["kind"] == "gla":
            level_masks, diag_mask = masks[t["d"]]
            a = jnp.where(diag_mask, jnp.sum(t["q"] * t["k"], axis=-1, keepdims=True), 0.0)
            for li in range(len(_GLA_LEVELS)):
                a = a + jnp.where(level_masks[li], t["lvl"][li], 0.0)
        else:
            a = t["qk"] * t["dmat"]
        t["a"] = _mx(a)
    for t in tasks:
        v = _mx(t["v"])
        out_refs[t["d"]][t["rows"], t["cols"]] = t["inter"] + _dot(t["a"], v)
        if t["kind"] == "gla":
            w = t["w"]
            decay = w[0:1, :] if t["d"] == 1 else w[c - 1:c, :]
            k_dec = t["k"] * w[c:2 * c]
        else:
            decay = t["s_dec"]
            k_dec = t["k"] * t["k_dec"]
        t["st"][...] = decay * t["s_t"] + _dot_tn(v, _mx(k_dec))


def _scan_kernel(*refs):
    streams, consts = (refs[0:7], refs[7:14]), refs[14:21]
    of_ref, ob_ref, sg_scr, sr_scr = refs[21:]
    wg_ref, bg_ref = consts[0], consts[1]

    @pl.when(pl.program_id(0) == 0)
    def _():
        sg_scr[...] = jnp.zeros_like(sg_scr)
        sr_scr[...] = jnp.zeros_like(sr_scr)

    lgs = [_log_sigmoid(_dot(_mx(streams[d][4][...]), wg_ref[d]) + bg_ref[d])
           * (1.0 / GLA_GATE_NORM) for d in (0, 1)]
    masks = [_gla_masks(False), _gla_masks(True)]
    for cc in range(SCAN_ROWS // CHUNK):
        _scan_chunk_step(streams, consts, (of_ref, ob_ref), sg_scr, sr_scr, lgs, masks, cc)


def _even_scan(z, cos_t, sin_t, wg, bg, n_lat_blocks, n_seq_blocks):
    rows = z.shape[0]
    n_blocks = rows // SCAN_ROWS
    r = SCAN_ROWS

    def rf(i):
        return jnp.where(i < n_seq_blocks, (i + n_lat_blocks) % n_seq_blocks, i)

    def rb(i):
        return jnp.where(i < n_seq_blocks, n_seq_blocks - 1 - i, i)

    def stream_specs(rmap):
        return [
            pl.BlockSpec((r, EV_BLOCK), lambda i: (rmap(i), EV_GLA_QK)),
            pl.BlockSpec((r, EV_BLOCK), lambda i: (rmap(i), EV_GLA_V)),
            pl.BlockSpec((r, EV_BLOCK), lambda i: (rmap(i), EV_RET_QK)),
            pl.BlockSpec((r, EV_BLOCK), lambda i: (rmap(i), EV_RET_V)),
            pl.BlockSpec((r, LANES), lambda i: (rmap(i), EV_DECAY_LANE_BLOCK)),
            pl.BlockSpec((r, LANES), lambda i: (rmap(i), 0)),
            pl.BlockSpec((r, LANES), lambda i: (rmap(i), 0)),
        ]

    mexp = jnp.asarray(np.stack([_gla_exponent_matrix(False), _gla_exponent_matrix(True)]),
                       MXU_DTYPE)
    rc = [_ret_constants(False), _ret_constants(True)]
    dmat = jnp.asarray(np.stack([rc[0][0], rc[1][0]]))
    qdec = jnp.asarray(np.stack([rc[0][1], rc[1][1]]))
    kdec = jnp.asarray(np.stack([rc[0][2], rc[1][2]]))
    sdec = jnp.asarray(np.stack([rc[0][3], rc[1][3]]))
    consts = [wg, bg, mexp, dmat, qdec, kdec, sdec]

    def full_spec(a):
        nd = a.ndim
        return pl.BlockSpec(a.shape, lambda i: (0,) * nd)

    d_out = GLA_HEADS * GLA_DV + RET_HEADS * RET_DV
    return pl.pallas_call(
        _scan_kernel,
        grid=(n_blocks,),
        in_specs=stream_specs(rf) + stream_specs(rb) + [full_spec(a) for a in consts],
        out_specs=[pl.BlockSpec((r, d_out), lambda i: (rf(i), 0)),
                   pl.BlockSpec((r, d_out), lambda i: (rb(i), 0))],
        out_shape=[jax.ShapeDtypeStruct((rows, d_out), F32)] * 2,
        scratch_shapes=[pltpu.VMEM((2, GLA_HEADS, GLA_DV, GLA_DK), F32),
                        pltpu.VMEM((2, RET_HEADS, RET_DV, RET_DK), F32)],
        compiler_params=_cparams(1),
        name="even_scan",
    )(z, z, z, z, z, cos_t, sin_t, z, z, z, z, z, cos_t, sin_t, *consts)


def _evout_kernel(of_ref, ob_ref, gr_ref, rg_ref, nw_ref, w_ref, x_ref, mod_ref, g_ref, b_ref,
                  o_ref, *, gate_row, alpha):
    parts = []
    for h in range(GLA_HEADS + RET_HEADS):
        lanes = slice(h * GLA_DV, (h + 1) * GLA_DV)
        oh = of_ref[:, lanes] + ob_ref[:, lanes]
        y = oh * lax.rsqrt(jnp.mean(oh * oh, axis=-1, keepdims=True) + RMS_EPS)
        if h < GLA_HEADS:
            y = y * nw_ref[...]
            gate = gr_ref[:, h * GLA_DV:(h + 1) * GLA_DV]
        else:
            hh = h - GLA_HEADS
            gate = rg_ref[:, hh * RET_DV:(hh + 1) * RET_DV]
        parts.append(_mx(y * _silu(gate)))
    y = _dot(jnp.concatenate(parts, axis=1), w_ref[...])
    v = alpha * x_ref[...] + mod_ref[gate_row:gate_row + 1, :] * y
    o_ref[...] = _layer_norm(v, g_ref[...], b_ref[...])


def _even_out(o_f, o_b, z, norm_w, w, x_all, mods, ln_g, ln_b, *, gate_row, alpha, n_lat_tiles):
    rows, d = x_all.shape
    d_out = o_f.shape[1]
    tm = ROW_TILE
    kern = functools.partial(_evout_kernel, gate_row=gate_row, alpha=alpha)
    return pl.pallas_call(
        kern,
        grid=(rows // tm,),
        in_specs=[
            pl.BlockSpec((tm, d_out), lambda i: (i, 0)),
            pl.BlockSpec((tm, d_out), lambda i: (i, 0)),
            pl.BlockSpec((tm, EV_BLOCK), lambda i: (i, EV_GLA_GATE)),
            pl.BlockSpec((tm, EV_BLOCK), lambda i: (i, EV_RET_GATE)),
            pl.BlockSpec((1, GLA_DV), lambda i: (0, 0)),
            pl.BlockSpec((d_out, d), lambda i: (0, 0), pipeline_mode=pl.Buffered(1)),
            pl.BlockSpec((tm, d), lambda i: (i, 0)),
            pl.BlockSpec((None, N_MOD, d), lambda i: (i // n_lat_tiles, 0, 0)),
            pl.BlockSpec((1, d), lambda i: (0, 0)),
            pl.BlockSpec((1, d), lambda i: (0, 0)),
        ],
        out_specs=pl.BlockSpec((tm, d), lambda i: (i, 0)),
        out_shape=jax.ShapeDtypeStruct((rows, d), F32),
        compiler_params=_cparams(1),
        name="even_out",
    )(o_f, o_b, z, z, norm_w.reshape(1, GLA_DV), w, x_all, mods,
      ln_g.reshape(1, d), ln_b.reshape(1, d))


def _axial128(xh, cos, sin_a, sin_b):
    return xh * cos + pltpu.roll(xh, LANES - 16, 1) * sin_a + pltpu.roll(xh, 16, 1) * sin_b


def _axial_rows(xt, cos_t, sin_t):
    rot = jnp.concatenate([-xt[16:32], xt[0:16], -xt[48:64], xt[32:48]], axis=0)
    return xt * cos_t + rot * sin_t


def _mlaproj_kernel(h_ref, win_ref, qn_ref, kvn_ref, wuqt_ref, wuk_ref, wuvt_ref,
                    cos_ref, sina_ref, sinb_ref, cost_ref, sint_ref, qt_ref, k_ref, vt_ref):
    z = _dot(h_ref[...], win_ref[...])
    cq = z[:, :MLA_Q_RANK]
    ckv = z[:, MLA_Q_RANK:MLA_Q_RANK + MLA_KV_RANK]
    kr = z[:, MLA_Q_RANK + MLA_KV_RANK:]
    cos, sin_a, sin_b = cos_ref[...], sina_ref[...], sinb_ref[...]
    cqn = _mx(cq * lax.rsqrt(jnp.mean(cq * cq, axis=-1, keepdims=True) + RMS_EPS) * qn_ref[...])
    ckvn = _mx(ckv * lax.rsqrt(jnp.mean(ckv * ckv, axis=-1, keepdims=True) + RMS_EPS)
               * kvn_ref[...])
    qt = _dot_nt(wuqt_ref[...], cqn)
    kn = _dot(ckvn, wuk_ref[...])
    vt_ref[...] = _mx(_dot_nt(wuvt_ref[...], ckvn))
    kr_rot = _mx(_axial128(kr, cos, sin_a, sin_b))
    cos_t, sin_t = cost_ref[...], sint_ref[...]
    qs = MLA_SCALE * LOG2E
    r0, r1 = MLA_NOPE, MLA_NOPE + MLA_ROPE
    for h in range(MLA_HEADS):
        b0 = h * QK_PAD
        qt_ref[b0:b0 + r0, :] = _mx(qt[b0:b0 + r0] * qs)
        qt_ref[b0 + r0:b0 + r1, :] = _mx(_axial_rows(qt[b0 + r0:b0 + r1], cos_t, sin_t) * qs)
        qt_ref[b0 + r1:b0 + QK_PAD, :] = _mx(qt[b0 + r1:b0 + QK_PAD])
        k_ref[:, b0:b0 + LANES] = _mx(kn[:, h * MLA_NOPE:(h + 1) * MLA_NOPE])
        k_ref[:, b0 + LANES:b0 + QK_PAD] = kr_rot


def _mla_proj(h, w_in_p, q_norm, kv_norm, w_uqt, w_uk, w_uvt, tables):
    cos_t, sina_t, sinb_t, cos_rows, sin_rows = tables
    rows, d = h.shape
    tm = MLA_PROJ_ROWS
    hq = MLA_HEADS * QK_PAD
    hv = MLA_HEADS * MLA_DV

    def const2(a):
        return pl.BlockSpec(a.shape, lambda i: (0, 0))

    qn = q_norm.reshape(1, -1)
    kvn = kv_norm.reshape(1, -1)
    return pl.pallas_call(
        _mlaproj_kernel,
        grid=(rows // tm,),
        in_specs=[pl.BlockSpec((tm, d), lambda i: (i, 0)),
                  const2(w_in_p), const2(qn), const2(kvn), const2(w_uqt), const2(w_uk),
                  const2(w_uvt),
                  pl.BlockSpec((tm, LANES), lambda i: (i, 0)),
                  pl.BlockSpec((tm, LANES), lambda i: (i, 0)),
                  pl.BlockSpec((tm, LANES), lambda i: (i, 0)),
                  pl.BlockSpec((MLA_ROPE, tm), lambda i: (0, i)),
                  pl.BlockSpec((MLA_ROPE, tm), lambda i: (0, i))],
        out_specs=[pl.BlockSpec((hq, tm), lambda i: (0, i)),
                   pl.BlockSpec((tm, hq), lambda i: (i, 0)),
                   pl.BlockSpec((hv, tm), lambda i: (0, i))],
        out_shape=[jax.ShapeDtypeStruct((hq, rows), MXU_DTYPE),
                   jax.ShapeDtypeStruct((rows, hq), MXU_DTYPE),
                   jax.ShapeDtypeStruct((hv, rows), MXU_DTYPE)],
        compiler_params=_cparams(1),
        name="mla_proj",
    )(h, w_in_p, qn, kvn, w_uqt, w_uk, w_uvt, cos_t, sina_t, sinb_t, cos_rows, sin_rows)


def _flash_kernel(qt_ref, k_ref, vt_ref, o_ref, *, n_lat, n_ctx):
    tq = qt_ref.shape[1]
    qt = qt_ref[...]

    def scores(start, size):
        return _dot(k_ref[start:start + size, :], qt)

    def store(acc, l):
        o_ref[...] = (acc / l).T.astype(o_ref.dtype)

    def attend_online(kv_slices):
        m = jnp.full((1, tq), -jnp.inf, F32)
        l = jnp.zeros((1, tq), F32)
        acc = jnp.zeros((MLA_DV, tq), F32)
        for start, size in kv_slices:
            s = scores(start, size)
            m_new = jnp.maximum(m, jnp.max(s, axis=0, keepdims=True))
            alpha = jnp.exp2(m - m_new)
            p = jnp.exp2(s - m_new)
            l = alpha * l + jnp.sum(p, axis=0, keepdims=True)
            acc = alpha * acc + _dot(vt_ref[:, start:start + size], _mx(p))
            m = m_new
        store(acc, l)

    def attend_fixed_shift(kv_slices):
        m = jnp.max(scores(kv_slices[0][0], SHIFT_KEYS), axis=0, keepdims=True)
        l = jnp.zeros((1, tq), F32)
        acc = jnp.zeros((MLA_DV, tq), F32)
        for start, size in kv_slices:
            p = jnp.exp2(scores(start, size) - m)
            l = l + jnp.sum(p, axis=0, keepdims=True)
            acc = acc + _dot(vt_ref[:, start:start + size], _mx(p))
        store(acc, l)
        n_bad = (jnp.sum(jnp.where(l < FIXED_SHIFT_LIMIT, 0.0, 1.0))
                 + jnp.sum(jnp.where(jnp.abs(acc) < FIXED_SHIFT_LIMIT, 0.0, 1.0)))
        return n_bad == 0.0

    n_kv = n_lat + n_ctx
    i = pl.program_id(1)
    lat_slices = [(s0, KV_TILE) for s0 in range(0, n_kv, KV_TILE)]

    @pl.when(i > 0)
    def _():
        ok = attend_fixed_shift(lat_slices)

        @pl.when(jnp.logical_not(ok))
        def _():
            attend_online(lat_slices)

    @pl.when(i == 0)
    def _():
        attend_online([(n_lat, n_ctx)])


def _mla_attention(qt, k, vt, *, n_lat, n_ctx, n_lat_tiles):
    rows = k.shape[0]
    n_kv = n_lat + n_ctx
    tm = ROW_TILE
    kern = functools.partial(_flash_kernel, n_lat=n_lat, n_ctx=n_ctx)
    n_tiles = rows // tm
    assert n_tiles == n_lat_tiles + 1

    def tile(i):
        return (i + n_lat_tiles) % n_tiles

    return pl.pallas_call(
        kern,
        grid=(MLA_HEADS, n_tiles),
        in_specs=[pl.BlockSpec((QK_PAD, tm), lambda h, i: (h, tile(i))),
                  pl.BlockSpec((n_kv, QK_PAD), lambda h, i: (0, h)),
                  pl.BlockSpec((MLA_DV, n_kv), lambda h, i: (h, 0))],
        out_specs=pl.BlockSpec((tm, MLA_DV), lambda h, i: (tile(i), h)),
        out_shape=jax.ShapeDtypeStruct((rows, MLA_HEADS * MLA_DV), MXU_DTYPE),
        compiler_params=_cparams(2),
        name="mla_flash",
    )(qt, k, vt)


def _prep_even(w_in, wg_f, bg_f, wg_b, bg_b):
    r = GLA_GATE_RANK
    gla_qk, gla_v = 2 * GLA_HEADS * GLA_DK, GLA_HEADS * GLA_DV
    ret_qk, ret_v = 2 * RET_HEADS * RET_DK, RET_HEADS * RET_DV
    widths = [gla_qk, gla_v, gla_v, 2 * r, ret_qk, ret_v, ret_v]
    offs = np.cumsum([0] + widths)
    gqk, gv, gr, gd, rqk, rv, rg = [w_in[:, a:b] for a, b in zip(offs[:-1], offs[1:])]
    d = w_in.shape[0]
    pad = jnp.zeros((d, 2 * LANES - 2 * r), w_in.dtype)
    w = jnp.concatenate([gqk, gv, gr, rqk, rv, rg, gd, pad], axis=1)
    hq = GLA_HEADS * GLA_DK
    wg = jnp.zeros((2, LANES, hq), F32)
    wg = wg.at[0, 0:r].set(wg_f).at[1, r:2 * r].set(wg_b)
    bg = jnp.stack([bg_f, bg_b]).reshape(2, 1, hq)
    return _mx(w), _mx(wg), bg


def _prep_mla(w_in, w_uq, w_ukv):
    d = w_in.shape[0]
    w_in_p = _mx(jnp.pad(w_in, ((0, 0), (0, MLA_IN_PAD - w_in.shape[1]))))
    r = w_uq.shape[0]
    uq = w_uq.reshape(r, MLA_HEADS, MLA_NOPE + MLA_ROPE)
    uq = jnp.pad(uq, ((0, 0), (0, 0), (0, QK_PAD - MLA_NOPE - MLA_ROPE)))
    w_uqt = _mx(uq.reshape(r, MLA_HEADS * QK_PAD).T)
    ukv = w_ukv.reshape(w_ukv.shape[0], MLA_HEADS, MLA_NOPE + MLA_DV)
    w_uk = _mx(ukv[:, :, :MLA_NOPE].reshape(-1, MLA_HEADS * MLA_NOPE))
    w_uvt = _mx(ukv[:, :, MLA_NOPE:].reshape(-1, MLA_HEADS * MLA_DV).T)
    return w_in_p, w_uqt, w_uk, w_uvt


def _rope_angles(pos, dim):
    inv = ROPE_BASE ** (-jnp.arange(0, dim, 2, dtype=F32) / dim)
    ang = pos.astype(F32)[:, None] * inv[None, :]
    return jnp.concatenate([ang, ang], axis=-1)


def _ret_tables(n_lat, rows):
    ang = _rope_angles(jnp.arange(n_lat), RET_DK)
    cos = jnp.ones((rows, RET_DK), F32).at[:n_lat].set(jnp.cos(ang))
    sin = jnp.sin(ang)
    sign = jnp.where(jnp.arange(RET_DK) < RET_DK // 2, -1.0, 1.0)
    sin_s = jnp.zeros((rows, RET_DK), F32).at[:n_lat].set(sin * sign)
    return cos, sin_s


def _axial_tables(n_lat, rows):
    n_rows = n_lat // GRID_W
    row = jnp.repeat(jnp.arange(n_rows), GRID_W)
    col = jnp.tile(jnp.arange(GRID_W), n_rows)
    half = MLA_ROPE // 2
    ang = jnp.concatenate([_rope_angles(row, half), _rope_angles(col, half)], axis=-1)
    lane = jnp.arange(MLA_ROPE)
    lo = (lane % half) < (half // 2)
    cos = jnp.ones((rows, LANES), F32).at[:n_lat, :MLA_ROPE].set(jnp.cos(ang))
    sin = jnp.sin(ang)
    sin_a = jnp.zeros((rows, LANES), F32).at[:n_lat, :MLA_ROPE].set(jnp.where(lo, -sin, 0.0))
    sin_b = jnp.zeros((rows, LANES), F32).at[:n_lat, :MLA_ROPE].set(jnp.where(lo, 0.0, sin))
    cos_rows = jnp.ones((MLA_ROPE, rows), F32).at[:, :n_lat].set(jnp.cos(ang).T)
    sin_rows = jnp.zeros((MLA_ROPE, rows), F32).at[:, :n_lat].set(sin.T)
    return cos, sin_a, sin_b, cos_rows, sin_rows


def kernel(x, c, ctx, c_ctx, w_ada, b_ada, ln_g, ln_b, w_ffn_in, w_ffn_out, ev_w_in, ev_gla_wg_f, ev_gla_bg_f, ev_gla_wg_b, ev_gla_bg_b, ev_gla_norm, ev_w_out, od_w_in, od_q_norm, od_kv_norm, od_w_uq, od_w_ukv, od_w_out):
    n_lat, d = x.shape[1], x.shape[2]
    n_ctx = ctx.shape[1]
    depth = w_ada.shape[0]
    assert x.shape[0] == 1 and n_lat % ROW_TILE == 0 and n_ctx == SCAN_ROWS
    assert (n_lat + n_ctx) % KV_TILE == 0
    alpha = (2 * depth) ** 0.25
    n_lat_tiles = n_lat // ROW_TILE
    n_tiles = n_lat_tiles + 1
    rows = n_tiles * ROW_TILE
    n_seq_blocks = (n_lat + n_ctx) // SCAN_ROWS

    x_all = x[0]
    x_tail = jnp.concatenate([ctx[0], jnp.zeros((ROW_TILE - n_ctx, d), x.dtype)], axis=0)
    c_rows = jnp.zeros((BF16_SUBLANES, d), F32).at[0].set(c[0]).at[1].set(c_ctx)
    mods = _ada(c_rows, w_ada, b_ada)[:, :2].reshape(depth, 2, N_MOD, d)

    ret_cos, ret_sin = _ret_tables(n_lat, rows)
    ax_tables = _axial_tables(n_lat, rows)

    wi, wo = _mx(w_ffn_in), _mx(w_ffn_out)
    for l in range(depth):
        last = l == depth - 1
        x_all, h_mix = _ffn(x_all, mods[l], wi, wo, ln_g[l, 0], ln_b[l, 0], layer=l, half=0,
                            mod_row=0, emit_row=3, alpha=alpha, n_lat_tiles=n_lat_tiles,
                            n_tiles=n_tiles, x_tail=x_tail if l == 0 else None)
        if l % 2 == 0:
            e = l // 2
            w_p, wg, bg = _prep_even(ev_w_in[e], ev_gla_wg_f[e], ev_gla_bg_f[e],
                                     ev_gla_wg_b[e], ev_gla_bg_b[e])
            z = _proj(h_mix, w_p, EV_PROJ_COLS)
            o_f, o_b = _even_scan(z, ret_cos, ret_sin, wg, bg, n_lat // SCAN_ROWS, n_seq_blocks)
            x_all = _even_out(o_f, o_b, z, ev_gla_norm[e], _mx(ev_w_out[e]), x_all, mods[l],
                              ln_g[l, 1], ln_b[l, 1], gate_row=5, alpha=alpha,
                              n_lat_tiles=n_lat_tiles)
        else:
            o = l // 2
            w_in_p, w_uqt, w_uk, w_uvt = _prep_mla(od_w_in[o], od_w_uq[o], od_w_ukv[o])
            qt, k, vt = _mla_proj(h_mix, w_in_p, od_q_norm[o], od_kv_norm[o], w_uqt, w_uk,
                                  w_uvt, ax_tables)
            a = _mla_attention(qt, k, vt, n_lat=n_lat, n_ctx=n_ctx, n_lat_tiles=n_lat_tiles)
            x_all = _outproj_ln(a, _mx(od_w_out[o]), x_all, mods[l], ln_g[l, 1], ln_b[l, 1],
                                gate_row=5, alpha=alpha, n_lat_tiles=n_lat_tiles)
        x_all, _ = _ffn(x_all, mods[l], wi, wo, ln_g[l, 2], ln_b[l, 2], layer=l, half=1,
                        mod_row=6, emit_row=None, alpha=alpha, n_lat_tiles=n_lat_tiles,
                        n_tiles=n_lat_tiles if last else n_tiles)
    return x_all[None]
```

```python
import functools
import math

import numpy as np
import jax
import jax.numpy as jnp
from jax import lax
from jax.experimental import pallas as pl
from jax.experimental.pallas import tpu as pltpu

F32 = jnp.float32
MXU_DTYPE = jnp.bfloat16

N_MOD = 9
LN_EPS = 1e-5
RMS_EPS = 1e-6
CHUNK = 64
ROPE_BASE = 10000.0
GRID_W = 64
GLA_HEADS = 4
GLA_DK = 128
GLA_DV = 256
GLA_GATE_RANK = 16
GLA_GATE_NORM = 16.0
RET_HEADS = 4
RET_DK = 128
RET_DV = 256
MLA_HEADS = 16
MLA_Q_RANK = 512
MLA_KV_RANK = 512
MLA_NOPE = 128
MLA_ROPE = 64
MLA_DV = 128
MLA_SCALE = (MLA_NOPE + MLA_ROPE) ** -0.5
LOG2E = 1.4426950408889634
FIXED_SHIFT_LIMIT = 2.0 ** 100

LANES = 128
BF16_SUBLANES = 16
ROW_TILE = 512
FF_CHUNK = 1024
LN_SLICE = 128
SCAN_ROWS = 256
MLA_PROJ_ROWS = 256
KV_TILE = 1280
SHIFT_KEYS = 256
QK_PAD = 256
ADA_COLS = 2048
VMEM_LIMIT = 56 * 1024 * 1024

EV_BLOCK = 1024
EV_GLA_QK, EV_GLA_V, EV_GLA_GATE, EV_RET_QK, EV_RET_V, EV_RET_GATE = range(6)
EV_DECAY_LANE_BLOCK = 6 * EV_BLOCK // LANES
EV_COLS = 6 * EV_BLOCK + 2 * LANES
EV_PROJ_COLS = EV_COLS // 2
MLA_IN_PAD = MLA_Q_RANK + MLA_KV_RANK + LANES


def _cparams(n_axes, vmem=VMEM_LIMIT):
    return pltpu.CompilerParams(
        dimension_semantics=("arbitrary",) * n_axes, vmem_limit_bytes=vmem)


def _mx(a):
    return a.astype(MXU_DTYPE)


def _dot(a, b):
    return jnp.dot(a, b, preferred_element_type=F32)


def _dot_nt(a, b):
    return lax.dot_general(a, b, (((1,), (1,)), ((), ())), preferred_element_type=F32)


def _dot_tn(a, b):
    return lax.dot_general(a, b, (((0,), (0,)), ((), ())), preferred_element_type=F32)


def _silu(v):
    return v / (1.0 + jnp.exp(-v))


def _layer_norm(v, g, b):
    mu = jnp.mean(v, axis=-1, keepdims=True)
    d = v - mu
    var = jnp.mean(d * d, axis=-1, keepdims=True)
    return d * lax.rsqrt(var + LN_EPS) * g + b


def _ada_kernel(c_ref, w_ref, b_ref, o_ref):
    s = _silu(c_ref[...])
    o_ref[...] = _dot(_mx(s), _mx(w_ref[...])) + b_ref[...]


def _ada(c_rows, w_ada, b_ada):
    depth, d, n = w_ada.shape
    rows = c_rows.shape[0]
    tn = ADA_COLS
    return pl.pallas_call(
        _ada_kernel,
        grid=(depth, n // tn),
        in_specs=[
            pl.BlockSpec((rows, d), lambda l, j: (0, 0)),
            pl.BlockSpec((None, d, tn), lambda l, j: (l, 0, j)),
            pl.BlockSpec((None, 1, tn), lambda l, j: (l, 0, j)),
        ],
        out_specs=pl.BlockSpec((None, rows, tn), lambda l, j: (l, 0, j)),
        out_shape=jax.ShapeDtypeStruct((depth, rows, n), F32),
        compiler_params=_cparams(2),
        name="ada_mod",
    )(c_rows, w_ada, b_ada.reshape(depth, 1, n))


def _ffn_kernel(*refs, mod_row, emit_row, alpha, overlap, n_tiles, tail_tile):
    if tail_tile is None:
        x_ref, xt_ref = refs[0], None
        mod_ref, wg_ref, wu_ref, wout_ref, g_ref, b_ref = refs[1:7]
        rest = refs[7:]
    else:
        x_ref, xt_ref, mod_ref, wg_ref, wu_ref, wout_ref, g_ref, b_ref = refs[:8]
        rest = refs[8:]
    if emit_row is None:
        o_ref, xm_scr, acc_scr, v_scr, em_scr = rest
        h_ref = None
    else:
        o_ref, h_ref, xm_scr, acc_scr, v_scr, em_scr = rest
    i, k = pl.program_id(0), pl.program_id(1)
    last_k = pl.num_programs(1) - 1
    n_slices = v_scr.shape[0] // LN_SLICE

    def read_x():
        if xt_ref is None:
            return x_ref[...]
        return jnp.where(i == tail_tile, xt_ref[...], x_ref[...])

    def finish_rows():
        r0 = pl.multiple_of(jnp.minimum(k, n_slices - 1) * LN_SLICE, LN_SLICE)
        y = _layer_norm(v_scr[pl.ds(r0, LN_SLICE), :], g_ref[...], b_ref[...])
        o_ref[...] = y
        if h_ref is not None:
            h_ref[...] = _mx(y * (1.0 + em_scr[1:2, :]) + em_scr[0:1, :])

    def accumulate(skip, first=False):
        xm = xm_scr[...]
        act = _silu(_dot(xm, wg_ref[:, skip:])) * _dot(xm, wu_ref[:, skip:])
        y = _dot(_mx(act), wout_ref[skip:, :])
        if first:
            acc_scr[...] = y
        else:
            acc_scr[...] += y

    @pl.when((i == 0) & (k == 0))
    def _():
        v_scr[...] = jnp.zeros_like(v_scr)
        em_scr[...] = jnp.zeros_like(em_scr)

    @pl.when((i < n_tiles) & (k == 0))
    def _():
        shift = mod_ref[mod_row:mod_row + 1, :]
        scale = mod_ref[mod_row + 1:mod_row + 2, :]
        xm_scr[...] = _mx(read_x() * (1.0 + scale) + shift)
        accumulate(0, first=True)
        finish_rows()

    @pl.when((i < n_tiles) & (k > 0) & (k < last_k))
    def _():
        accumulate(0)
        finish_rows()

    @pl.when((i < n_tiles) & (k == last_k))
    def _():
        accumulate(overlap)
        finish_rows()
        gate = mod_ref[mod_row + 2:mod_row + 3, :]
        v_scr[...] = alpha * read_x() + 0.5 * gate * acc_scr[...]
        if h_ref is not None:
            em_scr[...] = mod_ref[emit_row:emit_row + 2, :]

    @pl.when(i == n_tiles)
    def _():
        finish_rows()


def _ffn(x_all, mods, w_in, w_out, ln_g, ln_b, *, layer, half, mod_row, emit_row, alpha,
         n_lat_tiles, n_tiles, x_tail=None):
    d = x_all.shape[1]
    f = w_out.shape[2]
    tm, tk = ROW_TILE, FF_CHUNK
    n_chunks = -(-f // tk)
    overlap = n_chunks * tk - f
    rows = n_tiles * tm

    assert f % LANES == 0 and tk % LANES == 0 and n_chunks >= tm // LN_SLICE

    def win(i, k, base=0):
        k = jnp.where(i < n_tiles, k, n_chunks - 1)
        return (jnp.minimum(k * (tk // LANES), (f - tk) // LANES) + base // LANES) * LANES

    def src(i):
        return jnp.minimum(i, n_tiles - 1)

    n_slices = tm // LN_SLICE

    def dst(i, k):
        return jnp.where(i == 0, 0, (i - 1) * n_slices + jnp.minimum(k, n_slices - 1))

    out_shape = [jax.ShapeDtypeStruct((rows, d), F32)]
    out_specs = [pl.BlockSpec((LN_SLICE, d), lambda i, k: (dst(i, k), 0))]
    if emit_row is not None:
        out_shape.append(jax.ShapeDtypeStruct((rows, d), MXU_DTYPE))
        out_specs.append(pl.BlockSpec((LN_SLICE, d), lambda i, k: (dst(i, k), 0)))
    tail_tile = None if x_tail is None else n_tiles - 1
    kern = functools.partial(_ffn_kernel, mod_row=mod_row, emit_row=emit_row, alpha=alpha,
                             overlap=overlap, n_tiles=n_tiles, tail_tile=tail_tile)
    if x_tail is None:
        x_args = [x_all]
        x_specs = [pl.BlockSpec((tm, d), lambda i, k: (src(i), 0))]
    else:
        x_args = [x_all, x_tail]
        x_specs = [pl.BlockSpec((tm, d), lambda i, k: (jnp.minimum(i, tail_tile - 1), 0)),
                   pl.BlockSpec((tm, d), lambda i, k: (0, 0))]
    res = pl.pallas_call(
        kern,
        grid=(n_tiles + 1, n_chunks),
        in_specs=x_specs + [
            pl.BlockSpec((None, N_MOD, d), lambda i, k: (src(i) // n_lat_tiles, 0, 0)),
            pl.BlockSpec((None, None, pl.Element(d), pl.Element(tk)),
                         lambda i, k: (layer, half, 0, win(i, k))),
            pl.BlockSpec((None, None, pl.Element(d), pl.Element(tk)),
                         lambda i, k: (layer, half, 0, win(i, k, f))),
            pl.BlockSpec((None, None, pl.Element(tk), pl.Element(d)),
                         lambda i, k: (layer, half, win(i, k), 0)),
            pl.BlockSpec((1, d), lambda i, k: (0, 0)),
            pl.BlockSpec((1, d), lambda i, k: (0, 0)),
        ],
        out_specs=out_specs,
        out_shape=out_shape,
        scratch_shapes=[pltpu.VMEM((tm, d), MXU_DTYPE), pltpu.VMEM((tm, d), F32),
                        pltpu.VMEM((tm, d), F32), pltpu.VMEM((2, d), F32)],
        compiler_params=_cparams(2),
        name="ffn",
    )(*x_args, mods, w_in, w_in, w_out, ln_g.reshape(1, d), ln_b.reshape(1, d))
    return res if emit_row is not None else (res[0], None)


def _proj_kernel(h_ref, w_ref, o_ref):
    o_ref[...] = _dot(h_ref[...], w_ref[...])


def _proj(h, w, tn):
    rows, kdim = h.shape
    n = w.shape[1]
    tm = ROW_TILE
    return pl.pallas_call(
        _proj_kernel,
        grid=(n // tn, rows // tm),
        in_specs=[
            pl.BlockSpec((tm, kdim), lambda j, i: (i, 0)),
            pl.BlockSpec((kdim, tn), lambda j, i: (0, j)),
        ],
        out_specs=pl.BlockSpec((tm, tn), lambda j, i: (i, j)),
        out_shape=jax.ShapeDtypeStruct((rows, n), F32),
        compiler_params=_cparams(2),
        name="even_proj",
    )(h, w)


def _outproj_kernel(a_ref, w_ref, x_ref, mod_ref, g_ref, b_ref, o_ref, *, gate_row, alpha):
    y = _dot(a_ref[...], w_ref[...])
    gate = mod_ref[gate_row:gate_row + 1, :]
    v = alpha * x_ref[...] + gate * y
    o_ref[...] = _layer_norm(v, g_ref[...], b_ref[...])


def _outproj_ln(a, w, x_all, mods, ln_g, ln_b, *, gate_row, alpha, n_lat_tiles):
    rows, d = x_all.shape
    kdim = a.shape[1]
    tm = ROW_TILE
    kern = functools.partial(_outproj_kernel, gate_row=gate_row, alpha=alpha)
    return pl.pallas_call(
        kern,
        grid=(rows // tm,),
        in_specs=[
            pl.BlockSpec((tm, kdim), lambda i: (i, 0)),
            pl.BlockSpec((kdim, d), lambda i: (0, 0)),
            pl.BlockSpec((tm, d), lambda i: (i, 0)),
            pl.BlockSpec((None, N_MOD, d), lambda i: (i // n_lat_tiles, 0, 0)),
            pl.BlockSpec((1, d), lambda i: (0, 0)),
            pl.BlockSpec((1, d), lambda i: (0, 0)),
        ],
        out_specs=pl.BlockSpec((tm, d), lambda i: (i, 0)),
        out_shape=jax.ShapeDtypeStruct((rows, d), F32),
        compiler_params=_cparams(1),
        name="mix_out",
    )(a, w, x_all, mods, ln_g.reshape(1, d), ln_b.reshape(1, d))


_GLA_LEVELS = (64, 32, 16, 8, 4, 2)


def _scan_positions(reverse):
    idx = np.arange(CHUNK)
    return idx[::-1].copy() if reverse else idx


def _gla_exponent_matrix(reverse):
    pos = _scan_positions(reverse)
    pi, pt = pos[:, None], pos[None, :]
    mats = [pt <= pi,
            pt > pi]
    for n in _GLA_LEVELS:
        half = n // 2
        ref = (pi // n) * n + half - 1
        q_side = (pi % n) >= half
        mats.append(np.where(q_side, (pt > ref) & (pt <= pi), (pt > pi) & (pt <= ref)))
    return np.concatenate(mats, axis=0).astype(np.float32)


def _ret_constants(reverse):
    scales = np.arange(RET_HEADS, dtype=np.float64)
    off = 5.5 if reverse else 5.0
    log_gamma = np.log1p(-np.exp2(-off - scales))
    idx = np.arange(CHUNK, dtype=np.float64)
    diff = idx[:, None] - idx[None, :]
    dmat = np.where(diff >= 0, np.exp(log_gamma[:, None, None] * np.maximum(diff, 0.0)), 0.0)
    q_dec = np.exp(log_gamma[:, None] * (idx + 1.0))
    k_dec = np.exp(log_gamma[:, None] * (CHUNK - 1.0 - idx))
    s_dec = np.exp(log_gamma * CHUNK)
    if reverse:
        dmat = dmat[:, ::-1, ::-1]
        q_dec = q_dec[:, ::-1]
        k_dec = k_dec[:, ::-1]
    expand = lambda a: np.repeat(a.T, RET_DK, axis=1)
    return (dmat.astype(np.float32), expand(q_dec).astype(np.float32),
            expand(k_dec).astype(np.float32),
            np.repeat(s_dec, RET_DK)[None, :].astype(np.float32))


def _log_sigmoid(v):
    return jnp.minimum(v, 0.0) - jnp.log1p(jnp.exp(-jnp.abs(v)))


def _gla_masks(reverse):
    c = CHUNK
    row = lax.broadcasted_iota(jnp.int32, (c, c), 0)
    col = lax.broadcasted_iota(jnp.int32, (c, c), 1)
    pr, pc = ((c - 1) - row, (c - 1) - col) if reverse else (row, col)
    levels = []
    for n in _GLA_LEVELS:
        sh, half = int(math.log2(n)), n // 2
        levels.append((lax.shift_right_logical(pr, sh) == lax.shift_right_logical(pc, sh))
                      & ((pr & (n - 1)) >= half) & ((pc & (n - 1)) < half))
    return levels, row == col


def _rope128(xh, cos, sin_signed):
    return xh * cos + pltpu.roll(xh, 64, 1) * sin_signed


def _scan_chunk_step(streams, consts, out_refs, sg_scr, sr_scr, lgs, masks, cc):
    wg_ref, bg_ref, mexp_ref, dmat_ref, qdec_ref, kdec_ref, sdec_ref = consts
    c = CHUNK
    n_chunks = SCAN_ROWS // CHUNK
    hq = GLA_HEADS * GLA_DK
    hr = RET_HEADS * RET_DK
    o_off = GLA_HEADS * GLA_DV
    tasks = []
    for d in (0, 1):
        zqk, zv, zrqk, zrv, _, cos_ref, sin_ref = streams[d]
        reverse = d == 1
        r0 = ((n_chunks - 1 - cc) if reverse else cc) * c
        rows = slice(r0, r0 + c)
        g = lgs[d][rows]
        g_hi = _mx(g)
        g_lo = _mx(g - g_hi.astype(F32))
        mexp = mexp_ref[d]
        w_all = jnp.exp(_dot(mexp, g_hi) + _dot(mexp, g_lo))
        ones = jnp.ones((c, LANES), MXU_DTYPE)
        decay_rows = jnp.exp(_dot_tn(g_hi, ones) + _dot_tn(g_lo, ones))
        cs, sn = cos_ref[rows, :], sin_ref[rows, :]
        for h in range(GLA_HEADS):
            lanes = slice(h * GLA_DK, (h + 1) * GLA_DK)
            tasks.append(dict(
                kind="gla", d=d, rows=rows, w=w_all[:, lanes], decay=decay_rows[lanes],
                q=zqk[rows, lanes] * (GLA_DK ** -0.5),
                k=zqk[rows, hq + h * GLA_DK:hq + (h + 1) * GLA_DK],
                v=zv[rows, h * GLA_DV:(h + 1) * GLA_DV],
                st=sg_scr.at[d, h], cols=slice(h * GLA_DV, (h + 1) * GLA_DV)))
        for h in range(RET_HEADS):
            lanes = slice(h * RET_DK, (h + 1) * RET_DK)
            tasks.append(dict(
                kind="ret", d=d, rows=rows,
                q=_rope128(zrqk[rows, lanes], cs, sn),
                k=_rope128(zrqk[rows, hr + h * RET_DK:hr + (h + 1) * RET_DK]
                           * (RET_DK ** -0.5), cs, sn),
                v=zrv[rows, h * RET_DV:(h + 1) * RET_DV],
                dmat=dmat_ref[d, h], q_dec=qdec_ref[d][:, lanes], k_dec=kdec_ref[d][:, lanes],
                s_dec=sdec_ref[d][:, lanes], st=sr_scr.at[d, h],
                cols=slice(o_off + h * RET_DV, o_off + (h + 1) * RET_DV)))
    for t in tasks:
        t["s_t"] = t["st"][...]
        if t["kind"] == "gla":
            w, q, k = t["w"], t["q"], t["k"]
            t["inter"] = _dot(_mx(q * w[0:c]), _mx(t["s_t"]))
            t["lvl"] = [_dot_nt(_mx(q * w[(2 + li) * c:(3 + li) * c]),
                                _mx(k * w[(2 + li) * c:(3 + li) * c]))
                        for li in range(len(_GLA_LEVELS))]
        else:
            t["inter"] = _dot(_mx(t["q"] * t["q_dec"]), _mx(t["s_t"]))
            t["qk"] = _dot_nt(_mx(t["q"]), _mx(t["k"]))
    for t in tasks:
        if t["kind"] == "gla":
            level_masks, diag_mask = masks[t["d"]]
            a = jnp.where(diag_mask, jnp.sum(t["q"] * t["k"], axis=-1, keepdims=True), 0.0)
            for li in range(len(_GLA_LEVELS)):
                a = a + jnp.where(level_masks[li], t["lvl"][li], 0.0)
        else:
            a = t["qk"] * t["dmat"]
        t["a"] = _mx(a)
    for t in tasks:
        v = _mx(t["v"])
        out_refs[t["d"]][t["rows"], t["cols"]] = t["inter"] + _dot(t["a"], v)
        if t["kind"] == "gla":
            decay = jnp.concatenate([t["decay"]] * (GLA_DV // LANES), axis=1)
            k_dec = t["k"] * t["w"][c:2 * c]
        else:
            decay = t["s_dec"][:, 0:1]
            k_dec = t["k"] * t["k_dec"]
        t["st"][...] = decay * t["s_t"] + _dot_tn(_mx(k_dec), v)


def _scan_kernel(*refs):
    streams, consts = (refs[0:7], refs[7:14]), refs[14:21]
    of_ref, ob_ref, sg_scr, sr_scr = refs[21:]
    wg_ref, bg_ref = consts[0], consts[1]

    @pl.when(pl.program_id(0) == 0)
    def _():
        sg_scr[...] = jnp.zeros_like(sg_scr)
        sr_scr[...] = jnp.zeros_like(sr_scr)

    lgs = [_log_sigmoid(_dot(_mx(streams[d][4][...]), wg_ref[d]) + bg_ref[d])
           * (1.0 / GLA_GATE_NORM) for d in (0, 1)]
    masks = [_gla_masks(False), _gla_masks(True)]
    for cc in range(SCAN_ROWS // CHUNK):
        _scan_chunk_step(streams, consts, (of_ref, ob_ref), sg_scr, sr_scr, lgs, masks, cc)


def _even_scan(z, cos_t, sin_t, wg, bg, n_lat_blocks, n_seq_blocks):
    rows = z.shape[0]
    n_blocks = rows // SCAN_ROWS
    r = SCAN_ROWS

    def rf(i):
        return jnp.where(i < n_seq_blocks, (i + n_lat_blocks) % n_seq_blocks, i)

    def rb(i):
        return jnp.where(i < n_seq_blocks, n_seq_blocks - 1 - i, i)

    def stream_specs(rmap):
        return [
            pl.BlockSpec((r, EV_BLOCK), lambda i: (rmap(i), EV_GLA_QK)),
            pl.BlockSpec((r, EV_BLOCK), lambda i: (rmap(i), EV_GLA_V)),
            pl.BlockSpec((r, EV_BLOCK), lambda i: (rmap(i), EV_RET_QK)),
            pl.BlockSpec((r, EV_BLOCK), lambda i: (rmap(i), EV_RET_V)),
            pl.BlockSpec((r, LANES), lambda i: (rmap(i), EV_DECAY_LANE_BLOCK)),
            pl.BlockSpec((r, LANES), lambda i: (rmap(i), 0)),
            pl.BlockSpec((r, LANES), lambda i: (rmap(i), 0)),
        ]

    mexp = jnp.asarray(np.stack([_gla_exponent_matrix(False), _gla_exponent_matrix(True)]),
                       MXU_DTYPE)
    rc = [_ret_constants(False), _ret_constants(True)]
    dmat = jnp.asarray(np.stack([rc[0][0], rc[1][0]]))
    qdec = jnp.asarray(np.stack([rc[0][1], rc[1][1]]))
    kdec = jnp.asarray(np.stack([rc[0][2], rc[1][2]]))
    sdec = jnp.asarray(np.stack([rc[0][3], rc[1][3]]))
    consts = [wg, bg, mexp, dmat, qdec, kdec, sdec]

    def full_spec(a):
        nd = a.ndim
        return pl.BlockSpec(a.shape, lambda i: (0,) * nd)

    d_out = GLA_HEADS * GLA_DV + RET_HEADS * RET_DV
    return pl.pallas_call(
        _scan_kernel,
        grid=(n_blocks,),
        in_specs=stream_specs(rf) + stream_specs(rb) + [full_spec(a) for a in consts],
        out_specs=[pl.BlockSpec((r, d_out), lambda i: (rf(i), 0)),
                   pl.BlockSpec((r, d_out), lambda i: (rb(i), 0))],
        out_shape=[jax.ShapeDtypeStruct((rows, d_out), F32)] * 2,
        scratch_shapes=[pltpu.VMEM((2, GLA_HEADS, GLA_DK, GLA_DV), F32),
                        pltpu.VMEM((2, RET_HEADS, RET_DK, RET_DV), F32)],
        compiler_params=_cparams(1),
        name="even_scan",
    )(z, z, z, z, z, cos_t, sin_t, z, z, z, z, z, cos_t, sin_t, *consts)


def _evout_kernel(of_ref, ob_ref, gr_ref, rg_ref, nw_ref, w_ref, x_ref, mod_ref, g_ref, b_ref,
                  o_ref, *, gate_row, alpha):
    parts = []
    for h in range(GLA_HEADS + RET_HEADS):
        lanes = slice(h * GLA_DV, (h + 1) * GLA_DV)
        oh = of_ref[:, lanes] + ob_ref[:, lanes]
        y = oh * lax.rsqrt(jnp.mean(oh * oh, axis=-1, keepdims=True) + RMS_EPS)
        if h < GLA_HEADS:
            y = y * nw_ref[...]
            gate = gr_ref[:, h * GLA_DV:(h + 1) * GLA_DV]
        else:
            hh = h - GLA_HEADS
            gate = rg_ref[:, hh * RET_DV:(hh + 1) * RET_DV]
        parts.append(_mx(y * _silu(gate)))
    y = _dot(jnp.concatenate(parts, axis=1), w_ref[...])
    v = alpha * x_ref[...] + mod_ref[gate_row:gate_row + 1, :] * y
    o_ref[...] = _layer_norm(v, g_ref[...], b_ref[...])


def _even_out(o_f, o_b, z, norm_w, w, x_all, mods, ln_g, ln_b, *, gate_row, alpha, n_lat_tiles):
    rows, d = x_all.shape
    d_out = o_f.shape[1]
    tm = ROW_TILE
    kern = functools.partial(_evout_kernel, gate_row=gate_row, alpha=alpha)
    return pl.pallas_call(
        kern,
        grid=(rows // tm,),
        in_specs=[
            pl.BlockSpec((tm, d_out), lambda i: (i, 0)),
            pl.BlockSpec((tm, d_out), lambda i: (i, 0)),
            pl.BlockSpec((tm, EV_BLOCK), lambda i: (i, EV_GLA_GATE)),
            pl.BlockSpec((tm, EV_BLOCK), lambda i: (i, EV_RET_GATE)),
            pl.BlockSpec((1, GLA_DV), lambda i: (0, 0)),
            pl.BlockSpec((d_out, d), lambda i: (0, 0), pipeline_mode=pl.Buffered(1)),
            pl.BlockSpec((tm, d), lambda i: (i, 0)),
            pl.BlockSpec((None, N_MOD, d), lambda i: (i // n_lat_tiles, 0, 0)),
            pl.BlockSpec((1, d), lambda i: (0, 0)),
            pl.BlockSpec((1, d), lambda i: (0, 0)),
        ],
        out_specs=pl.BlockSpec((tm, d), lambda i: (i, 0)),
        out_shape=jax.ShapeDtypeStruct((rows, d), F32),
        compiler_params=_cparams(1),
        name="even_out",
    )(o_f, o_b, z, z, norm_w.reshape(1, GLA_DV), w, x_all, mods,
      ln_g.reshape(1, d), ln_b.reshape(1, d))


def _axial128(xh, cos, sin_a, sin_b):
    return xh * cos + pltpu.roll(xh, LANES - 16, 1) * sin_a + pltpu.roll(xh, 16, 1) * sin_b


def _axial_rows(xt, cos_t, sin_t):
    rot = jnp.concatenate([-xt[16:32], xt[0:16], -xt[48:64], xt[32:48]], axis=0)
    return xt * cos_t + rot * sin_t


def _mlaproj_kernel(h_ref, win_ref, qn_ref, kvn_ref, wuqt_ref, wuk_ref, wuvt_ref,
                    cos_ref, sina_ref, sinb_ref, cost_ref, sint_ref, qt_ref, k_ref, vt_ref):
    z = _dot(h_ref[...], win_ref[...])
    cq = z[:, :MLA_Q_RANK]
    ckv = z[:, MLA_Q_RANK:MLA_Q_RANK + MLA_KV_RANK]
    kr = z[:, MLA_Q_RANK + MLA_KV_RANK:]
    cos, sin_a, sin_b = cos_ref[...], sina_ref[...], sinb_ref[...]
    cqn = _mx(cq * lax.rsqrt(jnp.mean(cq * cq, axis=-1, keepdims=True) + RMS_EPS) * qn_ref[...])
    ckvn = _mx(ckv * lax.rsqrt(jnp.mean(ckv * ckv, axis=-1, keepdims=True) + RMS_EPS)
               * kvn_ref[...])
    qt = _dot_nt(wuqt_ref[...], cqn)
    kn = _dot(ckvn, wuk_ref[...])
    vt_ref[...] = _mx(_dot_nt(wuvt_ref[...], ckvn))
    kr_rot = _mx(_axial128(kr, cos, sin_a, sin_b))
    cos_t, sin_t = cost_ref[...], sint_ref[...]
    qs = MLA_SCALE * LOG2E
    r0, r1 = MLA_NOPE, MLA_NOPE + MLA_ROPE
    for h in range(MLA_HEADS):
        b0 = h * QK_PAD
        qt_ref[b0:b0 + r0, :] = _mx(qt[b0:b0 + r0] * qs)
        qt_ref[b0 + r0:b0 + r1, :] = _mx(_axial_rows(qt[b0 + r0:b0 + r1], cos_t, sin_t) * qs)
        qt_ref[b0 + r1:b0 + QK_PAD, :] = _mx(qt[b0 + r1:b0 + QK_PAD])
        k_ref[:, b0:b0 + LANES] = _mx(kn[:, h * MLA_NOPE:(h + 1) * MLA_NOPE])
        k_ref[:, b0 + LANES:b0 + QK_PAD] = kr_rot


def _mla_proj(h, w_in_p, q_norm, kv_norm, w_uqt, w_uk, w_uvt, tables):
    cos_t, sina_t, sinb_t, cos_rows, sin_rows = tables
    rows, d = h.shape
    tm = MLA_PROJ_ROWS
    hq = MLA_HEADS * QK_PAD
    hv = MLA_HEADS * MLA_DV

    def const2(a):
        return pl.BlockSpec(a.shape, lambda i: (0, 0))

    qn = q_norm.reshape(1, -1)
    kvn = kv_norm.reshape(1, -1)
    return pl.pallas_call(
        _mlaproj_kernel,
        grid=(rows // tm,),
        in_specs=[pl.BlockSpec((tm, d), lambda i: (i, 0)),
                  const2(w_in_p), const2(qn), const2(kvn), const2(w_uqt), const2(w_uk),
                  const2(w_uvt),
                  pl.BlockSpec((tm, LANES), lambda i: (i, 0)),
                  pl.BlockSpec((tm, LANES), lambda i: (i, 0)),
                  pl.BlockSpec((tm, LANES), lambda i: (i, 0)),
                  pl.BlockSpec((MLA_ROPE, tm), lambda i: (0, i)),
                  pl.BlockSpec((MLA_ROPE, tm), lambda i: (0, i))],
        out_specs=[pl.BlockSpec((hq, tm), lambda i: (0, i)),
                   pl.BlockSpec((tm, hq), lambda i: (i, 0)),
                   pl.BlockSpec((hv, tm), lambda i: (0, i))],
        out_shape=[jax.ShapeDtypeStruct((hq, rows), MXU_DTYPE),
                   jax.ShapeDtypeStruct((rows, hq), MXU_DTYPE),
                   jax.ShapeDtypeStruct((hv, rows), MXU_DTYPE)],
        compiler_params=_cparams(1),
        name="mla_proj",
    )(h, w_in_p, qn, kvn, w_uqt, w_uk, w_uvt, cos_t, sina_t, sinb_t, cos_rows, sin_rows)


def _flash_kernel(qt_ref, k_ref, vt_ref, o_ref, *, n_lat, n_ctx):
    tq = qt_ref.shape[1]
    qt = qt_ref[...]

    def scores(start, size):
        return _dot(k_ref[start:start + size, :], qt)

    def store(acc, l):
        o_ref[...] = (acc / l).T.astype(o_ref.dtype)

    def attend_online(kv_slices):
        m = jnp.full((1, tq), -jnp.inf, F32)
        l = jnp.zeros((1, tq), F32)
        acc = jnp.zeros((MLA_DV, tq), F32)
        for start, size in kv_slices:
            s = scores(start, size)
            m_new = jnp.maximum(m, jnp.max(s, axis=0, keepdims=True))
            alpha = jnp.exp2(m - m_new)
            p = jnp.exp2(s - m_new)
            l = alpha * l + jnp.sum(p, axis=0, keepdims=True)
            acc = alpha * acc + _dot(vt_ref[:, start:start + size], _mx(p))
            m = m_new
        store(acc, l)

    def attend_fixed_shift(kv_slices):
        m = jnp.max(scores(kv_slices[0][0], SHIFT_KEYS), axis=0, keepdims=True)
        l = jnp.zeros((1, tq), F32)
        acc = jnp.zeros((MLA_DV, tq), F32)
        for start, size in kv_slices:
            p = jnp.exp2(scores(start, size) - m)
            l = l + jnp.sum(p, axis=0, keepdims=True)
            acc = acc + _dot(vt_ref[:, start:start + size], _mx(p))
        store(acc, l)
        n_bad = (jnp.sum(jnp.where(l < FIXED_SHIFT_LIMIT, 0.0, 1.0))
                 + jnp.sum(jnp.where(jnp.abs(acc) < FIXED_SHIFT_LIMIT, 0.0, 1.0)))
        return n_bad == 0.0

    n_kv = n_lat + n_ctx
    i = pl.program_id(1)
    lat_slices = [(s0, KV_TILE) for s0 in range(0, n_kv, KV_TILE)]

    @pl.when(i > 0)
    def _():
        ok = attend_fixed_shift(lat_slices)

        @pl.when(jnp.logical_not(ok))
        def _():
            attend_online(lat_slices)

    @pl.when(i == 0)
    def _():
        attend_online([(n_lat, n_ctx)])


def _mla_attention(qt, k, vt, *, n_lat, n_ctx, n_lat_tiles):
    rows = k.shape[0]
    n_kv = n_lat + n_ctx
    tm = ROW_TILE
    kern = functools.partial(_flash_kernel, n_lat=n_lat, n_ctx=n_ctx)
    n_tiles = rows // tm
    assert n_tiles == n_lat_tiles + 1

    def tile(i):
        return (i + n_lat_tiles) % n_tiles

    return pl.pallas_call(
        kern,
        grid=(MLA_HEADS, n_tiles),
        in_specs=[pl.BlockSpec((QK_PAD, tm), lambda h, i: (h, tile(i))),
                  pl.BlockSpec((n_kv, QK_PAD), lambda h, i: (0, h)),
                  pl.BlockSpec((MLA_DV, n_kv), lambda h, i: (h, 0))],
        out_specs=pl.BlockSpec((tm, MLA_DV), lambda h, i: (tile(i), h)),
        out_shape=jax.ShapeDtypeStruct((rows, MLA_HEADS * MLA_DV), MXU_DTYPE),
        compiler_params=_cparams(2),
        name="mla_flash",
    )(qt, k, vt)


def _prep_even(w_in, wg_f, bg_f, wg_b, bg_b):
    r = GLA_GATE_RANK
    gla_qk, gla_v = 2 * GLA_HEADS * GLA_DK, GLA_HEADS * GLA_DV
    ret_qk, ret_v = 2 * RET_HEADS * RET_DK, RET_HEADS * RET_DV
    widths = [gla_qk, gla_v, gla_v, 2 * r, ret_qk, ret_v, ret_v]
    offs = np.cumsum([0] + widths)
    gqk, gv, gr, gd, rqk, rv, rg = [w_in[:, a:b] for a, b in zip(offs[:-1], offs[1:])]
    d = w_in.shape[0]
    pad = jnp.zeros((d, 2 * LANES - 2 * r), w_in.dtype)
    w = jnp.concatenate([gqk, gv, gr, rqk, rv, rg, gd, pad], axis=1)
    hq = GLA_HEADS * GLA_DK
    wg = jnp.zeros((2, LANES, hq), F32)
    wg = wg.at[0, 0:r].set(wg_f).at[1, r:2 * r].set(wg_b)
    bg = jnp.stack([bg_f, bg_b]).reshape(2, 1, hq)
    return _mx(w), _mx(wg), bg


def _prep_mla(w_in, w_uq, w_ukv):
    d = w_in.shape[0]
    w_in_p = _mx(jnp.pad(w_in, ((0, 0), (0, MLA_IN_PAD - w_in.shape[1]))))
    r = w_uq.shape[0]
    uq = w_uq.reshape(r, MLA_HEADS, MLA_NOPE + MLA_ROPE)
    uq = jnp.pad(uq, ((0, 0), (0, 0), (0, QK_PAD - MLA_NOPE - MLA_ROPE)))
    w_uqt = _mx(uq.reshape(r, MLA_HEADS * QK_PAD).T)
    ukv = w_ukv.reshape(w_ukv.shape[0], MLA_HEADS, MLA_NOPE + MLA_DV)
    w_uk = _mx(ukv[:, :, :MLA_NOPE].reshape(-1, MLA_HEADS * MLA_NOPE))
    w_uvt = _mx(ukv[:, :, MLA_NOPE:].reshape(-1, MLA_HEADS * MLA_DV).T)
    return w_in_p, w_uqt, w_uk, w_uvt


def _rope_angles(pos, dim):
    inv = ROPE_BASE ** (-jnp.arange(0, dim, 2, dtype=F32) / dim)
    ang = pos.astype(F32)[:, None] * inv[None, :]
    return jnp.concatenate([ang, ang], axis=-1)


def _ret_tables(n_lat, rows):
    ang = _rope_angles(jnp.arange(n_lat), RET_DK)
    cos = jnp.ones((rows, RET_DK), F32).at[:n_lat].set(jnp.cos(ang))
    sin = jnp.sin(ang)
    sign = jnp.where(jnp.arange(RET_DK) < RET_DK // 2, -1.0, 1.0)
    sin_s = jnp.zeros((rows, RET_DK), F32).at[:n_lat].set(sin * sign)
    return cos, sin_s


def _axial_tables(n_lat, rows):
    n_rows = n_lat // GRID_W
    row = jnp.repeat(jnp.arange(n_rows), GRID_W)
    col = jnp.tile(jnp.arange(GRID_W), n_rows)
    half = MLA_ROPE // 2
    ang = jnp.concatenate([_rope_angles(row, half), _rope_angles(col, half)], axis=-1)
    lane = jnp.arange(MLA_ROPE)
    lo = (lane % half) < (half // 2)
    cos = jnp.ones((rows, LANES), F32).at[:n_lat, :MLA_ROPE].set(jnp.cos(ang))
    sin = jnp.sin(ang)
    sin_a = jnp.zeros((rows, LANES), F32).at[:n_lat, :MLA_ROPE].set(jnp.where(lo, -sin, 0.0))
    sin_b = jnp.zeros((rows, LANES), F32).at[:n_lat, :MLA_ROPE].set(jnp.where(lo, 0.0, sin))
    cos_rows = jnp.ones((MLA_ROPE, rows), F32).at[:, :n_lat].set(jnp.cos(ang).T)
    sin_rows = jnp.zeros((MLA_ROPE, rows), F32).at[:, :n_lat].set(sin.T)
    return cos, sin_a, sin_b, cos_rows, sin_rows


def kernel(x, c, ctx, c_ctx, w_ada, b_ada, ln_g, ln_b, w_ffn_in, w_ffn_out, ev_w_in, ev_gla_wg_f, ev_gla_bg_f, ev_gla_wg_b, ev_gla_bg_b, ev_gla_norm, ev_w_out, od_w_in, od_q_norm, od_kv_norm, od_w_uq, od_w_ukv, od_w_out):
    n_lat, d = x.shape[1], x.shape[2]
    n_ctx = ctx.shape[1]
    depth = w_ada.shape[0]
    assert x.shape[0] == 1 and n_lat % ROW_TILE == 0 and n_ctx == SCAN_ROWS
    assert (n_lat + n_ctx) % KV_TILE == 0
    alpha = (2 * depth) ** 0.25
    n_lat_tiles = n_lat // ROW_TILE
    n_tiles = n_lat_tiles + 1
    rows = n_tiles * ROW_TILE
    n_seq_blocks = (n_lat + n_ctx) // SCAN_ROWS

    x_all = x[0]
    x_tail = jnp.concatenate([ctx[0], jnp.zeros((ROW_TILE - n_ctx, d), x.dtype)], axis=0)
    c_rows = jnp.zeros((BF16_SUBLANES, d), F32).at[0].set(c[0]).at[1].set(c_ctx)
    mods = _ada(c_rows, w_ada, b_ada)[:, :2].reshape(depth, 2, N_MOD, d)

    ret_cos, ret_sin = _ret_tables(n_lat, rows)
    ax_tables = _axial_tables(n_lat, rows)

    wi, wo = _mx(w_ffn_in), _mx(w_ffn_out)
    for l in range(depth):
        last = l == depth - 1
        x_all, h_mix = _ffn(x_all, mods[l], wi, wo, ln_g[l, 0], ln_b[l, 0], layer=l, half=0,
                            mod_row=0, emit_row=3, alpha=alpha, n_lat_tiles=n_lat_tiles,
                            n_tiles=n_tiles, x_tail=x_tail if l == 0 else None)
        if l % 2 == 0:
            e = l // 2
            w_p, wg, bg = _prep_even(ev_w_in[e], ev_gla_wg_f[e], ev_gla_bg_f[e],
                                     ev_gla_wg_b[e], ev_gla_bg_b[e])
            z = _proj(h_mix, w_p, EV_PROJ_COLS)
            o_f, o_b = _even_scan(z, ret_cos, ret_sin, wg, bg, n_lat // SCAN_ROWS, n_seq_blocks)
            x_all = _even_out(o_f, o_b, z, ev_gla_norm[e], _mx(ev_w_out[e]), x_all, mods[l],
                              ln_g[l, 1], ln_b[l, 1], gate_row=5, alpha=alpha,
                              n_lat_tiles=n_lat_tiles)
        else:
            o = l // 2
            w_in_p, w_uqt, w_uk, w_uvt = _prep_mla(od_w_in[o], od_w_uq[o], od_w_ukv[o])
            qt, k, vt = _mla_proj(h_mix, w_in_p, od_q_norm[o], od_kv_norm[o], w_uqt, w_uk,
                                  w_uvt, ax_tables)
            a = _mla_attention(qt, k, vt, n_lat=n_lat, n_ctx=n_ctx, n_lat_tiles=n_lat_tiles)
            x_all = _outproj_ln(a, _mx(od_w_out[o]), x_all, mods[l], ln_g[l, 1], ln_b[l, 1],
                                gate_row=5, alpha=alpha, n_lat_tiles=n_lat_tiles)
        x_all, _ = _ffn(x_all, mods[l], wi, wo, ln_g[l, 2], ln_b[l, 2], layer=l, half=1,
                        mod_row=6, emit_row=None, alpha=alpha, n_lat_tiles=n_lat_tiles,
                        n_tiles=n_lat_tiles if last else n_tiles)
    return x_all[None]
```

```python
import functools
import math

import numpy as np
import jax
import jax.numpy as jnp
from jax import lax
from jax.experimental import pallas as pl
from jax.experimental.pallas import tpu as pltpu

F32 = jnp.float32
MXU_DTYPE = jnp.bfloat16

N_MOD = 9
LN_EPS = 1e-5
RMS_EPS = 1e-6
CHUNK = 64
ROPE_BASE = 10000.0
GRID_W = 64
GLA_HEADS = 4
GLA_DK = 128
GLA_DV = 256
GLA_GATE_RANK = 16
GLA_GATE_NORM = 16.0
RET_HEADS = 4
RET_DK = 128
RET_DV = 256
MLA_HEADS = 16
MLA_Q_RANK = 512
MLA_KV_RANK = 512
MLA_NOPE = 128
MLA_ROPE = 64
MLA_DV = 128
MLA_SCALE = (MLA_NOPE + MLA_ROPE) ** -0.5
LOG2E = 1.4426950408889634
FIXED_SHIFT_LIMIT = 2.0 ** 100

LANES = 128
BF16_SUBLANES = 16
ROW_TILE = 512
FF_CHUNK = 1024
LN_SLICE = 96
SCAN_ROWS = 256
MLA_PROJ_ROWS = 256
KV_TILE = 1280
SHIFT_KEYS = 128
QK_PAD = 256
ADA_COLS = 2048
VMEM_LIMIT = 56 * 1024 * 1024

EV_BLOCK = 1024
EV_GLA_QK, EV_GLA_V, EV_GLA_GATE, EV_RET_QK, EV_RET_V, EV_RET_GATE = range(6)
EV_DECAY_LANE_BLOCK = 6 * EV_BLOCK // LANES
EV_COLS = 6 * EV_BLOCK + 2 * LANES
EV_PROJ_COLS = EV_COLS // 2
MLA_IN_PAD = MLA_Q_RANK + MLA_KV_RANK + LANES


def _cparams(n_axes, vmem=VMEM_LIMIT):
    return pltpu.CompilerParams(
        dimension_semantics=("arbitrary",) * n_axes, vmem_limit_bytes=vmem)


def _mx(a):
    return a.astype(MXU_DTYPE)


def _dot(a, b):
    return jnp.dot(a, b, preferred_element_type=F32)


def _dot_nt(a, b):
    return lax.dot_general(a, b, (((1,), (1,)), ((), ())), preferred_element_type=F32)


def _dot_tn(a, b):
    return lax.dot_general(a, b, (((0,), (0,)), ((), ())), preferred_element_type=F32)


def _silu(v):
    return v / (1.0 + jnp.exp(-v))


def _layer_norm(v, g, b):
    mu = jnp.mean(v, axis=-1, keepdims=True)
    d = v - mu
    var = jnp.mean(d * d, axis=-1, keepdims=True)
    return d * lax.rsqrt(var + LN_EPS) * g + b


def _ada_kernel(c_ref, w_ref, b_ref, o_ref):
    s = _silu(c_ref[...])
    o_ref[...] = _dot(_mx(s), _mx(w_ref[...])) + b_ref[...]


def _ada(c_rows, w_ada, b_ada):
    depth, d, n = w_ada.shape
    rows = c_rows.shape[0]
    tn = ADA_COLS
    return pl.pallas_call(
        _ada_kernel,
        grid=(depth, n // tn),
        in_specs=[
            pl.BlockSpec((rows, d), lambda l, j: (0, 0)),
            pl.BlockSpec((None, d, tn), lambda l, j: (l, 0, j)),
            pl.BlockSpec((None, 1, tn), lambda l, j: (l, 0, j)),
        ],
        out_specs=pl.BlockSpec((None, rows, tn), lambda l, j: (l, 0, j)),
        out_shape=jax.ShapeDtypeStruct((depth, rows, n), F32),
        compiler_params=_cparams(2),
        name="ada_mod",
    )(c_rows, w_ada, b_ada.reshape(depth, 1, n))


def _ffn_kernel(*refs, mod_row, emit_row, alpha, overlap, n_tiles, tail_tile):
    if tail_tile is None:
        x_ref, xt_ref = refs[0], None
        mod_ref, wg_ref, wu_ref, wout_ref, g_ref, b_ref = refs[1:7]
        rest = refs[7:]
    else:
        x_ref, xt_ref, mod_ref, wg_ref, wu_ref, wout_ref, g_ref, b_ref = refs[:8]
        rest = refs[8:]
    if emit_row is None:
        o_ref, xm_scr, acc_scr, v_scr, em_scr = rest
        h_ref = None
    else:
        o_ref, h_ref, xm_scr, acc_scr, v_scr, em_scr = rest
    i, k = pl.program_id(0), pl.program_id(1)
    last_k = pl.num_programs(1) - 1
    tm = v_scr.shape[0]

    def read_x():
        if xt_ref is None:
            return x_ref[...]
        return jnp.where(i == tail_tile, xt_ref[...], x_ref[...])

    def finish_rows():
        r0 = pl.multiple_of(jnp.minimum(k * LN_SLICE, tm - LN_SLICE), BF16_SUBLANES)
        y = _layer_norm(v_scr[pl.ds(r0, LN_SLICE), :], g_ref[...], b_ref[...])
        o_ref[...] = y
        if h_ref is not None:
            h_ref[...] = _mx(y * (1.0 + em_scr[1:2, :]) + em_scr[0:1, :])

    def accumulate(skip, first=False):
        xm = xm_scr[...]
        act = _silu(_dot(xm, wg_ref[:, skip:])) * _dot(xm, wu_ref[:, skip:])
        y = _dot(_mx(act), wout_ref[skip:, :])
        if first:
            acc_scr[...] = y
        else:
            acc_scr[...] += y

    @pl.when((i == 0) & (k == 0))
    def _():
        v_scr[...] = jnp.zeros_like(v_scr)
        em_scr[...] = jnp.zeros_like(em_scr)

    @pl.when((i < n_tiles) & (k == 0))
    def _():
        shift = mod_ref[mod_row:mod_row + 1, :]
        scale = mod_ref[mod_row + 1:mod_row + 2, :]
        xm_scr[...] = _mx(read_x() * (1.0 + scale) + shift)
        accumulate(0, first=True)
        finish_rows()

    @pl.when((i < n_tiles) & (k > 0) & (k < last_k))
    def _():
        accumulate(0)
        finish_rows()

    @pl.when((i < n_tiles) & (k == last_k))
    def _():
        accumulate(overlap)
        finish_rows()
        gate = mod_ref[mod_row + 2:mod_row + 3, :]
        v_scr[...] = alpha * read_x() + 0.5 * gate * acc_scr[...]
        if h_ref is not None:
            em_scr[...] = mod_ref[emit_row:emit_row + 2, :]

    @pl.when(i == n_tiles)
    def _():
        finish_rows()


def _ffn(x_all, mods, w_in, w_out, ln_g, ln_b, *, layer, half, mod_row, emit_row, alpha,
         n_lat_tiles, n_tiles, x_tail=None):
    d = x_all.shape[1]
    f = w_out.shape[2]
    tm, tk = ROW_TILE, FF_CHUNK
    n_chunks = -(-f // tk)
    overlap = n_chunks * tk - f
    rows = n_tiles * tm

    assert f % LANES == 0 and tk % LANES == 0 and n_chunks >= 2

    def win(i, k, base=0):
        k = jnp.where(i < n_tiles, k, n_chunks - 1)
        return (jnp.minimum(k * (tk // LANES), (f - tk) // LANES) + base // LANES) * LANES

    def src(i):
        return jnp.minimum(i, n_tiles - 1)

    g = BF16_SUBLANES
    assert tm % g == 0 and LN_SLICE % g == 0 and n_chunks * LN_SLICE >= tm

    def dst(i, k):
        r0 = jnp.minimum(k * (LN_SLICE // g), (tm - LN_SLICE) // g)
        return jnp.where(i == 0, 0, (i - 1) * (tm // g) + r0) * g

    slice_block = (pl.Element(LN_SLICE), pl.Element(d))
    out_shape = [jax.ShapeDtypeStruct((rows, d), F32)]
    out_specs = [pl.BlockSpec(slice_block, lambda i, k: (dst(i, k), 0))]
    if emit_row is not None:
        out_shape.append(jax.ShapeDtypeStruct((rows, d), MXU_DTYPE))
        out_specs.append(pl.BlockSpec(slice_block, lambda i, k: (dst(i, k), 0)))
    tail_tile = None if x_tail is None else n_tiles - 1
    kern = functools.partial(_ffn_kernel, mod_row=mod_row, emit_row=emit_row, alpha=alpha,
                             overlap=overlap, n_tiles=n_tiles, tail_tile=tail_tile)
    if x_tail is None:
        x_args = [x_all]
        x_specs = [pl.BlockSpec((tm, d), lambda i, k: (src(i), 0))]
    else:
        x_args = [x_all, x_tail]
        x_specs = [pl.BlockSpec((tm, d), lambda i, k: (jnp.minimum(i, tail_tile - 1), 0)),
                   pl.BlockSpec((tm, d), lambda i, k: (0, 0))]
    res = pl.pallas_call(
        kern,
        grid=(n_tiles + 1, n_chunks),
        in_specs=x_specs + [
            pl.BlockSpec((None, N_MOD, d), lambda i, k: (src(i) // n_lat_tiles, 0, 0)),
            pl.BlockSpec((None, None, pl.Element(d), pl.Element(tk)),
                         lambda i, k: (layer, half, 0, win(i, k))),
            pl.BlockSpec((None, None, pl.Element(d), pl.Element(tk)),
                         lambda i, k: (layer, half, 0, win(i, k, f))),
            pl.BlockSpec((None, None, pl.Element(tk), pl.Element(d)),
                         lambda i, k: (layer, half, win(i, k), 0)),
            pl.BlockSpec((1, d), lambda i, k: (0, 0)),
            pl.BlockSpec((1, d), lambda i, k: (0, 0)),
        ],
        out_specs=out_specs,
        out_shape=out_shape,
        scratch_shapes=[pltpu.VMEM((tm, d), MXU_DTYPE), pltpu.VMEM((tm, d), F32),
                        pltpu.VMEM((tm, d), F32), pltpu.VMEM((2, d), F32)],
        compiler_params=_cparams(2),
        name="ffn",
    )(*x_args, mods, w_in, w_in, w_out, ln_g.reshape(1, d), ln_b.reshape(1, d))
    return res if emit_row is not None else (res[0], None)


def _proj_kernel(h_ref, w_ref, o_ref):
    o_ref[...] = _dot(h_ref[...], w_ref[...])


def _proj(h, w, tn):
    rows, kdim = h.shape
    n = w.shape[1]
    tm = ROW_TILE
    return pl.pallas_call(
        _proj_kernel,
        grid=(n // tn, rows // tm),
        in_specs=[
            pl.BlockSpec((tm, kdim), lambda j, i: (i, 0)),
            pl.BlockSpec((kdim, tn), lambda j, i: (0, j)),
        ],
        out_specs=pl.BlockSpec((tm, tn), lambda j, i: (i, j)),
        out_shape=jax.ShapeDtypeStruct((rows, n), F32),
        compiler_params=_cparams(2),
        name="even_proj",
    )(h, w)


def _outproj_kernel(a_ref, w_ref, x_ref, mod_ref, g_ref, b_ref, o_ref, *, gate_row, alpha):
    y = _dot(a_ref[...], w_ref[...])
    gate = mod_ref[gate_row:gate_row + 1, :]
    v = alpha * x_ref[...] + gate * y
    o_ref[...] = _layer_norm(v, g_ref[...], b_ref[...])


def _outproj_ln(a, w, x_all, mods, ln_g, ln_b, *, gate_row, alpha, n_lat_tiles):
    rows, d = x_all.shape
    kdim = a.shape[1]
    tm = ROW_TILE
    kern = functools.partial(_outproj_kernel, gate_row=gate_row, alpha=alpha)
    return pl.pallas_call(
        kern,
        grid=(rows // tm,),
        in_specs=[
            pl.BlockSpec((tm, kdim), lambda i: (i, 0)),
            pl.BlockSpec((kdim, d), lambda i: (0, 0)),
            pl.BlockSpec((tm, d), lambda i: (i, 0)),
            pl.BlockSpec((None, N_MOD, d), lambda i: (i // n_lat_tiles, 0, 0)),
            pl.BlockSpec((1, d), lambda i: (0, 0)),
            pl.BlockSpec((1, d), lambda i: (0, 0)),
        ],
        out_specs=pl.BlockSpec((tm, d), lambda i: (i, 0)),
        out_shape=jax.ShapeDtypeStruct((rows, d), F32),
        compiler_params=_cparams(1),
        name="mix_out",
    )(a, w, x_all, mods, ln_g.reshape(1, d), ln_b.reshape(1, d))


_GLA_LEVELS = (64, 32, 16, 8, 4, 2)


def _scan_positions(reverse):
    idx = np.arange(CHUNK)
    return idx[::-1].copy() if reverse else idx


def _gla_exponent_matrix(reverse):
    pos = _scan_positions(reverse)
    pi, pt = pos[:, None], pos[None, :]
    mats = [pt <= pi,
            pt > pi]
    for n in _GLA_LEVELS:
        half = n // 2
        ref = (pi // n) * n + half - 1
        q_side = (pi % n) >= half
        mats.append(np.where(q_side, (pt > ref) & (pt <= pi), (pt > pi) & (pt <= ref)))
    return np.concatenate(mats, axis=0).astype(np.float32)


def _ret_constants(reverse):
    scales = np.arange(RET_HEADS, dtype=np.float64)
    off = 5.5 if reverse else 5.0
    log_gamma = np.log1p(-np.exp2(-off - scales))
    idx = np.arange(CHUNK, dtype=np.float64)
    diff = idx[:, None] - idx[None, :]
    dmat = np.where(diff >= 0, np.exp(log_gamma[:, None, None] * np.maximum(diff, 0.0)), 0.0)
    q_dec = np.exp(log_gamma[:, None] * (idx + 1.0))
    k_dec = np.exp(log_gamma[:, None] * (CHUNK - 1.0 - idx))
    s_dec = np.exp(log_gamma * CHUNK)
    if reverse:
        dmat = dmat[:, ::-1, ::-1]
        q_dec = q_dec[:, ::-1]
        k_dec = k_dec[:, ::-1]
    expand = lambda a: np.repeat(a.T, RET_DK, axis=1)
    return (dmat.astype(np.float32), expand(q_dec).astype(np.float32),
            expand(k_dec).astype(np.float32),
            np.repeat(s_dec, RET_DK)[None, :].astype(np.float32))


def _log_sigmoid(v):
    return jnp.minimum(v, 0.0) - jnp.log1p(jnp.exp(-jnp.abs(v)))


def _gla_masks(reverse):
    c = CHUNK
    row = lax.broadcasted_iota(jnp.int32, (c, c), 0)
    col = lax.broadcasted_iota(jnp.int32, (c, c), 1)
    pr, pc = ((c - 1) - row, (c - 1) - col) if reverse else (row, col)
    levels = []
    for n in _GLA_LEVELS:
        sh, half = int(math.log2(n)), n // 2
        levels.append((lax.shift_right_logical(pr, sh) == lax.shift_right_logical(pc, sh))
                      & ((pr & (n - 1)) >= half) & ((pc & (n - 1)) < half))
    return levels, row == col


def _rope128(xh, cos, sin_signed):
    return xh * cos + pltpu.roll(xh, 64, 1) * sin_signed


def _scan_chunk_step(streams, consts, out_refs, sg_scr, sr_scr, lgs, masks, cc):
    wg_ref, bg_ref, mexp_ref, dmat_ref, qdec_ref, kdec_ref, sdec_ref = consts
    c = CHUNK
    n_chunks = SCAN_ROWS // CHUNK
    hq = GLA_HEADS * GLA_DK
    hr = RET_HEADS * RET_DK
    o_off = GLA_HEADS * GLA_DV
    tasks = []
    for d in (0, 1):
        zqk, zv, zrqk, zrv, _, cos_ref, sin_ref = streams[d]
        reverse = d == 1
        r0 = ((n_chunks - 1 - cc) if reverse else cc) * c
        rows = slice(r0, r0 + c)
        g = lgs[d][rows]
        g_hi = _mx(g)
        g_lo = _mx(g - g_hi.astype(F32))
        mexp = mexp_ref[d]
        w_all = jnp.exp(_dot(mexp, g_hi) + _dot(mexp, g_lo))
        ones = jnp.ones((c, LANES), MXU_DTYPE)
        decay_rows = jnp.exp(_dot_tn(g_hi, ones) + _dot_tn(g_lo, ones))
        cs, sn = cos_ref[rows, :], sin_ref[rows, :]
        for h in range(GLA_HEADS):
            lanes = slice(h * GLA_DK, (h + 1) * GLA_DK)
            tasks.append(dict(
                kind="gla", d=d, rows=rows, w=w_all[:, lanes], decay=decay_rows[lanes],
                q=zqk[rows, lanes] * (GLA_DK ** -0.5),
                k=zqk[rows, hq + h * GLA_DK:hq + (h + 1) * GLA_DK],
                v=zv[rows, h * GLA_DV:(h + 1) * GLA_DV],
                st=sg_scr.at[d, h], cols=slice(h * GLA_DV, (h + 1) * GLA_DV)))
        for h in range(RET_HEADS):
            lanes = slice(h * RET_DK, (h + 1) * RET_DK)
            tasks.append(dict(
                kind="ret", d=d, rows=rows,
                q=_rope128(zrqk[rows, lanes], cs, sn),
                k=_rope128(zrqk[rows, hr + h * RET_DK:hr + (h + 1) * RET_DK]
                           * (RET_DK ** -0.5), cs, sn),
                v=zrv[rows, h * RET_DV:(h + 1) * RET_DV],
                dmat=dmat_ref[d, h], q_dec=qdec_ref[d][:, lanes], k_dec=kdec_ref[d][:, lanes],
                s_dec=sdec_ref[d][:, lanes], st=sr_scr.at[d, h],
                cols=slice(o_off + h * RET_DV, o_off + (h + 1) * RET_DV)))
    for t in tasks:
        t["s_t"] = t["st"][...]
        if t["kind"] == "gla":
            w, q, k = t["w"], t["q"], t["k"]
            t["inter"] = _dot(_mx(q * w[0:c]), _mx(t["s_t"]))
            t["lvl"] = [_dot_nt(_mx(q * w[(2 + li) * c:(3 + li) * c]),
                                _mx(k * w[(2 + li) * c:(3 + li) * c]))
                        for li in range(len(_GLA_LEVELS))]
        else:
            t["inter"] = _dot(_mx(t["q"] * t["q_dec"]), _mx(t["s_t"]))
            t["qk"] = _dot_nt(_mx(t["q"]), _mx(t["k"]))
    for t in tasks:
        if t["kind"] == "gla":
            level_masks, diag_mask = masks[t["d"]]
            a = jnp.where(diag_mask, jnp.sum(t["q"] * t["k"], axis=-1, keepdims=True), 0.0)
            for li in range(len(_GLA_LEVELS)):
                a = a + jnp.where(level_masks[li], t["lvl"][li], 0.0)
        else:
            a = t["qk"] * t["dmat"]
        t["a"] = _mx(a)
    for t in tasks:
        v = _mx(t["v"])
        out_refs[t["d"]][t["rows"], t["cols"]] = t["inter"] + _dot(t["a"], v)
        if t["kind"] == "gla":
            decay = jnp.concatenate([t["decay"]] * (GLA_DV // LANES), axis=1)
            k_dec = t["k"] * t["w"][c:2 * c]
        else:
            decay = t["s_dec"][:, 0:1]
            k_dec = t["k"] * t["k_dec"]
        t["st"][...] = decay * t["s_t"] + _dot_tn(_mx(k_dec), v)


def _scan_kernel(*refs):
    streams, consts = (refs[0:7], refs[7:14]), refs[14:21]
    of_ref, ob_ref, sg_scr, sr_scr = refs[21:]
    wg_ref, bg_ref = consts[0], consts[1]

    @pl.when(pl.program_id(0) == 0)
    def _():
        sg_scr[...] = jnp.zeros_like(sg_scr)
        sr_scr[...] = jnp.zeros_like(sr_scr)

    lgs = [_log_sigmoid(_dot(_mx(streams[d][4][...]), wg_ref[d]) + bg_ref[d])
           * (1.0 / GLA_GATE_NORM) for d in (0, 1)]
    masks = [_gla_masks(False), _gla_masks(True)]
    for cc in range(SCAN_ROWS // CHUNK):
        _scan_chunk_step(streams, consts, (of_ref, ob_ref), sg_scr, sr_scr, lgs, masks, cc)


def _even_scan(z, cos_t, sin_t, wg, bg, n_lat_blocks, n_seq_blocks):
    rows = z.shape[0]
    n_blocks = rows // SCAN_ROWS
    r = SCAN_ROWS

    def rf(i):
        return jnp.where(i < n_seq_blocks, (i + n_lat_blocks) % n_seq_blocks, i)

    def rb(i):
        return jnp.where(i < n_seq_blocks, n_seq_blocks - 1 - i, i)

    def stream_specs(rmap):
        return [
            pl.BlockSpec((r, EV_BLOCK), lambda i: (rmap(i), EV_GLA_QK)),
            pl.BlockSpec((r, EV_BLOCK), lambda i: (rmap(i), EV_GLA_V)),
            pl.BlockSpec((r, EV_BLOCK), lambda i: (rmap(i), EV_RET_QK)),
            pl.BlockSpec((r, EV_BLOCK), lambda i: (rmap(i), EV_RET_V)),
            pl.BlockSpec((r, LANES), lambda i: (rmap(i), EV_DECAY_LANE_BLOCK)),
            pl.BlockSpec((r, LANES), lambda i: (rmap(i), 0)),
            pl.BlockSpec((r, LANES), lambda i: (rmap(i), 0)),
        ]

    mexp = jnp.asarray(np.stack([_gla_exponent_matrix(False), _gla_exponent_matrix(True)]),
                       MXU_DTYPE)
    rc = [_ret_constants(False), _ret_constants(True)]
    dmat = jnp.asarray(np.stack([rc[0][0], rc[1][0]]))
    qdec = jnp.asarray(np.stack([rc[0][1], rc[1][1]]))
    kdec = jnp.asarray(np.stack([rc[0][2], rc[1][2]]))
    sdec = jnp.asarray(np.stack([rc[0][3], rc[1][3]]))
    consts = [wg, bg, mexp, dmat, qdec, kdec, sdec]

    def full_spec(a):
        nd = a.ndim
        return pl.BlockSpec(a.shape, lambda i: (0,) * nd)

    d_out = GLA_HEADS * GLA_DV + RET_HEADS * RET_DV
    return pl.pallas_call(
        _scan_kernel,
        grid=(n_blocks,),
        in_specs=stream_specs(rf) + stream_specs(rb) + [full_spec(a) for a in consts],
        out_specs=[pl.BlockSpec((r, d_out), lambda i: (rf(i), 0)),
                   pl.BlockSpec((r, d_out), lambda i: (rb(i), 0))],
        out_shape=[jax.ShapeDtypeStruct((rows, d_out), F32)] * 2,
        scratch_shapes=[pltpu.VMEM((2, GLA_HEADS, GLA_DK, GLA_DV), F32),
                        pltpu.VMEM((2, RET_HEADS, RET_DK, RET_DV), F32)],
        compiler_params=_cparams(1),
        name="even_scan",
    )(z, z, z, z, z, cos_t, sin_t, z, z, z, z, z, cos_t, sin_t, *consts)


def _evout_kernel(of_ref, ob_ref, gr_ref, rg_ref, nw_ref, w_ref, x_ref, mod_ref, g_ref, b_ref,
                  o_ref, *, gate_row, alpha):
    parts = []
    for h in range(GLA_HEADS + RET_HEADS):
        lanes = slice(h * GLA_DV, (h + 1) * GLA_DV)
        oh = of_ref[:, lanes] + ob_ref[:, lanes]
        y = oh * lax.rsqrt(jnp.mean(oh * oh, axis=-1, keepdims=True) + RMS_EPS)
        if h < GLA_HEADS:
            y = y * nw_ref[...]
            gate = gr_ref[:, h * GLA_DV:(h + 1) * GLA_DV]
        else:
            hh = h - GLA_HEADS
            gate = rg_ref[:, hh * RET_DV:(hh + 1) * RET_DV]
        parts.append(_mx(y * _silu(gate)))
    y = _dot(jnp.concatenate(parts, axis=1), w_ref[...])
    v = alpha * x_ref[...] + mod_ref[gate_row:gate_row + 1, :] * y
    o_ref[...] = _layer_norm(v, g_ref[...], b_ref[...])


def _even_out(o_f, o_b, z, norm_w, w, x_all, mods, ln_g, ln_b, *, gate_row, alpha, n_lat_tiles):
    rows, d = x_all.shape
    d_out = o_f.shape[1]
    tm = ROW_TILE
    kern = functools.partial(_evout_kernel, gate_row=gate_row, alpha=alpha)
    return pl.pallas_call(
        kern,
        grid=(rows // tm,),
        in_specs=[
            pl.BlockSpec((tm, d_out), lambda i: (i, 0)),
            pl.BlockSpec((tm, d_out), lambda i: (i, 0)),
            pl.BlockSpec((tm, EV_BLOCK), lambda i: (i, EV_GLA_GATE)),
            pl.BlockSpec((tm, EV_BLOCK), lambda i: (i, EV_RET_GATE)),
            pl.BlockSpec((1, GLA_DV), lambda i: (0, 0)),
            pl.BlockSpec((d_out, d), lambda i: (0, 0), pipeline_mode=pl.Buffered(1)),
            pl.BlockSpec((tm, d), lambda i: (i, 0)),
            pl.BlockSpec((None, N_MOD, d), lambda i: (i // n_lat_tiles, 0, 0)),
            pl.BlockSpec((1, d), lambda i: (0, 0)),
            pl.BlockSpec((1, d), lambda i: (0, 0)),
        ],
        out_specs=pl.BlockSpec((tm, d), lambda i: (i, 0)),
        out_shape=jax.ShapeDtypeStruct((rows, d), F32),
        compiler_params=_cparams(1),
        name="even_out",
    )(o_f, o_b, z, z, norm_w.reshape(1, GLA_DV), w, x_all, mods,
      ln_g.reshape(1, d), ln_b.reshape(1, d))


def _axial128(xh, cos, sin_a, sin_b):
    return xh * cos + pltpu.roll(xh, LANES - 16, 1) * sin_a + pltpu.roll(xh, 16, 1) * sin_b


def _axial_rows(xt, cos_t, sin_t):
    rot = jnp.concatenate([-xt[16:32], xt[0:16], -xt[48:64], xt[32:48]], axis=0)
    return xt * cos_t + rot * sin_t


def _mlaproj_kernel(h_ref, win_ref, qn_ref, kvn_ref, wuqt_ref, wuk_ref, wuvt_ref,
                    cos_ref, sina_ref, sinb_ref, cost_ref, sint_ref, qt_ref, k_ref, vt_ref):
    z = _dot(h_ref[...], win_ref[...])
    cq = z[:, :MLA_Q_RANK]
    ckv = z[:, MLA_Q_RANK:MLA_Q_RANK + MLA_KV_RANK]
    kr = z[:, MLA_Q_RANK + MLA_KV_RANK:]
    cos, sin_a, sin_b = cos_ref[...], sina_ref[...], sinb_ref[...]
    cqn = _mx(cq * lax.rsqrt(jnp.mean(cq * cq, axis=-1, keepdims=True) + RMS_EPS) * qn_ref[...])
    ckvn = _mx(ckv * lax.rsqrt(jnp.mean(ckv * ckv, axis=-1, keepdims=True) + RMS_EPS)
               * kvn_ref[...])
    qt = _dot_nt(wuqt_ref[...], cqn)
    kn = _dot(ckvn, wuk_ref[...])
    vt_ref[...] = _mx(_dot_nt(wuvt_ref[...], ckvn))
    kr_rot = _mx(_axial128(kr, cos, sin_a, sin_b))
    cos_t, sin_t = cost_ref[...], sint_ref[...]
    qs = MLA_SCALE * LOG2E
    r0, r1 = MLA_NOPE, MLA_NOPE + MLA_ROPE
    for h in range(MLA_HEADS):
        b0 = h * QK_PAD
        qt_ref[b0:b0 + r0, :] = _mx(qt[b0:b0 + r0] * qs)
        qt_ref[b0 + r0:b0 + r1, :] = _mx(_axial_rows(qt[b0 + r0:b0 + r1], cos_t, sin_t) * qs)
        qt_ref[b0 + r1:b0 + QK_PAD, :] = _mx(qt[b0 + r1:b0 + QK_PAD])
        k_ref[:, b0:b0 + LANES] = _mx(kn[:, h * MLA_NOPE:(h + 1) * MLA_NOPE])
        k_ref[:, b0 + LANES:b0 + QK_PAD] = kr_rot


def _mla_proj(h, w_in_p, q_norm, kv_norm, w_uqt, w_uk, w_uvt, tables):
    cos_t, sina_t, sinb_t, cos_rows, sin_rows = tables
    rows, d = h.shape
    tm = MLA_PROJ_ROWS
    hq = MLA_HEADS * QK_PAD
    hv = MLA_HEADS * MLA_DV

    def const2(a):
        return pl.BlockSpec(a.shape, lambda i: (0, 0))

    qn = q_norm.reshape(1, -1)
    kvn = kv_norm.reshape(1, -1)
    return pl.pallas_call(
        _mlaproj_kernel,
        grid=(rows // tm,),
        in_specs=[pl.BlockSpec((tm, d), lambda i: (i, 0)),
                  const2(w_in_p), const2(qn), const2(kvn), const2(w_uqt), const2(w_uk),
                  const2(w_uvt),
                  pl.BlockSpec((tm, LANES), lambda i: (i, 0)),
                  pl.BlockSpec((tm, LANES), lambda i: (i, 0)),
                  pl.BlockSpec((tm, LANES), lambda i: (i, 0)),
                  pl.BlockSpec((MLA_ROPE, tm), lambda i: (0, i)),
                  pl.BlockSpec((MLA_ROPE, tm), lambda i: (0, i))],
        out_specs=[pl.BlockSpec((hq, tm), lambda i: (0, i)),
                   pl.BlockSpec((tm, hq), lambda i: (i, 0)),
                   pl.BlockSpec((hv, tm), lambda i: (0, i))],
        out_shape=[jax.ShapeDtypeStruct((hq, rows), MXU_DTYPE),
                   jax.ShapeDtypeStruct((rows, hq), MXU_DTYPE),
                   jax.ShapeDtypeStruct((hv, rows), MXU_DTYPE)],
        compiler_params=_cparams(1),
        name="mla_proj",
    )(h, w_in_p, qn, kvn, w_uqt, w_uk, w_uvt, cos_t, sina_t, sinb_t, cos_rows, sin_rows)


def _flash_kernel(qt_ref, k_ref, vt_ref, o_ref, *, n_lat, n_ctx):
    tq = qt_ref.shape[1]
    qt = qt_ref[...]

    def scores(start, size):
        return _dot(k_ref[start:start + size, :], qt)

    def store(acc, l):
        o_ref[...] = (acc / l).T.astype(o_ref.dtype)

    def attend_online(kv_slices):
        m = jnp.full((1, tq), -jnp.inf, F32)
        l = jnp.zeros((1, tq), F32)
        acc = jnp.zeros((MLA_DV, tq), F32)
        for start, size in kv_slices:
            s = scores(start, size)
            m_new = jnp.maximum(m, jnp.max(s, axis=0, keepdims=True))
            alpha = jnp.exp2(m - m_new)
            p = jnp.exp2(s - m_new)
            l = alpha * l + jnp.sum(p, axis=0, keepdims=True)
            acc = alpha * acc + _dot(vt_ref[:, start:start + size], _mx(p))
            m = m_new
        store(acc, l)

    def attend_fixed_shift(kv_slices):
        m = jnp.max(scores(kv_slices[0][0], SHIFT_KEYS), axis=0, keepdims=True)
        l = jnp.zeros((1, tq), F32)
        acc = jnp.zeros((MLA_DV, tq), F32)
        for start, size in kv_slices:
            p = jnp.exp2(scores(start, size) - m)
            l = l + jnp.sum(p, axis=0, keepdims=True)
            acc = acc + _dot(vt_ref[:, start:start + size], _mx(p))
        store(acc, l)
        n_bad = (jnp.sum(jnp.where(l < FIXED_SHIFT_LIMIT, 0.0, 1.0))
                 + jnp.sum(jnp.where(jnp.abs(acc) < FIXED_SHIFT_LIMIT, 0.0, 1.0)))
        return n_bad == 0.0

    n_kv = n_lat + n_ctx
    i = pl.program_id(1)
    lat_slices = [(s0, KV_TILE) for s0 in range(0, n_kv, KV_TILE)]

    @pl.when(i > 0)
    def _():
        ok = attend_fixed_shift(lat_slices)

        @pl.when(jnp.logical_not(ok))
        def _():
            attend_online(lat_slices)

    @pl.when(i == 0)
    def _():
        attend_online([(n_lat, n_ctx)])


def _mla_attention(qt, k, vt, *, n_lat, n_ctx, n_lat_tiles):
    rows = k.shape[0]
    n_kv = n_lat + n_ctx
    tm = ROW_TILE
    kern = functools.partial(_flash_kernel, n_lat=n_lat, n_ctx=n_ctx)
    n_tiles = rows // tm
    assert n_tiles == n_lat_tiles + 1

    def tile(i):
        return (i + n_lat_tiles) % n_tiles

    return pl.pallas_call(
        kern,
        grid=(MLA_HEADS, n_tiles),
        in_specs=[pl.BlockSpec((QK_PAD, tm), lambda h, i: (h, tile(i))),
                  pl.BlockSpec((n_kv, QK_PAD), lambda h, i: (0, h)),
                  pl.BlockSpec((MLA_DV, n_kv), lambda h, i: (h, 0))],
        out_specs=pl.BlockSpec((tm, MLA_DV), lambda h, i: (tile(i), h)),
        out_shape=jax.ShapeDtypeStruct((rows, MLA_HEADS * MLA_DV), MXU_DTYPE),
        compiler_params=_cparams(2),
        name="mla_flash",
    )(qt, k, vt)


def _prep_even(w_in, wg_f, bg_f, wg_b, bg_b):
    r = GLA_GATE_RANK
    gla_qk, gla_v = 2 * GLA_HEADS * GLA_DK, GLA_HEADS * GLA_DV
    ret_qk, ret_v = 2 * RET_HEADS * RET_DK, RET_HEADS * RET_DV
    widths = [gla_qk, gla_v, gla_v, 2 * r, ret_qk, ret_v, ret_v]
    offs = np.cumsum([0] + widths)
    w_in = _mx(w_in)
    gqk, gv, gr, gd, rqk, rv, rg = [w_in[:, a:b] for a, b in zip(offs[:-1], offs[1:])]
    d = w_in.shape[0]
    pad = jnp.zeros((d, 2 * LANES - 2 * r), w_in.dtype)
    w = jnp.concatenate([gqk, gv, gr, rqk, rv, rg, gd, pad], axis=1)
    hq = GLA_HEADS * GLA_DK
    wg = jnp.zeros((2, LANES, hq), F32)
    wg = wg.at[0, 0:r].set(wg_f).at[1, r:2 * r].set(wg_b)
    bg = jnp.stack([bg_f, bg_b]).reshape(2, 1, hq)
    return w, _mx(wg), bg


def _prep_mla(w_in, w_uq, w_ukv):
    w_in, w_uq, w_ukv = _mx(w_in), _mx(w_uq), _mx(w_ukv)
    d = w_in.shape[0]
    w_in_p = jnp.pad(w_in, ((0, 0), (0, MLA_IN_PAD - w_in.shape[1])))
    r = w_uq.shape[0]
    uq = w_uq.reshape(r, MLA_HEADS, MLA_NOPE + MLA_ROPE)
    uq = jnp.pad(uq, ((0, 0), (0, 0), (0, QK_PAD - MLA_NOPE - MLA_ROPE)))
    w_uqt = _mx(uq.reshape(r, MLA_HEADS * QK_PAD).T)
    ukv = w_ukv.reshape(w_ukv.shape[0], MLA_HEADS, MLA_NOPE + MLA_DV)
    w_uk = _mx(ukv[:, :, :MLA_NOPE].reshape(-1, MLA_HEADS * MLA_NOPE))
    w_uvt = _mx(ukv[:, :, MLA_NOPE:].reshape(-1, MLA_HEADS * MLA_DV).T)
    return w_in_p, w_uqt, w_uk, w_uvt


def _rope_angles(pos, dim):
    inv = ROPE_BASE ** (-jnp.arange(0, dim, 2, dtype=F32) / dim)
    ang = pos.astype(F32)[:, None] * inv[None, :]
    return jnp.concatenate([ang, ang], axis=-1)


def _ret_tables(n_lat, rows):
    ang = _rope_angles(jnp.arange(n_lat), RET_DK)
    cos = jnp.ones((rows, RET_DK), F32).at[:n_lat].set(jnp.cos(ang))
    sin = jnp.sin(ang)
    sign = jnp.where(jnp.arange(RET_DK) < RET_DK // 2, -1.0, 1.0)
    sin_s = jnp.zeros((rows, RET_DK), F32).at[:n_lat].set(sin * sign)
    return cos, sin_s


def _axial_tables(n_lat, rows):
    n_rows = n_lat // GRID_W
    row = jnp.repeat(jnp.arange(n_rows), GRID_W)
    col = jnp.tile(jnp.arange(GRID_W), n_rows)
    half = MLA_ROPE // 2
    ang = jnp.concatenate([_rope_angles(row, half), _rope_angles(col, half)], axis=-1)
    lane = jnp.arange(MLA_ROPE)
    lo = (lane % half) < (half // 2)
    cos = jnp.ones((rows, LANES), F32).at[:n_lat, :MLA_ROPE].set(jnp.cos(ang))
    sin = jnp.sin(ang)
    sin_a = jnp.zeros((rows, LANES), F32).at[:n_lat, :MLA_ROPE].set(jnp.where(lo, -sin, 0.0))
    sin_b = jnp.zeros((rows, LANES), F32).at[:n_lat, :MLA_ROPE].set(jnp.where(lo, 0.0, sin))
    cos_rows = jnp.ones((MLA_ROPE, rows), F32).at[:, :n_lat].set(jnp.cos(ang).T)
    sin_rows = jnp.zeros((MLA_ROPE, rows), F32).at[:, :n_lat].set(sin.T)
    return cos, sin_a, sin_b, cos_rows, sin_rows


def kernel(x, c, ctx, c_ctx, w_ada, b_ada, ln_g, ln_b, w_ffn_in, w_ffn_out, ev_w_in, ev_gla_wg_f, ev_gla_bg_f, ev_gla_wg_b, ev_gla_bg_b, ev_gla_norm, ev_w_out, od_w_in, od_q_norm, od_kv_norm, od_w_uq, od_w_ukv, od_w_out):
    n_lat, d = x.shape[1], x.shape[2]
    n_ctx = ctx.shape[1]
    depth = w_ada.shape[0]
    assert x.shape[0] == 1 and n_lat % ROW_TILE == 0 and n_ctx == SCAN_ROWS
    assert (n_lat + n_ctx) % KV_TILE == 0
    alpha = (2 * depth) ** 0.25
    n_lat_tiles = n_lat // ROW_TILE
    n_tiles = n_lat_tiles + 1
    rows = n_tiles * ROW_TILE
    n_seq_blocks = (n_lat + n_ctx) // SCAN_ROWS

    x_all = x[0]
    x_tail = jnp.concatenate([ctx[0], jnp.zeros((ROW_TILE - n_ctx, d), x.dtype)], axis=0)
    c_rows = jnp.zeros((BF16_SUBLANES, d), F32).at[0].set(c[0]).at[1].set(c_ctx)
    mods = _ada(c_rows, w_ada, b_ada)[:, :2].reshape(depth, 2, N_MOD, d)

    ret_cos, ret_sin = _ret_tables(n_lat, rows)
    ax_tables = _axial_tables(n_lat, rows)

    wi, wo = _mx(w_ffn_in), _mx(w_ffn_out)
    for l in range(depth):
        last = l == depth - 1
        x_all, h_mix = _ffn(x_all, mods[l], wi, wo, ln_g[l, 0], ln_b[l, 0], layer=l, half=0,
                            mod_row=0, emit_row=3, alpha=alpha, n_lat_tiles=n_lat_tiles,
                            n_tiles=n_tiles, x_tail=x_tail if l == 0 else None)
        if l % 2 == 0:
            e = l // 2
            w_p, wg, bg = _prep_even(ev_w_in[e], ev_gla_wg_f[e], ev_gla_bg_f[e],
                                     ev_gla_wg_b[e], ev_gla_bg_b[e])
            z = _proj(h_mix, w_p, EV_PROJ_COLS)
            o_f, o_b = _even_scan(z, ret_cos, ret_sin, wg, bg, n_lat // SCAN_ROWS, n_seq_blocks)
            x_all = _even_out(o_f, o_b, z, ev_gla_norm[e], _mx(ev_w_out[e]), x_all, mods[l],
                              ln_g[l, 1], ln_b[l, 1], gate_row=5, alpha=alpha,
                              n_lat_tiles=n_lat_tiles)
        else:
            o = l // 2
            w_in_p, w_uqt, w_uk, w_uvt = _prep_mla(od_w_in[o], od_w_uq[o], od_w_ukv[o])
            qt, k, vt = _mla_proj(h_mix, w_in_p, od_q_norm[o], od_kv_norm[o], w_uqt, w_uk,
                                  w_uvt, ax_tables)
            a = _mla_attention(qt, k, vt, n_lat=n_lat, n_ctx=n_ctx, n_lat_tiles=n_lat_tiles)
            x_all = _outproj_ln(a, _mx(od_w_out[o]), x_all, mods[l], ln_g[l, 1], ln_b[l, 1],
                                gate_row=5, alpha=alpha, n_lat_tiles=n_lat_tiles)
        x_all, _ = _ffn(x_all, mods[l], wi, wo, ln_g[l, 2], ln_b[l, 2], layer=l, half=1,
                        mod_row=6, emit_row=None, alpha=alpha, n_lat_tiles=n_lat_tiles,
                        n_tiles=n_lat_tiles if last else n_tiles)
    return x_all[None]
```

```python
import functools
import math

import numpy as np
import jax
import jax.numpy as jnp
from jax import lax
from jax.experimental import pallas as pl
from jax.experimental.pallas import tpu as pltpu

F32 = jnp.float32
MXU_DTYPE = jnp.bfloat16

N_MOD = 9
LN_EPS = 1e-5
RMS_EPS = 1e-6
CHUNK = 64
ROPE_BASE = 10000.0
GRID_W = 64
GLA_HEADS = 4
GLA_DK = 128
GLA_DV = 256
GLA_GATE_RANK = 16
GLA_GATE_NORM = 16.0
RET_HEADS = 4
RET_DK = 128
RET_DV = 256
MLA_HEADS = 16
MLA_Q_RANK = 512
MLA_KV_RANK = 512
MLA_NOPE = 128
MLA_ROPE = 64
MLA_DV = 128
MLA_SCALE = (MLA_NOPE + MLA_ROPE) ** -0.5
LOG2E = 1.4426950408889634
FIXED_SHIFT_LIMIT = 2.0 ** 100

LANES = 128
BF16_SUBLANES = 16
ROW_TILE = 512
FF_CHUNK = 1024
LN_SLICE = 128
SCAN_ROWS = 256
MLA_PROJ_ROWS = 256
KV_TILE = 1280
SHIFT_KEYS = 256
QK_PAD = 256
ADA_COLS = 2048
VMEM_LIMIT = 56 * 1024 * 1024
FFN_VMEM_LIMIT = 60 * 1024 * 1024

EV_BLOCK = 1024
EV_GLA_QK, EV_GLA_V, EV_GLA_GATE, EV_RET_QK, EV_RET_V, EV_RET_GATE = range(6)
EV_DECAY_LANE_BLOCK = 6 * EV_BLOCK // LANES
EV_COLS = 6 * EV_BLOCK + 2 * LANES
EV_PROJ_COLS = EV_COLS // 2
MLA_IN_PAD = MLA_Q_RANK + MLA_KV_RANK + LANES


def _cparams(n_axes, vmem=VMEM_LIMIT):
    return pltpu.CompilerParams(
        dimension_semantics=("arbitrary",) * n_axes, vmem_limit_bytes=vmem)


def _mx(a):
    return a.astype(MXU_DTYPE)


def _dot(a, b):
    return jnp.dot(a, b, preferred_element_type=F32)


def _dot_nt(a, b):
    return lax.dot_general(a, b, (((1,), (1,)), ((), ())), preferred_element_type=F32)


def _dot_tn(a, b):
    return lax.dot_general(a, b, (((0,), (0,)), ((), ())), preferred_element_type=F32)


def _silu(v):
    return v / (1.0 + jnp.exp(-v))


def _layer_norm(v, g, b):
    mu = jnp.mean(v, axis=-1, keepdims=True)
    d = v - mu
    var = jnp.mean(d * d, axis=-1, keepdims=True)
    return d * lax.rsqrt(var + LN_EPS) * g + b


def _ada_kernel(c_ref, w_ref, b_ref, o_ref):
    s = _silu(c_ref[...])
    o_ref[...] = _dot(_mx(s), _mx(w_ref[...])) + b_ref[...]


def _ada(c_rows, w_ada, b_ada):
    depth, d, n = w_ada.shape
    rows = c_rows.shape[0]
    tn = ADA_COLS
    return pl.pallas_call(
        _ada_kernel,
        grid=(depth, n // tn),
        in_specs=[
            pl.BlockSpec((rows, d), lambda l, j: (0, 0)),
            pl.BlockSpec((None, d, tn), lambda l, j: (l, 0, j)),
            pl.BlockSpec((None, 1, tn), lambda l, j: (l, 0, j)),
        ],
        out_specs=pl.BlockSpec((None, rows, tn), lambda l, j: (l, 0, j)),
        out_shape=jax.ShapeDtypeStruct((depth, rows, n), F32),
        compiler_params=_cparams(2),
        name="ada_mod",
    )(c_rows, w_ada, b_ada.reshape(depth, 1, n))


def _ffn_kernel(*refs, mod_row, emit_row, alpha, n_tiles, tail_tile):
    if tail_tile is None:
        x_ref, xt_ref = refs[0], None
        refs = refs[1:]
    else:
        x_ref, xt_ref = refs[:2]
        refs = refs[2:]
    mod_ref, wg_ref, wu_ref, wout_ref, wg_t, wu_t, wout_t, g_ref, b_ref = refs[:9]
    rest = refs[9:]
    if emit_row is None:
        o_ref, xm_scr, acc_scr, v_scr, em_scr = rest
        h_ref = None
    else:
        o_ref, h_ref, xm_scr, acc_scr, v_scr, em_scr = rest
    i, k = pl.program_id(0), pl.program_id(1)
    last_k = pl.num_programs(1) - 1
    n_slices = v_scr.shape[0] // LN_SLICE

    def read_x():
        if xt_ref is None:
            return x_ref[...]
        return jnp.where(i == tail_tile, xt_ref[...], x_ref[...])

    def finish_rows():
        r0 = pl.multiple_of(jnp.minimum(k, n_slices - 1) * LN_SLICE, LN_SLICE)
        y = _layer_norm(v_scr[pl.ds(r0, LN_SLICE), :], g_ref[...], b_ref[...])
        o_ref[...] = y
        if h_ref is not None:
            h_ref[...] = _mx(y * (1.0 + em_scr[1:2, :]) + em_scr[0:1, :])

    def accumulate(wg, wu, wout, first=False):
        xm = xm_scr[...]
        act = _silu(_dot(xm, wg[...])) * _dot(xm, wu[...])
        y = _dot(_mx(act), wout[...])
        if first:
            acc_scr[...] = y
        else:
            acc_scr[...] += y

    @pl.when((i == 0) & (k == 0))
    def _():
        v_scr[...] = jnp.zeros_like(v_scr)
        em_scr[...] = jnp.zeros_like(em_scr)

    @pl.when((i < n_tiles) & (k == 0))
    def _():
        shift = mod_ref[mod_row:mod_row + 1, :]
        scale = mod_ref[mod_row + 1:mod_row + 2, :]
        xm_scr[...] = _mx(read_x() * (1.0 + scale) + shift)
        accumulate(wg_ref, wu_ref, wout_ref, first=True)
        finish_rows()

    @pl.when((i < n_tiles) & (k > 0) & (k < last_k))
    def _():
        accumulate(wg_ref, wu_ref, wout_ref)
        finish_rows()

    @pl.when((i < n_tiles) & (k == last_k))
    def _():
        accumulate(wg_t, wu_t, wout_t)
        finish_rows()
        gate = mod_ref[mod_row + 2:mod_row + 3, :]
        v_scr[...] = alpha * read_x() + 0.5 * gate * acc_scr[...]
        if h_ref is not None:
            em_scr[...] = mod_ref[emit_row:emit_row + 2, :]

    @pl.when(i == n_tiles)
    def _():
        finish_rows()


def _ffn(x_all, mods, w_in, w_out, ln_g, ln_b, *, layer, half, mod_row, emit_row, alpha,
         n_lat_tiles, n_tiles, x_tail=None):
    d = x_all.shape[1]
    f = w_out.shape[2]
    tm, tk = ROW_TILE, FF_CHUNK
    n_whole = f // tk
    tail = f - n_whole * tk
    n_chunks = n_whole + 1
    rows = n_tiles * tm

    assert f % LANES == 0 and tk % LANES == 0 and tail > 0 and n_chunks >= tm // LN_SLICE

    def win(i, k, base=0):
        k = jnp.where(i < n_tiles, jnp.minimum(k, n_whole - 1), n_whole - 1)
        return (k * (tk // LANES) + base // LANES) * LANES

    def resident(shape, *offsets):
        return pl.BlockSpec((None, None) + tuple(pl.Element(s) for s in shape),
                            lambda i, k: (layer, half) + offsets, pipeline_mode=pl.Buffered(1))

    def src(i):
        return jnp.minimum(i, n_tiles - 1)

    n_slices = tm // LN_SLICE

    def dst(i, k):
        return jnp.where(i == 0, 0, (i - 1) * n_slices + jnp.minimum(k, n_slices - 1))

    out_shape = [jax.ShapeDtypeStruct((rows, d), F32)]
    out_specs = [pl.BlockSpec((LN_SLICE, d), lambda i, k: (dst(i, k), 0))]
    if emit_row is not None:
        out_shape.append(jax.ShapeDtypeStruct((rows, d), MXU_DTYPE))
        out_specs.append(pl.BlockSpec((LN_SLICE, d), lambda i, k: (dst(i, k), 0)))
    tail_tile = None if x_tail is None else n_tiles - 1
    kern = functools.partial(_ffn_kernel, mod_row=mod_row, emit_row=emit_row, alpha=alpha,
                             n_tiles=n_tiles, tail_tile=tail_tile)
    if x_tail is None:
        x_args = [x_all]
        x_specs = [pl.BlockSpec((tm, d), lambda i, k: (src(i), 0))]
    else:
        x_args = [x_all, x_tail]
        x_specs = [pl.BlockSpec((tm, d), lambda i, k: (jnp.minimum(i, tail_tile - 1), 0)),
                   pl.BlockSpec((tm, d), lambda i, k: (0, 0))]
    res = pl.pallas_call(
        kern,
        grid=(n_tiles + 1, n_chunks),
        in_specs=x_specs + [
            pl.BlockSpec((None, N_MOD, d), lambda i, k: (src(i) // n_lat_tiles, 0, 0)),
            pl.BlockSpec((None, None, pl.Element(d), pl.Element(tk)),
                         lambda i, k: (layer, half, 0, win(i, k))),
            pl.BlockSpec((None, None, pl.Element(d), pl.Element(tk)),
                         lambda i, k: (layer, half, 0, win(i, k, f))),
            pl.BlockSpec((None, None, pl.Element(tk), pl.Element(d)),
                         lambda i, k: (layer, half, win(i, k), 0)),
            resident((d, tail), 0, n_whole * tk),
            resident((d, tail), 0, f + n_whole * tk),
            resident((tail, d), n_whole * tk, 0),
            pl.BlockSpec((1, d), lambda i, k: (0, 0)),
            pl.BlockSpec((1, d), lambda i, k: (0, 0)),
        ],
        out_specs=out_specs,
        out_shape=out_shape,
        scratch_shapes=[pltpu.VMEM((tm, d), MXU_DTYPE), pltpu.VMEM((tm, d), F32),
                        pltpu.VMEM((tm, d), F32), pltpu.VMEM((2, d), F32)],
        compiler_params=_cparams(2, vmem=FFN_VMEM_LIMIT),
        name="ffn",
    )(*x_args, mods, w_in, w_in, w_out, w_in, w_in, w_out,
      ln_g.reshape(1, d), ln_b.reshape(1, d))
    return res if emit_row is not None else (res[0], None)


def _proj_kernel(h_ref, w_ref, o_ref):
    o_ref[...] = _dot(h_ref[...], w_ref[...])


def _proj(h, w, tn):
    rows, kdim = h.shape
    n = w.shape[1]
    tm = ROW_TILE
    return pl.pallas_call(
        _proj_kernel,
        grid=(n // tn, rows // tm),
        in_specs=[
            pl.BlockSpec((tm, kdim), lambda j, i: (i, 0)),
            pl.BlockSpec((kdim, tn), lambda j, i: (0, j)),
        ],
        out_specs=pl.BlockSpec((tm, tn), lambda j, i: (i, j)),
        out_shape=jax.ShapeDtypeStruct((rows, n), F32),
        compiler_params=_cparams(2),
        name="even_proj",
    )(h, w)


def _outproj_kernel(a_ref, w_ref, x_ref, mod_ref, g_ref, b_ref, o_ref, *, gate_row, alpha):
    y = _dot(a_ref[...], w_ref[...])
    gate = mod_ref[gate_row:gate_row + 1, :]
    v = alpha * x_ref[...] + gate * y
    o_ref[...] = _layer_norm(v, g_ref[...], b_ref[...])


def _outproj_ln(a, w, x_all, mods, ln_g, ln_b, *, gate_row, alpha, n_lat_tiles):
    rows, d = x_all.shape
    kdim = a.shape[1]
    tm = ROW_TILE
    kern = functools.partial(_outproj_kernel, gate_row=gate_row, alpha=alpha)
    return pl.pallas_call(
        kern,
        grid=(rows // tm,),
        in_specs=[
            pl.BlockSpec((tm, kdim), lambda i: (i, 0)),
            pl.BlockSpec((kdim, d), lambda i: (0, 0)),
            pl.BlockSpec((tm, d), lambda i: (i, 0)),
            pl.BlockSpec((None, N_MOD, d), lambda i: (i // n_lat_tiles, 0, 0)),
            pl.BlockSpec((1, d), lambda i: (0, 0)),
            pl.BlockSpec((1, d), lambda i: (0, 0)),
        ],
        out_specs=pl.BlockSpec((tm, d), lambda i: (i, 0)),
        out_shape=jax.ShapeDtypeStruct((rows, d), F32),
        compiler_params=_cparams(1),
        name="mix_out",
    )(a, w, x_all, mods, ln_g.reshape(1, d), ln_b.reshape(1, d))


_GLA_LEVELS = (64, 32, 16, 8, 4, 2)


def _scan_positions(reverse):
    idx = np.arange(CHUNK)
    return idx[::-1].copy() if reverse else idx


def _gla_exponent_matrix(reverse):
    pos = _scan_positions(reverse)
    pi, pt = pos[:, None], pos[None, :]
    mats = [pt <= pi,
            pt > pi]
    for n in _GLA_LEVELS:
        half = n // 2
        ref = (pi // n) * n + half - 1
        q_side = (pi % n) >= half
        mats.append(np.where(q_side, (pt > ref) & (pt <= pi), (pt > pi) & (pt <= ref)))
    return np.concatenate(mats, axis=0).astype(np.float32)


def _ret_constants(reverse):
    scales = np.arange(RET_HEADS, dtype=np.float64)
    off = 5.5 if reverse else 5.0
    log_gamma = np.log1p(-np.exp2(-off - scales))
    idx = np.arange(CHUNK, dtype=np.float64)
    diff = idx[:, None] - idx[None, :]
    dmat = np.where(diff >= 0, np.exp(log_gamma[:, None, None] * np.maximum(diff, 0.0)), 0.0)
    q_dec = np.exp(log_gamma[:, None] * (idx + 1.0))
    k_dec = np.exp(log_gamma[:, None] * (CHUNK - 1.0 - idx))
    s_dec = np.exp(log_gamma * CHUNK)
    if reverse:
        dmat = dmat[:, ::-1, ::-1]
        q_dec = q_dec[:, ::-1]
        k_dec = k_dec[:, ::-1]
    expand = lambda a: np.repeat(a.T, RET_DK, axis=1)
    return (dmat.astype(np.float32), expand(q_dec).astype(np.float32),
            expand(k_dec).astype(np.float32),
            np.repeat(s_dec, RET_DK)[None, :].astype(np.float32))


def _log_sigmoid(v):
    return jnp.minimum(v, 0.0) - jnp.log1p(jnp.exp(-jnp.abs(v)))


def _gla_masks(reverse):
    c = CHUNK
    row = lax.broadcasted_iota(jnp.int32, (c, c), 0)
    col = lax.broadcasted_iota(jnp.int32, (c, c), 1)
    pr, pc = ((c - 1) - row, (c - 1) - col) if reverse else (row, col)
    levels = []
    for n in _GLA_LEVELS:
        sh, half = int(math.log2(n)), n // 2
        levels.append((lax.shift_right_logical(pr, sh) == lax.shift_right_logical(pc, sh))
                      & ((pr & (n - 1)) >= half) & ((pc & (n - 1)) < half))
    return levels, row == col


def _rope128(xh, cos, sin_signed):
    return xh * cos + pltpu.roll(xh, 64, 1) * sin_signed


def _scan_chunk_step(streams, consts, out_refs, sg_scr, sr_scr, lgs, masks, cc):
    wg_ref, bg_ref, mexp_ref, dmat_ref, qdec_ref, kdec_ref, sdec_ref = consts
    c = CHUNK
    n_chunks = SCAN_ROWS // CHUNK
    hq = GLA_HEADS * GLA_DK
    hr = RET_HEADS * RET_DK
    o_off = GLA_HEADS * GLA_DV
    tasks = []
    for d in (0, 1):
        zqk, zv, zrqk, zrv, _, cos_ref, sin_ref = streams[d]
        reverse = d == 1
        r0 = ((n_chunks - 1 - cc) if reverse else cc) * c
        rows = slice(r0, r0 + c)
        g = lgs[d][rows]
        g_hi = _mx(g)
        g_lo = _mx(g - g_hi.astype(F32))
        mexp = mexp_ref[d]
        w_all = jnp.exp(_dot(mexp, g_hi) + _dot(mexp, g_lo))
        ones = jnp.ones((c, LANES), MXU_DTYPE)
        decay_rows = jnp.exp(_dot_tn(g_hi, ones) + _dot_tn(g_lo, ones))
        cs, sn = cos_ref[rows, :], sin_ref[rows, :]
        for h in range(GLA_HEADS):
            lanes = slice(h * GLA_DK, (h + 1) * GLA_DK)
            tasks.append(dict(
                kind="gla", d=d, rows=rows, w=w_all[:, lanes], decay=decay_rows[lanes],
                q=zqk[rows, lanes] * (GLA_DK ** -0.5),
                k=zqk[rows, hq + h * GLA_DK:hq + (h + 1) * GLA_DK],
                v=zv[rows, h * GLA_DV:(h + 1) * GLA_DV],
                st=sg_scr.at[d, h], cols=slice(h * GLA_DV, (h + 1) * GLA_DV)))
        for h in range(RET_HEADS):
            lanes = slice(h * RET_DK, (h + 1) * RET_DK)
            tasks.append(dict(
                kind="ret", d=d, rows=rows,
                q=_rope128(zrqk[rows, lanes], cs, sn),
                k=_rope128(zrqk[rows, hr + h * RET_DK:hr + (h + 1) * RET_DK]
                           * (RET_DK ** -0.5), cs, sn),
                v=zrv[rows, h * RET_DV:(h + 1) * RET_DV],
                dmat=dmat_ref[d, h], q_dec=qdec_ref[d][:, lanes], k_dec=kdec_ref[d][:, lanes],
                s_dec=sdec_ref[d][:, lanes], st=sr_scr.at[d, h],
                cols=slice(o_off + h * RET_DV, o_off + (h + 1) * RET_DV)))
    for t in tasks:
        t["s_t"] = t["st"][...]
        if t["kind"] == "gla":
            w, q, k = t["w"], t["q"], t["k"]
            t["inter"] = _dot(_mx(q * w[0:c]), _mx(t["s_t"]))
            t["lvl"] = [_dot_nt(_mx(q * w[(2 + li) * c:(3 + li) * c]),
                                _mx(k * w[(2 + li) * c:(3 + li) * c]))
                        for li in range(len(_GLA_LEVELS))]
        else:
            t["inter"] = _dot(_mx(t["q"] * t["q_dec"]), _mx(t["s_t"]))
            t["qk"] = _dot_nt(_mx(t["q"]), _mx(t["k"]))
    for t in tasks:
        if t["kind"] == "gla":
            level_masks, diag_mask = masks[t["d"]]
            a = jnp.where(diag_mask, jnp.sum(t["q"] * t["k"], axis=-1, keepdims=True), 0.0)
            for li in range(len(_GLA_LEVELS)):
                a = a + jnp.where(level_masks[li], t["lvl"][li], 0.0)
        else:
            a = t["qk"] * t["dmat"]
        t["a"] = _mx(a)
    for t in tasks:
        v = _mx(t["v"])
        out_refs[t["d"]][t["rows"], t["cols"]] = t["inter"] + _dot(t["a"], v)
        if t["kind"] == "gla":
            decay = jnp.concatenate([t["decay"]] * (GLA_DV // LANES), axis=1)
            k_dec = t["k"] * t["w"][c:2 * c]
        else:
            decay = t["s_dec"][:, 0:1]
            k_dec = t["k"] * t["k_dec"]
        t["st"][...] = decay * t["s_t"] + _dot_tn(_mx(k_dec), v)


def _scan_kernel(*refs):
    streams, consts = (refs[0:7], refs[7:14]), refs[14:21]
    of_ref, ob_ref, sg_scr, sr_scr = refs[21:]
    wg_ref, bg_ref = consts[0], consts[1]

    @pl.when(pl.program_id(0) == 0)
    def _():
        sg_scr[...] = jnp.zeros_like(sg_scr)
        sr_scr[...] = jnp.zeros_like(sr_scr)

    lgs = [_log_sigmoid(_dot(_mx(streams[d][4][...]), wg_ref[d]) + bg_ref[d])
           * (1.0 / GLA_GATE_NORM) for d in (0, 1)]
    masks = [_gla_masks(False), _gla_masks(True)]
    for cc in range(SCAN_ROWS // CHUNK):
        _scan_chunk_step(streams, consts, (of_ref, ob_ref), sg_scr, sr_scr, lgs, masks, cc)


def _even_scan(z, cos_t, sin_t, wg, bg, n_lat_blocks, n_seq_blocks):
    rows = z.shape[0]
    n_blocks = rows // SCAN_ROWS
    r = SCAN_ROWS

    def rf(i):
        return jnp.where(i < n_seq_blocks, (i + n_lat_blocks) % n_seq_blocks, i)

    def rb(i):
        return jnp.where(i < n_seq_blocks, n_seq_blocks - 1 - i, i)

    def stream_specs(rmap):
        return [
            pl.BlockSpec((r, EV_BLOCK), lambda i: (rmap(i), EV_GLA_QK)),
            pl.BlockSpec((r, EV_BLOCK), lambda i: (rmap(i), EV_GLA_V)),
            pl.BlockSpec((r, EV_BLOCK), lambda i: (rmap(i), EV_RET_QK)),
            pl.BlockSpec((r, EV_BLOCK), lambda i: (rmap(i), EV_RET_V)),
            pl.BlockSpec((r, LANES), lambda i: (rmap(i), EV_DECAY_LANE_BLOCK)),
            pl.BlockSpec((r, LANES), lambda i: (rmap(i), 0)),
            pl.BlockSpec((r, LANES), lambda i: (rmap(i), 0)),
        ]

    mexp = jnp.asarray(np.stack([_gla_exponent_matrix(False), _gla_exponent_matrix(True)]),
                       MXU_DTYPE)
    rc = [_ret_constants(False), _ret_constants(True)]
    dmat = jnp.asarray(np.stack([rc[0][0], rc[1][0]]))
    qdec = jnp.asarray(np.stack([rc[0][1], rc[1][1]]))
    kdec = jnp.asarray(np.stack([rc[0][2], rc[1][2]]))
    sdec = jnp.asarray(np.stack([rc[0][3], rc[1][3]]))
    consts = [wg, bg, mexp, dmat, qdec, kdec, sdec]

    def full_spec(a):
        nd = a.ndim
        return pl.BlockSpec(a.shape, lambda i: (0,) * nd)

    d_out = GLA_HEADS * GLA_DV + RET_HEADS * RET_DV
    return pl.pallas_call(
        _scan_kernel,
        grid=(n_blocks,),
        in_specs=stream_specs(rf) + stream_specs(rb) + [full_spec(a) for a in consts],
        out_specs=[pl.BlockSpec((r, d_out), lambda i: (rf(i), 0)),
                   pl.BlockSpec((r, d_out), lambda i: (rb(i), 0))],
        out_shape=[jax.ShapeDtypeStruct((rows, d_out), F32)] * 2,
        scratch_shapes=[pltpu.VMEM((2, GLA_HEADS, GLA_DK, GLA_DV), F32),
                        pltpu.VMEM((2, RET_HEADS, RET_DK, RET_DV), F32)],
        compiler_params=_cparams(1),
        name="even_scan",
    )(z, z, z, z, z, cos_t, sin_t, z, z, z, z, z, cos_t, sin_t, *consts)


def _evout_kernel(of_ref, ob_ref, gr_ref, rg_ref, nw_ref, w_ref, x_ref, mod_ref, g_ref, b_ref,
                  o_ref, *, gate_row, alpha):
    parts = []
    for h in range(GLA_HEADS + RET_HEADS):
        lanes = slice(h * GLA_DV, (h + 1) * GLA_DV)
        oh = of_ref[:, lanes] + ob_ref[:, lanes]
        y = oh * lax.rsqrt(jnp.mean(oh * oh, axis=-1, keepdims=True) + RMS_EPS)
        if h < GLA_HEADS:
            y = y * nw_ref[...]
            gate = gr_ref[:, h * GLA_DV:(h + 1) * GLA_DV]
        else:
            hh = h - GLA_HEADS
            gate = rg_ref[:, hh * RET_DV:(hh + 1) * RET_DV]
        parts.append(_mx(y * _silu(gate)))
    y = _dot(jnp.concatenate(parts, axis=1), w_ref[...])
    v = alpha * x_ref[...] + mod_ref[gate_row:gate_row + 1, :] * y
    o_ref[...] = _layer_norm(v, g_ref[...], b_ref[...])


def _even_out(o_f, o_b, z, norm_w, w, x_all, mods, ln_g, ln_b, *, gate_row, alpha, n_lat_tiles):
    rows, d = x_all.shape
    d_out = o_f.shape[1]
    tm = ROW_TILE
    kern = functools.partial(_evout_kernel, gate_row=gate_row, alpha=alpha)
    return pl.pallas_call(
        kern,
        grid=(rows // tm,),
        in_specs=[
            pl.BlockSpec((tm, d_out), lambda i: (i, 0)),
            pl.BlockSpec((tm, d_out), lambda i: (i, 0)),
            pl.BlockSpec((tm, EV_BLOCK), lambda i: (i, EV_GLA_GATE)),
            pl.BlockSpec((tm, EV_BLOCK), lambda i: (i, EV_RET_GATE)),
            pl.BlockSpec((1, GLA_DV), lambda i: (0, 0)),
            pl.BlockSpec((d_out, d), lambda i: (0, 0), pipeline_mode=pl.Buffered(1)),
            pl.BlockSpec((tm, d), lambda i: (i, 0)),
            pl.BlockSpec((None, N_MOD, d), lambda i: (i // n_lat_tiles, 0, 0)),
            pl.BlockSpec((1, d), lambda i: (0, 0)),
            pl.BlockSpec((1, d), lambda i: (0, 0)),
        ],
        out_specs=pl.BlockSpec((tm, d), lambda i: (i, 0)),
        out_shape=jax.ShapeDtypeStruct((rows, d), F32),
        compiler_params=_cparams(1),
        name="even_out",
    )(o_f, o_b, z, z, norm_w.reshape(1, GLA_DV), w, x_all, mods,
      ln_g.reshape(1, d), ln_b.reshape(1, d))


def _axial128(xh, cos, sin_a, sin_b):
    return xh * cos + pltpu.roll(xh, LANES - 16, 1) * sin_a + pltpu.roll(xh, 16, 1) * sin_b


def _axial_rows(xt, cos_t, sin_t):
    rot = jnp.concatenate([-xt[16:32], xt[0:16], -xt[48:64], xt[32:48]], axis=0)
    return xt * cos_t + rot * sin_t


def _mlaproj_kernel(h_ref, win_ref, qn_ref, kvn_ref, wuqt_ref, wuk_ref, wuvt_ref,
                    cos_ref, sina_ref, sinb_ref, cost_ref, sint_ref, qt_ref, k_ref, vt_ref):
    z = _dot(h_ref[...], win_ref[...])
    cq = z[:, :MLA_Q_RANK]
    ckv = z[:, MLA_Q_RANK:MLA_Q_RANK + MLA_KV_RANK]
    kr = z[:, MLA_Q_RANK + MLA_KV_RANK:]
    cos, sin_a, sin_b = cos_ref[...], sina_ref[...], sinb_ref[...]
    cqn = _mx(cq * lax.rsqrt(jnp.mean(cq * cq, axis=-1, keepdims=True) + RMS_EPS) * qn_ref[...])
    ckvn = _mx(ckv * lax.rsqrt(jnp.mean(ckv * ckv, axis=-1, keepdims=True) + RMS_EPS)
               * kvn_ref[...])
    qt = _dot_nt(wuqt_ref[...], cqn)
    kn = _dot(ckvn, wuk_ref[...])
    vt_ref[...] = _mx(_dot_nt(wuvt_ref[...], ckvn))
    kr_rot = _mx(_axial128(kr, cos, sin_a, sin_b))
    cos_t, sin_t = cost_ref[...], sint_ref[...]
    qs = MLA_SCALE * LOG2E
    r0, r1 = MLA_NOPE, MLA_NOPE + MLA_ROPE
    for h in range(MLA_HEADS):
        b0 = h * QK_PAD
        qt_ref[b0:b0 + r0, :] = _mx(qt[b0:b0 + r0] * qs)
        qt_ref[b0 + r0:b0 + r1, :] = _mx(_axial_rows(qt[b0 + r0:b0 + r1], cos_t, sin_t) * qs)
        qt_ref[b0 + r1:b0 + QK_PAD, :] = _mx(qt[b0 + r1:b0 + QK_PAD])
        k_ref[:, b0:b0 + LANES] = _mx(kn[:, h * MLA_NOPE:(h + 1) * MLA_NOPE])
        k_ref[:, b0 + LANES:b0 + QK_PAD] = kr_rot


def _mla_proj(h, w_in_p, q_norm, kv_norm, w_uqt, w_uk, w_uvt, tables):
    cos_t, sina_t, sinb_t, cos_rows, sin_rows = tables
    rows, d = h.shape
    tm = MLA_PROJ_ROWS
    hq = MLA_HEADS * QK_PAD
    hv = MLA_HEADS * MLA_DV

    def const2(a):
        return pl.BlockSpec(a.shape, lambda i: (0, 0))

    qn = q_norm.reshape(1, -1)
    kvn = kv_norm.reshape(1, -1)
    return pl.pallas_call(
        _mlaproj_kernel,
        grid=(rows // tm,),
        in_specs=[pl.BlockSpec((tm, d), lambda i: (i, 0)),
                  const2(w_in_p), const2(qn), const2(kvn), const2(w_uqt), const2(w_uk),
                  const2(w_uvt),
                  pl.BlockSpec((tm, LANES), lambda i: (i, 0)),
                  pl.BlockSpec((tm, LANES), lambda i: (i, 0)),
                  pl.BlockSpec((tm, LANES), lambda i: (i, 0)),
                  pl.BlockSpec((MLA_ROPE, tm), lambda i: (0, i)),
                  pl.BlockSpec((MLA_ROPE, tm), lambda i: (0, i))],
        out_specs=[pl.BlockSpec((hq, tm), lambda i: (0, i)),
                   pl.BlockSpec((tm, hq), lambda i: (i, 0)),
                   pl.BlockSpec((hv, tm), lambda i: (0, i))],
        out_shape=[jax.ShapeDtypeStruct((hq, rows), MXU_DTYPE),
                   jax.ShapeDtypeStruct((rows, hq), MXU_DTYPE),
                   jax.ShapeDtypeStruct((hv, rows), MXU_DTYPE)],
        compiler_params=_cparams(1),
        name="mla_proj",
    )(h, w_in_p, qn, kvn, w_uqt, w_uk, w_uvt, cos_t, sina_t, sinb_t, cos_rows, sin_rows)


def _flash_kernel(qt_ref, k_ref, vt_ref, o_ref, *, n_lat, n_ctx):
    tq = qt_ref.shape[1]
    qt = qt_ref[...]

    def scores(start, size):
        return _dot(k_ref[start:start + size, :], qt)

    def store(acc, l):
        o_ref[...] = (acc / l).T.astype(o_ref.dtype)

    def attend_online(kv_slices):
        m = jnp.full((1, tq), -jnp.inf, F32)
        l = jnp.zeros((1, tq), F32)
        acc = jnp.zeros((MLA_DV, tq), F32)
        for start, size in kv_slices:
            s = scores(start, size)
            m_new = jnp.maximum(m, jnp.max(s, axis=0, keepdims=True))
            alpha = jnp.exp2(m - m_new)
            p = jnp.exp2(s - m_new)
            l = alpha * l + jnp.sum(p, axis=0, keepdims=True)
            acc = alpha * acc + _dot(vt_ref[:, start:start + size], _mx(p))
            m = m_new
        store(acc, l)

    def attend_fixed_shift(kv_slices):
        m = jnp.max(scores(kv_slices[0][0], SHIFT_KEYS), axis=0, keepdims=True)
        l = jnp.zeros((1, tq), F32)
        acc = jnp.zeros((MLA_DV, tq), F32)
        for start, size in kv_slices:
            p = jnp.exp2(scores(start, size) - m)
            l = l + jnp.sum(p, axis=0, keepdims=True)
            acc = acc + _dot(vt_ref[:, start:start + size], _mx(p))
        store(acc, l)
        n_bad = (jnp.sum(jnp.where(l < FIXED_SHIFT_LIMIT, 0.0, 1.0))
                 + jnp.sum(jnp.where(jnp.abs(acc) < FIXED_SHIFT_LIMIT, 0.0, 1.0)))
        return n_bad == 0.0

    n_kv = n_lat + n_ctx
    i = pl.program_id(1)
    lat_slices = [(s0, KV_TILE) for s0 in range(0, n_kv, KV_TILE)]

    @pl.when(i > 0)
    def _():
        ok = attend_fixed_shift(lat_slices)

        @pl.when(jnp.logical_not(ok))
        def _():
            attend_online(lat_slices)

    @pl.when(i == 0)
    def _():
        attend_online([(n_lat, n_ctx)])


def _mla_attention(qt, k, vt, *, n_lat, n_ctx, n_lat_tiles):
    rows = k.shape[0]
    n_kv = n_lat + n_ctx
    tm = ROW_TILE
    kern = functools.partial(_flash_kernel, n_lat=n_lat, n_ctx=n_ctx)
    n_tiles = rows // tm
    assert n_tiles == n_lat_tiles + 1

    def tile(i):
        return (i + n_lat_tiles) % n_tiles

    return pl.pallas_call(
        kern,
        grid=(MLA_HEADS, n_tiles),
        in_specs=[pl.BlockSpec((QK_PAD, tm), lambda h, i: (h, tile(i))),
                  pl.BlockSpec((n_kv, QK_PAD), lambda h, i: (0, h)),
                  pl.BlockSpec((MLA_DV, n_kv), lambda h, i: (h, 0))],
        out_specs=pl.BlockSpec((tm, MLA_DV), lambda h, i: (tile(i), h)),
        out_shape=jax.ShapeDtypeStruct((rows, MLA_HEADS * MLA_DV), MXU_DTYPE),
        compiler_params=_cparams(2),
        name="mla_flash",
    )(qt, k, vt)


def _prep_even(w_in, wg_f, bg_f, wg_b, bg_b):
    r = GLA_GATE_RANK
    gla_qk, gla_v = 2 * GLA_HEADS * GLA_DK, GLA_HEADS * GLA_DV
    ret_qk, ret_v = 2 * RET_HEADS * RET_DK, RET_HEADS * RET_DV
    widths = [gla_qk, gla_v, gla_v, 2 * r, ret_qk, ret_v, ret_v]
    offs = np.cumsum([0] + widths)
    gqk, gv, gr, gd, rqk, rv, rg = [w_in[:, a:b] for a, b in zip(offs[:-1], offs[1:])]
    d = w_in.shape[0]
    pad = jnp.zeros((d, 2 * LANES - 2 * r), w_in.dtype)
    w = jnp.concatenate([gqk, gv, gr, rqk, rv, rg, gd, pad], axis=1)
    hq = GLA_HEADS * GLA_DK
    wg = jnp.zeros((2, LANES, hq), F32)
    wg = wg.at[0, 0:r].set(wg_f).at[1, r:2 * r].set(wg_b)
    bg = jnp.stack([bg_f, bg_b]).reshape(2, 1, hq)
    return _mx(w), _mx(wg), bg


def _prep_mla(w_in, w_uq, w_ukv):
    d = w_in.shape[0]
    w_in_p = _mx(jnp.pad(w_in, ((0, 0), (0, MLA_IN_PAD - w_in.shape[1]))))
    r = w_uq.shape[0]
    uq = w_uq.reshape(r, MLA_HEADS, MLA_NOPE + MLA_ROPE)
    uq = jnp.pad(uq, ((0, 0), (0, 0), (0, QK_PAD - MLA_NOPE - MLA_ROPE)))
    w_uqt = _mx(uq.reshape(r, MLA_HEADS * QK_PAD).T)
    ukv = w_ukv.reshape(w_ukv.shape[0], MLA_HEADS, MLA_NOPE + MLA_DV)
    w_uk = _mx(ukv[:, :, :MLA_NOPE].reshape(-1, MLA_HEADS * MLA_NOPE))
    w_uvt = _mx(ukv[:, :, MLA_NOPE:].reshape(-1, MLA_HEADS * MLA_DV).T)
    return w_in_p, w_uqt, w_uk, w_uvt


def _rope_angles(pos, dim):
    inv = ROPE_BASE ** (-jnp.arange(0, dim, 2, dtype=F32) / dim)
    ang = pos.astype(F32)[:, None] * inv[None, :]
    return jnp.concatenate([ang, ang], axis=-1)


def _ret_tables(n_lat, rows):
    ang = _rope_angles(jnp.arange(n_lat), RET_DK)
    cos = jnp.ones((rows, RET_DK), F32).at[:n_lat].set(jnp.cos(ang))
    sin = jnp.sin(ang)
    sign = jnp.where(jnp.arange(RET_DK) < RET_DK // 2, -1.0, 1.0)
    sin_s = jnp.zeros((rows, RET_DK), F32).at[:n_lat].set(sin * sign)
    return cos, sin_s


def _axial_tables(n_lat, rows):
    n_rows = n_lat // GRID_W
    row = jnp.repeat(jnp.arange(n_rows), GRID_W)
    col = jnp.tile(jnp.arange(GRID_W), n_rows)
    half = MLA_ROPE // 2
    ang = jnp.concatenate([_rope_angles(row, half), _rope_angles(col, half)], axis=-1)
    lane = jnp.arange(MLA_ROPE)
    lo = (lane % half) < (half // 2)
    cos = jnp.ones((rows, LANES), F32).at[:n_lat, :MLA_ROPE].set(jnp.cos(ang))
    sin = jnp.sin(ang)
    sin_a = jnp.zeros((rows, LANES), F32).at[:n_lat, :MLA_ROPE].set(jnp.where(lo, -sin, 0.0))
    sin_b = jnp.zeros((rows, LANES), F32).at[:n_lat, :MLA_ROPE].set(jnp.where(lo, 0.0, sin))
    cos_rows = jnp.ones((MLA_ROPE, rows), F32).at[:, :n_lat].set(jnp.cos(ang).T)
    sin_rows = jnp.zeros((MLA_ROPE, rows), F32).at[:, :n_lat].set(sin.T)
    return cos, sin_a, sin_b, cos_rows, sin_rows


def kernel(x, c, ctx, c_ctx, w_ada, b_ada, ln_g, ln_b, w_ffn_in, w_ffn_out, ev_w_in, ev_gla_wg_f, ev_gla_bg_f, ev_gla_wg_b, ev_gla_bg_b, ev_gla_norm, ev_w_out, od_w_in, od_q_norm, od_kv_norm, od_w_uq, od_w_ukv, od_w_out):
    n_lat, d = x.shape[1], x.shape[2]
    n_ctx = ctx.shape[1]
    depth = w_ada.shape[0]
    assert x.shape[0] == 1 and n_lat % ROW_TILE == 0 and n_ctx == SCAN_ROWS
    assert (n_lat + n_ctx) % KV_TILE == 0
    alpha = (2 * depth) ** 0.25
    n_lat_tiles = n_lat // ROW_TILE
    n_tiles = n_lat_tiles + 1
    rows = n_tiles * ROW_TILE
    n_seq_blocks = (n_lat + n_ctx) // SCAN_ROWS

    x_all = x[0]
    x_tail = jnp.concatenate([ctx[0], jnp.zeros((ROW_TILE - n_ctx, d), x.dtype)], axis=0)
    c_rows = jnp.zeros((BF16_SUBLANES, d), F32).at[0].set(c[0]).at[1].set(c_ctx)
    mods = _ada(c_rows, w_ada, b_ada)[:, :2].reshape(depth, 2, N_MOD, d)

    ret_cos, ret_sin = _ret_tables(n_lat, rows)
    ax_tables = _axial_tables(n_lat, rows)

    wi, wo = _mx(w_ffn_in), _mx(w_ffn_out)
    for l in range(depth):
        last = l == depth - 1
        x_all, h_mix = _ffn(x_all, mods[l], wi, wo, ln_g[l, 0], ln_b[l, 0], layer=l, half=0,
                            mod_row=0, emit_row=3, alpha=alpha, n_lat_tiles=n_lat_tiles,
                            n_tiles=n_tiles, x_tail=x_tail if l == 0 else None)
        if l % 2 == 0:
            e = l // 2
            w_p, wg, bg = _prep_even(ev_w_in[e], ev_gla_wg_f[e], ev_gla_bg_f[e],
                                     ev_gla_wg_b[e], ev_gla_bg_b[e])
            z = _proj(h_mix, w_p, EV_PROJ_COLS)
            o_f, o_b = _even_scan(z, ret_cos, ret_sin, wg, bg, n_lat // SCAN_ROWS, n_seq_blocks)
            x_all = _even_out(o_f, o_b, z, ev_gla_norm[e], _mx(ev_w_out[e]), x_all, mods[l],
                              ln_g[l, 1], ln_b[l, 1], gate_row=5, alpha=alpha,
                              n_lat_tiles=n_lat_tiles)
        else:
            o = l // 2
            w_in_p, w_uqt, w_uk, w_uvt = _prep_mla(od_w_in[o], od_w_uq[o], od_w_ukv[o])
            qt, k, vt = _mla_proj(h_mix, w_in_p, od_q_norm[o], od_kv_norm[o], w_uqt, w_uk,
                                  w_uvt, ax_tables)
            a = _mla_attention(qt, k, vt, n_lat=n_lat, n_ctx=n_ctx, n_lat_tiles=n_lat_tiles)
            x_all = _outproj_ln(a, _mx(od_w_out[o]), x_all, mods[l], ln_g[l, 1], ln_b[l, 1],
                                gate_row=5, alpha=alpha, n_lat_tiles=n_lat_tiles)
        x_all, _ = _ffn(x_all, mods[l], wi, wo, ln_g[l, 2], ln_b[l, 2], layer=l, half=1,
                        mod_row=6, emit_row=None, alpha=alpha, n_lat_tiles=n_lat_tiles,
                        n_tiles=n_lat_tiles if last else n_tiles)
    return x_all[None]
```

```python
import functools
import math

import numpy as np
import jax
import jax.numpy as jnp
from jax import lax
from jax.experimental import pallas as pl
from jax.experimental.pallas import tpu as pltpu

F32 = jnp.float32
MXU_DTYPE = jnp.bfloat16

N_MOD = 9
LN_EPS = 1e-5
RMS_EPS = 1e-6
CHUNK = 64
ROPE_BASE = 10000.0
GRID_W = 64
GLA_HEADS = 4
GLA_DK = 128
GLA_DV = 256
GLA_GATE_RANK = 16
GLA_GATE_NORM = 16.0
RET_HEADS = 4
RET_DK = 128
RET_DV = 256
MLA_HEADS = 16
MLA_Q_RANK = 512
MLA_KV_RANK = 512
MLA_NOPE = 128
MLA_ROPE = 64
MLA_DV = 128
MLA_SCALE = (MLA_NOPE + MLA_ROPE) ** -0.5
LOG2E = 1.4426950408889634
FIXED_SHIFT_LIMIT = 2.0 ** 100

LANES = 128
BF16_SUBLANES = 16
ROW_TILE = 512
FF_CHUNK = 1024
LN_SLICE = 128
SCAN_ROWS = 256
MLA_PROJ_ROWS = 256
KV_TILE = 1280
SHIFT_KEYS = 256
QK_PAD = 256
ADA_COLS = 2048
VMEM_LIMIT = 56 * 1024 * 1024
FFN_VMEM_LIMIT = 60 * 1024 * 1024

EV_BLOCK = 1024
EV_GLA_QK, EV_GLA_V, EV_GLA_GATE, EV_RET_QK, EV_RET_V, EV_RET_GATE = range(6)
EV_DECAY_LANE_BLOCK = 6 * EV_BLOCK // LANES
EV_COLS = 6 * EV_BLOCK + 2 * LANES
EV_PROJ_COLS = EV_COLS // 2
MLA_IN_PAD = MLA_Q_RANK + MLA_KV_RANK + LANES


def _cparams(n_axes, vmem=VMEM_LIMIT):
    return pltpu.CompilerParams(
        dimension_semantics=("arbitrary",) * n_axes, vmem_limit_bytes=vmem)


def _mx(a):
    return a.astype(MXU_DTYPE)


def _dot(a, b):
    return jnp.dot(a, b, preferred_element_type=F32)


def _dot_nt(a, b):
    return lax.dot_general(a, b, (((1,), (1,)), ((), ())), preferred_element_type=F32)


def _dot_tn(a, b):
    return lax.dot_general(a, b, (((0,), (0,)), ((), ())), preferred_element_type=F32)


def _silu(v):
    return v / (1.0 + jnp.exp(-v))


def _layer_norm(v, g, b):
    mu = jnp.mean(v, axis=-1, keepdims=True)
    d = v - mu
    var = jnp.mean(d * d, axis=-1, keepdims=True)
    return d * lax.rsqrt(var + LN_EPS) * g + b


def _ada_kernel(c_ref, w_ref, b_ref, o_ref):
    s = _silu(c_ref[...])
    o_ref[...] = _dot(_mx(s), _mx(w_ref[...])) + b_ref[...]


def _ada(c_rows, w_ada, b_ada):
    depth, d, n = w_ada.shape
    rows = c_rows.shape[0]
    tn = ADA_COLS
    return pl.pallas_call(
        _ada_kernel,
        grid=(depth, n // tn),
        in_specs=[
            pl.BlockSpec((rows, d), lambda l, j: (0, 0)),
            pl.BlockSpec((None, d, tn), lambda l, j: (l, 0, j)),
            pl.BlockSpec((None, 1, tn), lambda l, j: (l, 0, j)),
        ],
        out_specs=pl.BlockSpec((None, rows, tn), lambda l, j: (l, 0, j)),
        out_shape=jax.ShapeDtypeStruct((depth, rows, n), F32),
        compiler_params=_cparams(2),
        name="ada_mod",
    )(c_rows, w_ada, b_ada.reshape(depth, 1, n))


def _ffn_kernel(*refs, mod_row, emit_row, alpha, n_tiles, tail_tile):
    if tail_tile is None:
        x_ref, xt_ref = refs[0], None
        refs = refs[1:]
    else:
        x_ref, xt_ref = refs[:2]
        refs = refs[2:]
    mod_ref, wg_ref, wu_ref, wout_ref, wg_t, wu_t, wout_t, g_ref, b_ref = refs[:9]
    rest = refs[9:]
    if emit_row is None:
        o_ref, xm_scr, acc_scr, v_scr, em_scr = rest
        h_ref = None
    else:
        o_ref, h_ref, xm_scr, acc_scr, v_scr, em_scr = rest
    i, k = pl.program_id(0), pl.program_id(1)
    last_k = pl.num_programs(1) - 1
    n_slices = v_scr.shape[0] // LN_SLICE

    def read_x():
        if xt_ref is None:
            return x_ref[...]
        return jnp.where(i == tail_tile, xt_ref[...], x_ref[...])

    def finish_rows():
        r0 = pl.multiple_of(jnp.minimum(k, n_slices - 1) * LN_SLICE, LN_SLICE)
        y = _layer_norm(v_scr[pl.ds(r0, LN_SLICE), :], g_ref[...], b_ref[...])
        o_ref[...] = y
        if h_ref is not None:
            h_ref[...] = _mx(y * (1.0 + em_scr[1:2, :]) + em_scr[0:1, :])

    def accumulate(wg, wu, wout, first=False):
        xm = xm_scr[...]
        act = _silu(_dot(xm, wg[...])) * _dot(xm, wu[...])
        y = _dot(_mx(act), wout[...])
        if first:
            acc_scr[...] = y
        else:
            acc_scr[...] += y

    @pl.when((i == 0) & (k == 0))
    def _():
        v_scr[...] = jnp.zeros_like(v_scr)
        em_scr[...] = jnp.zeros_like(em_scr)

    @pl.when((i < n_tiles) & (k == 0))
    def _():
        shift = mod_ref[mod_row:mod_row + 1, :]
        scale = mod_ref[mod_row + 1:mod_row + 2, :]
        xm_scr[...] = _mx(read_x() * (1.0 + scale) + shift)
        accumulate(wg_ref, wu_ref, wout_ref, first=True)
        finish_rows()

    @pl.when((i < n_tiles) & (k > 0) & (k < last_k))
    def _():
        accumulate(wg_ref, wu_ref, wout_ref)
        finish_rows()

    @pl.when((i < n_tiles) & (k == last_k))
    def _():
        accumulate(wg_t, wu_t, wout_t)
        finish_rows()
        gate = mod_ref[mod_row + 2:mod_row + 3, :]
        v_scr[...] = alpha * read_x() + 0.5 * gate * acc_scr[...]
        if h_ref is not None:
            em_scr[...] = mod_ref[emit_row:emit_row + 2, :]

    @pl.when(i == n_tiles)
    def _():
        finish_rows()


def _ffn_whole_windows(w_in):
    depth, halves, d, two_f = w_in.shape
    f = two_f // 2
    n = f // FF_CHUNK
    parts = [w_in[..., g * f:g * f + n * FF_CHUNK].reshape(depth, halves, d, n, FF_CHUNK)
             for g in (0, 1)]
    return jnp.stack(parts, axis=2).transpose(0, 1, 2, 4, 3, 5)


def _ffn(x_all, mods, w_win, w_in, w_out, ln_g, ln_b, *, layer, half, mod_row, emit_row, alpha,
         n_lat_tiles, n_tiles, x_tail=None):
    d = x_all.shape[1]
    f = w_out.shape[2]
    tm, tk = ROW_TILE, FF_CHUNK
    n_whole = f // tk
    tail = f - n_whole * tk
    n_chunks = n_whole + 1
    rows = n_tiles * tm

    assert f % LANES == 0 and tk % LANES == 0 and tail > 0 and n_chunks >= tm // LN_SLICE

    def widx(i, k):
        return jnp.where(i < n_tiles, jnp.minimum(k, n_whole - 1), n_whole - 1)

    def win(i, k):
        return widx(i, k) * (tk // LANES) * LANES

    def resident(shape, *offsets):
        return pl.BlockSpec((None, None) + tuple(pl.Element(s) for s in shape),
                            lambda i, k: (layer, half) + offsets, pipeline_mode=pl.Buffered(1))

    def src(i):
        return jnp.minimum(i, n_tiles - 1)

    n_slices = tm // LN_SLICE

    def dst(i, k):
        return jnp.where(i == 0, 0, (i - 1) * n_slices + jnp.minimum(k, n_slices - 1))

    out_shape = [jax.ShapeDtypeStruct((rows, d), F32)]
    out_specs = [pl.BlockSpec((LN_SLICE, d), lambda i, k: (dst(i, k), 0))]
    if emit_row is not None:
        out_shape.append(jax.ShapeDtypeStruct((rows, d), MXU_DTYPE))
        out_specs.append(pl.BlockSpec((LN_SLICE, d), lambda i, k: (dst(i, k), 0)))
    tail_tile = None if x_tail is None else n_tiles - 1
    kern = functools.partial(_ffn_kernel, mod_row=mod_row, emit_row=emit_row, alpha=alpha,
                             n_tiles=n_tiles, tail_tile=tail_tile)
    if x_tail is None:
        x_args = [x_all]
        x_specs = [pl.BlockSpec((tm, d), lambda i, k: (src(i), 0))]
    else:
        x_args = [x_all, x_tail]
        x_specs = [pl.BlockSpec((tm, d), lambda i, k: (jnp.minimum(i, tail_tile - 1), 0)),
                   pl.BlockSpec((tm, d), lambda i, k: (0, 0))]
    res = pl.pallas_call(
        kern,
        grid=(n_tiles + 1, n_chunks),
        in_specs=x_specs + [
            pl.BlockSpec((None, N_MOD, d), lambda i, k: (src(i) // n_lat_tiles, 0, 0)),
            pl.BlockSpec((None, None, None, None, d, tk),
                         lambda i, k: (layer, half, 0, widx(i, k), 0, 0)),
            pl.BlockSpec((None, None, None, None, d, tk),
                         lambda i, k: (layer, half, 1, widx(i, k), 0, 0)),
            pl.BlockSpec((None, None, pl.Element(tk), pl.Element(d)),
                         lambda i, k: (layer, half, win(i, k), 0)),
            resident((d, tail), 0, n_whole * tk),
            resident((d, tail), 0, f + n_whole * tk),
            resident((tail, d), n_whole * tk, 0),
            pl.BlockSpec((1, d), lambda i, k: (0, 0)),
            pl.BlockSpec((1, d), lambda i, k: (0, 0)),
        ],
        out_specs=out_specs,
        out_shape=out_shape,
        scratch_shapes=[pltpu.VMEM((tm, d), MXU_DTYPE), pltpu.VMEM((tm, d), F32),
                        pltpu.VMEM((tm, d), F32), pltpu.VMEM((2, d), F32)],
        compiler_params=_cparams(2, vmem=FFN_VMEM_LIMIT),
        name="ffn",
    )(*x_args, mods, w_win, w_win, w_out, w_in, w_in, w_out,
      ln_g.reshape(1, d), ln_b.reshape(1, d))
    return res if emit_row is not None else (res[0], None)


def _proj_kernel(h_ref, w_ref, o_ref):
    o_ref[...] = _dot(h_ref[...], w_ref[...])


def _proj(h, w, tn):
    rows, kdim = h.shape
    n = w.shape[1]
    tm = ROW_TILE
    return pl.pallas_call(
        _proj_kernel,
        grid=(n // tn, rows // tm),
        in_specs=[
            pl.BlockSpec((tm, kdim), lambda j, i: (i, 0)),
            pl.BlockSpec((kdim, tn), lambda j, i: (0, j)),
        ],
        out_specs=pl.BlockSpec((tm, tn), lambda j, i: (i, j)),
        out_shape=jax.ShapeDtypeStruct((rows, n), F32),
        compiler_params=_cparams(2),
        name="even_proj",
    )(h, w)


def _outproj_kernel(a_ref, w_ref, x_ref, mod_ref, g_ref, b_ref, o_ref, *, gate_row, alpha):
    y = _dot(a_ref[...], w_ref[...])
    gate = mod_ref[gate_row:gate_row + 1, :]
    v = alpha * x_ref[...] + gate * y
    o_ref[...] = _layer_norm(v, g_ref[...], b_ref[...])


def _outproj_ln(a, w, x_all, mods, ln_g, ln_b, *, gate_row, alpha, n_lat_tiles):
    rows, d = x_all.shape
    kdim = a.shape[1]
    tm = ROW_TILE
    kern = functools.partial(_outproj_kernel, gate_row=gate_row, alpha=alpha)
    return pl.pallas_call(
        kern,
        grid=(rows // tm,),
        in_specs=[
            pl.BlockSpec((tm, kdim), lambda i: (i, 0)),
            pl.BlockSpec((kdim, d), lambda i: (0, 0)),
            pl.BlockSpec((tm, d), lambda i: (i, 0)),
            pl.BlockSpec((None, N_MOD, d), lambda i: (i // n_lat_tiles, 0, 0)),
            pl.BlockSpec((1, d), lambda i: (0, 0)),
            pl.BlockSpec((1, d), lambda i: (0, 0)),
        ],
        out_specs=pl.BlockSpec((tm, d), lambda i: (i, 0)),
        out_shape=jax.ShapeDtypeStruct((rows, d), F32),
        compiler_params=_cparams(1),
        name="mix_out",
    )(a, w, x_all, mods, ln_g.reshape(1, d), ln_b.reshape(1, d))


_GLA_LEVELS = (64, 32, 16, 8, 4, 2)


def _scan_positions(reverse):
    idx = np.arange(CHUNK)
    return idx[::-1].copy() if reverse else idx


def _gla_exponent_matrix(reverse):
    pos = _scan_positions(reverse)
    pi, pt = pos[:, None], pos[None, :]
    mats = [pt <= pi,
            pt > pi]
    for n in _GLA_LEVELS:
        half = n // 2
        ref = (pi // n) * n + half - 1
        q_side = (pi % n) >= half
        mats.append(np.where(q_side, (pt > ref) & (pt <= pi), (pt > pi) & (pt <= ref)))
    return np.concatenate(mats, axis=0).astype(np.float32)


def _ret_constants(reverse):
    scales = np.arange(RET_HEADS, dtype=np.float64)
    off = 5.5 if reverse else 5.0
    log_gamma = np.log1p(-np.exp2(-off - scales))
    idx = np.arange(CHUNK, dtype=np.float64)
    diff = idx[:, None] - idx[None, :]
    dmat = np.where(diff >= 0, np.exp(log_gamma[:, None, None] * np.maximum(diff, 0.0)), 0.0)
    q_dec = np.exp(log_gamma[:, None] * (idx + 1.0))
    k_dec = np.exp(log_gamma[:, None] * (CHUNK - 1.0 - idx))
    s_dec = np.exp(log_gamma * CHUNK)
    if reverse:
        dmat = dmat[:, ::-1, ::-1]
        q_dec = q_dec[:, ::-1]
        k_dec = k_dec[:, ::-1]
    expand = lambda a: np.repeat(a.T, RET_DK, axis=1)
    return (dmat.astype(np.float32), expand(q_dec).astype(np.float32),
            expand(k_dec).astype(np.float32),
            np.repeat(s_dec, RET_DK)[None, :].astype(np.float32))


def _log_sigmoid(v):
    return jnp.minimum(v, 0.0) - jnp.log1p(jnp.exp(-jnp.abs(v)))


def _gla_masks(reverse):
    c = CHUNK
    row = lax.broadcasted_iota(jnp.int32, (c, c), 0)
    col = lax.broadcasted_iota(jnp.int32, (c, c), 1)
    pr, pc = ((c - 1) - row, (c - 1) - col) if reverse else (row, col)
    levels = []
    for n in _GLA_LEVELS:
        sh, half = int(math.log2(n)), n // 2
        levels.append((lax.shift_right_logical(pr, sh) == lax.shift_right_logical(pc, sh))
                      & ((pr & (n - 1)) >= half) & ((pc & (n - 1)) < half))
    return levels, row == col


def _rope128(xh, cos, sin_signed):
    return xh * cos + pltpu.roll(xh, 64, 1) * sin_signed


def _scan_chunk_step(streams, consts, out_refs, sg_scr, sr_scr, lgs, masks, cc):
    wg_ref, bg_ref, mexp_ref, dmat_ref, qdec_ref, kdec_ref, sdec_ref = consts
    c = CHUNK
    n_chunks = SCAN_ROWS // CHUNK
    hq = GLA_HEADS * GLA_DK
    hr = RET_HEADS * RET_DK
    o_off = GLA_HEADS * GLA_DV
    tasks = []
    for d in (0, 1):
        zqk, zv, zrqk, zrv, _, cos_ref, sin_ref = streams[d]
        reverse = d == 1
        r0 = ((n_chunks - 1 - cc) if reverse else cc) * c
        rows = slice(r0, r0 + c)
        g = lgs[d][rows]
        g_hi = _mx(g)
        g_lo = _mx(g - g_hi.astype(F32))
        mexp = mexp_ref[d]
        w_all = jnp.exp(_dot(mexp, g_hi) + _dot(mexp, g_lo))
        ones = jnp.ones((c, LANES), MXU_DTYPE)
        decay_rows = jnp.exp(_dot_tn(g_hi, ones) + _dot_tn(g_lo, ones))
        cs, sn = cos_ref[rows, :], sin_ref[rows, :]
        for h in range(GLA_HEADS):
            lanes = slice(h * GLA_DK, (h + 1) * GLA_DK)
            tasks.append(dict(
                kind="gla", d=d, rows=rows, w=w_all[:, lanes], decay=decay_rows[lanes],
                q=zqk[rows, lanes] * (GLA_DK ** -0.5),
                k=zqk[rows, hq + h * GLA_DK:hq + (h + 1) * GLA_DK],
                v=zv[rows, h * GLA_DV:(h + 1) * GLA_DV],
                st=sg_scr.at[d, h], cols=slice(h * GLA_DV, (h + 1) * GLA_DV)))
        for h in range(RET_HEADS):
            lanes = slice(h * RET_DK, (h + 1) * RET_DK)
            tasks.append(dict(
                kind="ret", d=d, rows=rows,
                q=_rope128(zrqk[rows, lanes], cs, sn),
                k=_rope128(zrqk[rows, hr + h * RET_DK:hr + (h + 1) * RET_DK]
                           * (RET_DK ** -0.5), cs, sn),
                v=zrv[rows, h * RET_DV:(h + 1) * RET_DV],
                dmat=dmat_ref[d, h], q_dec=qdec_ref[d][:, lanes], k_dec=kdec_ref[d][:, lanes],
                s_dec=sdec_ref[d][:, lanes], st=sr_scr.at[d, h],
                cols=slice(o_off + h * RET_DV, o_off + (h + 1) * RET_DV)))
    for t in tasks:
        t["s_t"] = t["st"][...]
        if t["kind"] == "gla":
            w, q, k = t["w"], t["q"], t["k"]
            t["inter"] = _dot(_mx(q * w[0:c]), _mx(t["s_t"]))
            t["lvl"] = [_dot_nt(_mx(q * w[(2 + li) * c:(3 + li) * c]),
                                _mx(k * w[(2 + li) * c:(3 + li) * c]))
                        for li in range(len(_GLA_LEVELS))]
        else:
            t["inter"] = _dot(_mx(t["q"] * t["q_dec"]), _mx(t["s_t"]))
            t["qk"] = _dot_nt(_mx(t["q"]), _mx(t["k"]))
    for t in tasks:
        if t["kind"] == "gla":
            level_masks, diag_mask = masks[t["d"]]
            a = jnp.where(diag_mask, jnp.sum(t["q"] * t["k"], axis=-1, keepdims=True), 0.0)
            for li in range(len(_GLA_LEVELS)):
                a = a + jnp.where(level_masks[li], t["lvl"][li], 0.0)
        else:
            a = t["qk"] * t["dmat"]
        t["a"] = _mx(a)
    for t in tasks:
        v = _mx(t["v"])
        out_refs[t["d"]][t["rows"], t["cols"]] = t["inter"] + _dot(t["a"], v)
        if t["kind"] == "gla":
            decay = jnp.concatenate([t["decay"]] * (GLA_DV // LANES), axis=1)
            k_dec = t["k"] * t["w"][c:2 * c]
        else:
            decay = t["s_dec"][:, 0:1]
            k_dec = t["k"] * t["k_dec"]
        t["st"][...] = decay * t["s_t"] + _dot_tn(_mx(k_dec), v)


def _scan_kernel(*refs):
    streams, consts = (refs[0:7], refs[7:14]), refs[14:21]
    of_ref, ob_ref, sg_scr, sr_scr = refs[21:]
    wg_ref, bg_ref = consts[0], consts[1]

    @pl.when(pl.program_id(0) == 0)
    def _():
        sg_scr[...] = jnp.zeros_like(sg_scr)
        sr_scr[...] = jnp.zeros_like(sr_scr)

    lgs = [_log_sigmoid(_dot(_mx(streams[d][4][...]), wg_ref[d]) + bg_ref[d])
           * (1.0 / GLA_GATE_NORM) for d in (0, 1)]
    masks = [_gla_masks(False), _gla_masks(True)]
    for cc in range(SCAN_ROWS // CHUNK):
        _scan_chunk_step(streams, consts, (of_ref, ob_ref), sg_scr, sr_scr, lgs, masks, cc)


def _even_scan(z, cos_t, sin_t, wg, bg, n_lat_blocks, n_seq_blocks):
    rows = z.shape[0]
    n_blocks = rows // SCAN_ROWS
    r = SCAN_ROWS

    def rf(i):
        return jnp.where(i < n_seq_blocks, (i + n_lat_blocks) % n_seq_blocks, i)

    def rb(i):
        return jnp.where(i < n_seq_blocks, n_seq_blocks - 1 - i, i)

    def stream_specs(rmap):
        return [
            pl.BlockSpec((r, EV_BLOCK), lambda i: (rmap(i), EV_GLA_QK)),
            pl.BlockSpec((r, EV_BLOCK), lambda i: (rmap(i), EV_GLA_V)),
            pl.BlockSpec((r, EV_BLOCK), lambda i: (rmap(i), EV_RET_QK)),
            pl.BlockSpec((r, EV_BLOCK), lambda i: (rmap(i), EV_RET_V)),
            pl.BlockSpec((r, LANES), lambda i: (rmap(i), EV_DECAY_LANE_BLOCK)),
            pl.BlockSpec((r, LANES), lambda i: (rmap(i), 0)),
            pl.BlockSpec((r, LANES), lambda i: (rmap(i), 0)),
        ]

    mexp = jnp.asarray(np.stack([_gla_exponent_matrix(False), _gla_exponent_matrix(True)]),
                       MXU_DTYPE)
    rc = [_ret_constants(False), _ret_constants(True)]
    dmat = jnp.asarray(np.stack([rc[0][0], rc[1][0]]))
    qdec = jnp.asarray(np.stack([rc[0][1], rc[1][1]]))
    kdec = jnp.asarray(np.stack([rc[0][2], rc[1][2]]))
    sdec = jnp.asarray(np.stack([rc[0][3], rc[1][3]]))
    consts = [wg, bg, mexp, dmat, qdec, kdec, sdec]

    def full_spec(a):
        nd = a.ndim
        return pl.BlockSpec(a.shape, lambda i: (0,) * nd)

    d_out = GLA_HEADS * GLA_DV + RET_HEADS * RET_DV
    return pl.pallas_call(
        _scan_kernel,
        grid=(n_blocks,),
        in_specs=stream_specs(rf) + stream_specs(rb) + [full_spec(a) for a in consts],
        out_specs=[pl.BlockSpec((r, d_out), lambda i: (rf(i), 0)),
                   pl.BlockSpec((r, d_out), lambda i: (rb(i), 0))],
        out_shape=[jax.ShapeDtypeStruct((rows, d_out), F32)] * 2,
        scratch_shapes=[pltpu.VMEM((2, GLA_HEADS, GLA_DK, GLA_DV), F32),
                        pltpu.VMEM((2, RET_HEADS, RET_DK, RET_DV), F32)],
        compiler_params=_cparams(1),
        name="even_scan",
    )(z, z, z, z, z, cos_t, sin_t, z, z, z, z, z, cos_t, sin_t, *consts)


def _evout_kernel(of_ref, ob_ref, gr_ref, rg_ref, nw_ref, w_ref, x_ref, mod_ref, g_ref, b_ref,
                  o_ref, *, gate_row, alpha):
    parts = []
    for h in range(GLA_HEADS + RET_HEADS):
        lanes = slice(h * GLA_DV, (h + 1) * GLA_DV)
        oh = of_ref[:, lanes] + ob_ref[:, lanes]
        y = oh * lax.rsqrt(jnp.mean(oh * oh, axis=-1, keepdims=True) + RMS_EPS)
        if h < GLA_HEADS:
            y = y * nw_ref[...]
            gate = gr_ref[:, h * GLA_DV:(h + 1) * GLA_DV]
        else:
            hh = h - GLA_HEADS
            gate = rg_ref[:, hh * RET_DV:(hh + 1) * RET_DV]
        parts.append(_mx(y * _silu(gate)))
    y = _dot(jnp.concatenate(parts, axis=1), w_ref[...])
    v = alpha * x_ref[...] + mod_ref[gate_row:gate_row + 1, :] * y
    o_ref[...] = _layer_norm(v, g_ref[...], b_ref[...])


def _even_out(o_f, o_b, z, norm_w, w, x_all, mods, ln_g, ln_b, *, gate_row, alpha, n_lat_tiles):
    rows, d = x_all.shape
    d_out = o_f.shape[1]
    tm = ROW_TILE
    kern = functools.partial(_evout_kernel, gate_row=gate_row, alpha=alpha)
    return pl.pallas_call(
        kern,
        grid=(rows // tm,),
        in_specs=[
            pl.BlockSpec((tm, d_out), lambda i: (i, 0)),
            pl.BlockSpec((tm, d_out), lambda i: (i, 0)),
            pl.BlockSpec((tm, EV_BLOCK), lambda i: (i, EV_GLA_GATE)),
            pl.BlockSpec((tm, EV_BLOCK), lambda i: (i, EV_RET_GATE)),
            pl.BlockSpec((1, GLA_DV), lambda i: (0, 0)),
            pl.BlockSpec((d_out, d), lambda i: (0, 0), pipeline_mode=pl.Buffered(1)),
            pl.BlockSpec((tm, d), lambda i: (i, 0)),
            pl.BlockSpec((None, N_MOD, d), lambda i: (i // n_lat_tiles, 0, 0)),
            pl.BlockSpec((1, d), lambda i: (0, 0)),
            pl.BlockSpec((1, d), lambda i: (0, 0)),
        ],
        out_specs=pl.BlockSpec((tm, d), lambda i: (i, 0)),
        out_shape=jax.ShapeDtypeStruct((rows, d), F32),
        compiler_params=_cparams(1),
        name="even_out",
    )(o_f, o_b, z, z, norm_w.reshape(1, GLA_DV), w, x_all, mods,
      ln_g.reshape(1, d), ln_b.reshape(1, d))


def _axial128(xh, cos, sin_a, sin_b):
    return xh * cos + pltpu.roll(xh, LANES - 16, 1) * sin_a + pltpu.roll(xh, 16, 1) * sin_b


def _axial_rows(xt, cos_t, sin_t):
    rot = jnp.concatenate([-xt[16:32], xt[0:16], -xt[48:64], xt[32:48]], axis=0)
    return xt * cos_t + rot * sin_t


def _mlaproj_kernel(h_ref, win_ref, qn_ref, kvn_ref, wuqt_ref, wuk_ref, wuvt_ref,
                    cos_ref, sina_ref, sinb_ref, cost_ref, sint_ref, qt_ref, k_ref, vt_ref):
    z = _dot(h_ref[...], win_ref[...])
    cq = z[:, :MLA_Q_RANK]
    ckv = z[:, MLA_Q_RANK:MLA_Q_RANK + MLA_KV_RANK]
    kr = z[:, MLA_Q_RANK + MLA_KV_RANK:]
    cos, sin_a, sin_b = cos_ref[...], sina_ref[...], sinb_ref[...]
    cqn = _mx(cq * lax.rsqrt(jnp.mean(cq * cq, axis=-1, keepdims=True) + RMS_EPS) * qn_ref[...])
    ckvn = _mx(ckv * lax.rsqrt(jnp.mean(ckv * ckv, axis=-1, keepdims=True) + RMS_EPS)
               * kvn_ref[...])
    qt = _dot_nt(wuqt_ref[...], cqn)
    kn = _dot(ckvn, wuk_ref[...])
    vt_ref[...] = _mx(_dot_nt(wuvt_ref[...], ckvn))
    kr_rot = _mx(_axial128(kr, cos, sin_a, sin_b))
    cos_t, sin_t = cost_ref[...], sint_ref[...]
    qs = MLA_SCALE * LOG2E
    r0, r1 = MLA_NOPE, MLA_NOPE + MLA_ROPE
    for h in range(MLA_HEADS):
        b0 = h * QK_PAD
        qt_ref[b0:b0 + r0, :] = _mx(qt[b0:b0 + r0] * qs)
        qt_ref[b0 + r0:b0 + r1, :] = _mx(_axial_rows(qt[b0 + r0:b0 + r1], cos_t, sin_t) * qs)
        qt_ref[b0 + r1:b0 + QK_PAD, :] = _mx(qt[b0 + r1:b0 + QK_PAD])
        k_ref[:, b0:b0 + LANES] = _mx(kn[:, h * MLA_NOPE:(h + 1) * MLA_NOPE])
        k_ref[:, b0 + LANES:b0 + QK_PAD] = kr_rot


def _mla_proj(h, w_in_p, q_norm, kv_norm, w_uqt, w_uk, w_uvt, tables):
    cos_t, sina_t, sinb_t, cos_rows, sin_rows = tables
    rows, d = h.shape
    tm = MLA_PROJ_ROWS
    hq = MLA_HEADS * QK_PAD
    hv = MLA_HEADS * MLA_DV

    def const2(a):
        return pl.BlockSpec(a.shape, lambda i: (0, 0))

    qn = q_norm.reshape(1, -1)
    kvn = kv_norm.reshape(1, -1)
    return pl.pallas_call(
        _mlaproj_kernel,
        grid=(rows // tm,),
        in_specs=[pl.BlockSpec((tm, d), lambda i: (i, 0)),
                  const2(w_in_p), const2(qn), const2(kvn), const2(w_uqt), const2(w_uk),
                  const2(w_uvt),
                  pl.BlockSpec((tm, LANES), lambda i: (i, 0)),
                  pl.BlockSpec((tm, LANES), lambda i: (i, 0)),
                  pl.BlockSpec((tm, LANES), lambda i: (i, 0)),
                  pl.BlockSpec((MLA_ROPE, tm), lambda i: (0, i)),
                  pl.BlockSpec((MLA_ROPE, tm), lambda i: (0, i))],
        out_specs=[pl.BlockSpec((hq, tm), lambda i: (0, i)),
                   pl.BlockSpec((tm, hq), lambda i: (i, 0)),
                   pl.BlockSpec((hv, tm), lambda i: (0, i))],
        out_shape=[jax.ShapeDtypeStruct((hq, rows), MXU_DTYPE),
                   jax.ShapeDtypeStruct((rows, hq), MXU_DTYPE),
                   jax.ShapeDtypeStruct((hv, rows), MXU_DTYPE)],
        compiler_params=_cparams(1),
        name="mla_proj",
    )(h, w_in_p, qn, kvn, w_uqt, w_uk, w_uvt, cos_t, sina_t, sinb_t, cos_rows, sin_rows)


def _flash_kernel(qt_ref, k_ref, vt_ref, o_ref, *, n_lat, n_ctx):
    tq = qt_ref.shape[1]
    qt = qt_ref[...]

    def scores(start, size):
        return _dot(k_ref[start:start + size, :], qt)

    def store(acc, l):
        o_ref[...] = (acc / l).T.astype(o_ref.dtype)

    def attend_online(kv_slices):
        m = jnp.full((1, tq), -jnp.inf, F32)
        l = jnp.zeros((1, tq), F32)
        acc = jnp.zeros((MLA_DV, tq), F32)
        for start, size in kv_slices:
            s = scores(start, size)
            m_new = jnp.maximum(m, jnp.max(s, axis=0, keepdims=True))
            alpha = jnp.exp2(m - m_new)
            p = jnp.exp2(s - m_new)
            l = alpha * l + jnp.sum(p, axis=0, keepdims=True)
            acc = alpha * acc + _dot(vt_ref[:, start:start + size], _mx(p))
            m = m_new
        store(acc, l)

    def attend_fixed_shift(kv_slices):
        m = jnp.max(scores(kv_slices[0][0], SHIFT_KEYS), axis=0, keepdims=True)
        l = jnp.zeros((1, tq), F32)
        acc = jnp.zeros((MLA_DV, tq), F32)
        for start, size in kv_slices:
            p = jnp.exp2(scores(start, size) - m)
            l = l + jnp.sum(p, axis=0, keepdims=True)
            acc = acc + _dot(vt_ref[:, start:start + size], _mx(p))
        store(acc, l)
        n_bad = (jnp.sum(jnp.where(l < FIXED_SHIFT_LIMIT, 0.0, 1.0))
                 + jnp.sum(jnp.where(jnp.abs(acc) < FIXED_SHIFT_LIMIT, 0.0, 1.0)))
        return n_bad == 0.0

    n_kv = n_lat + n_ctx
    i = pl.program_id(1)
    lat_slices = [(s0, KV_TILE) for s0 in range(0, n_kv, KV_TILE)]

    @pl.when(i > 0)
    def _():
        ok = attend_fixed_shift(lat_slices)

        @pl.when(jnp.logical_not(ok))
        def _():
            attend_online(lat_slices)

    @pl.when(i == 0)
    def _():
        attend_online([(n_lat, n_ctx)])


def _mla_attention(qt, k, vt, *, n_lat, n_ctx, n_lat_tiles):
    rows = k.shape[0]
    n_kv = n_lat + n_ctx
    tm = ROW_TILE
    kern = functools.partial(_flash_kernel, n_lat=n_lat, n_ctx=n_ctx)
    n_tiles = rows // tm
    assert n_tiles == n_lat_tiles + 1

    def tile(i):
        return (i + n_lat_tiles) % n_tiles

    return pl.pallas_call(
        kern,
        grid=(MLA_HEADS, n_tiles),
        in_specs=[pl.BlockSpec((QK_PAD, tm), lambda h, i: (h, tile(i))),
                  pl.BlockSpec((n_kv, QK_PAD), lambda h, i: (0, h)),
                  pl.BlockSpec((MLA_DV, n_kv), lambda h, i: (h, 0))],
        out_specs=pl.BlockSpec((tm, MLA_DV), lambda h, i: (tile(i), h)),
        out_shape=jax.ShapeDtypeStruct((rows, MLA_HEADS * MLA_DV), MXU_DTYPE),
        compiler_params=_cparams(2),
        name="mla_flash",
    )(qt, k, vt)


def _prep_even(w_in, wg_f, bg_f, wg_b, bg_b):
    r = GLA_GATE_RANK
    gla_qk, gla_v = 2 * GLA_HEADS * GLA_DK, GLA_HEADS * GLA_DV
    ret_qk, ret_v = 2 * RET_HEADS * RET_DK, RET_HEADS * RET_DV
    widths = [gla_qk, gla_v, gla_v, 2 * r, ret_qk, ret_v, ret_v]
    offs = np.cumsum([0] + widths)
    gqk, gv, gr, gd, rqk, rv, rg = [w_in[:, a:b] for a, b in zip(offs[:-1], offs[1:])]
    d = w_in.shape[0]
    pad = jnp.zeros((d, 2 * LANES - 2 * r), w_in.dtype)
    w = jnp.concatenate([gqk, gv, gr, rqk, rv, rg, gd, pad], axis=1)
    hq = GLA_HEADS * GLA_DK
    wg = jnp.zeros((2, LANES, hq), F32)
    wg = wg.at[0, 0:r].set(wg_f).at[1, r:2 * r].set(wg_b)
    bg = jnp.stack([bg_f, bg_b]).reshape(2, 1, hq)
    return _mx(w), _mx(wg), bg


def _prep_mla(w_in, w_uq, w_ukv):
    d = w_in.shape[0]
    w_in_p = _mx(jnp.pad(w_in, ((0, 0), (0, MLA_IN_PAD - w_in.shape[1]))))
    r = w_uq.shape[0]
    uq = w_uq.reshape(r, MLA_HEADS, MLA_NOPE + MLA_ROPE)
    uq = jnp.pad(uq, ((0, 0), (0, 0), (0, QK_PAD - MLA_NOPE - MLA_ROPE)))
    w_uqt = _mx(uq.reshape(r, MLA_HEADS * QK_PAD).T)
    ukv = w_ukv.reshape(w_ukv.shape[0], MLA_HEADS, MLA_NOPE + MLA_DV)
    w_uk = _mx(ukv[:, :, :MLA_NOPE].reshape(-1, MLA_HEADS * MLA_NOPE))
    w_uvt = _mx(ukv[:, :, MLA_NOPE:].reshape(-1, MLA_HEADS * MLA_DV).T)
    return w_in_p, w_uqt, w_uk, w_uvt


def _rope_angles(pos, dim):
    inv = ROPE_BASE ** (-jnp.arange(0, dim, 2, dtype=F32) / dim)
    ang = pos.astype(F32)[:, None] * inv[None, :]
    return jnp.concatenate([ang, ang], axis=-1)


def _ret_tables(n_lat, rows):
    ang = _rope_angles(jnp.arange(n_lat), RET_DK)
    cos = jnp.ones((rows, RET_DK), F32).at[:n_lat].set(jnp.cos(ang))
    sin = jnp.sin(ang)
    sign = jnp.where(jnp.arange(RET_DK) < RET_DK // 2, -1.0, 1.0)
    sin_s = jnp.zeros((rows, RET_DK), F32).at[:n_lat].set(sin * sign)
    return cos, sin_s


def _axial_tables(n_lat, rows):
    n_rows = n_lat // GRID_W
    row = jnp.repeat(jnp.arange(n_rows), GRID_W)
    col = jnp.tile(jnp.arange(GRID_W), n_rows)
    half = MLA_ROPE // 2
    ang = jnp.concatenate([_rope_angles(row, half), _rope_angles(col, half)], axis=-1)
    lane = jnp.arange(MLA_ROPE)
    lo = (lane % half) < (half // 2)
    cos = jnp.ones((rows, LANES), F32).at[:n_lat, :MLA_ROPE].set(jnp.cos(ang))
    sin = jnp.sin(ang)
    sin_a = jnp.zeros((rows, LANES), F32).at[:n_lat, :MLA_ROPE].set(jnp.where(lo, -sin, 0.0))
    sin_b = jnp.zeros((rows, LANES), F32).at[:n_lat, :MLA_ROPE].set(jnp.where(lo, 0.0, sin))
    cos_rows = jnp.ones((MLA_ROPE, rows), F32).at[:, :n_lat].set(jnp.cos(ang).T)
    sin_rows = jnp.zeros((MLA_ROPE, rows), F32).at[:, :n_lat].set(sin.T)
    return cos, sin_a, sin_b, cos_rows, sin_rows


def kernel(x, c, ctx, c_ctx, w_ada, b_ada, ln_g, ln_b, w_ffn_in, w_ffn_out, ev_w_in, ev_gla_wg_f, ev_gla_bg_f, ev_gla_wg_b, ev_gla_bg_b, ev_gla_norm, ev_w_out, od_w_in, od_q_norm, od_kv_norm, od_w_uq, od_w_ukv, od_w_out):
    n_lat, d = x.shape[1], x.shape[2]
    n_ctx = ctx.shape[1]
    depth = w_ada.shape[0]
    assert x.shape[0] == 1 and n_lat % ROW_TILE == 0 and n_ctx == SCAN_ROWS
    assert (n_lat + n_ctx) % KV_TILE == 0
    alpha = (2 * depth) ** 0.25
    n_lat_tiles = n_lat // ROW_TILE
    n_tiles = n_lat_tiles + 1
    rows = n_tiles * ROW_TILE
    n_seq_blocks = (n_lat + n_ctx) // SCAN_ROWS

    x_all = x[0]
    x_tail = jnp.concatenate([ctx[0], jnp.zeros((ROW_TILE - n_ctx, d), x.dtype)], axis=0)
    c_rows = jnp.zeros((BF16_SUBLANES, d), F32).at[0].set(c[0]).at[1].set(c_ctx)
    mods = _ada(c_rows, w_ada, b_ada)[:, :2].reshape(depth, 2, N_MOD, d)

    ret_cos, ret_sin = _ret_tables(n_lat, rows)
    ax_tables = _axial_tables(n_lat, rows)

    wi, wo = _mx(w_ffn_in), _mx(w_ffn_out)
    ww = _ffn_whole_windows(wi)
    for l in range(depth):
        last = l == depth - 1
        x_all, h_mix = _ffn(x_all, mods[l], ww, wi, wo, ln_g[l, 0], ln_b[l, 0], layer=l, half=0,
                            mod_row=0, emit_row=3, alpha=alpha, n_lat_tiles=n_lat_tiles,
                            n_tiles=n_tiles, x_tail=x_tail if l == 0 else None)
        if l % 2 == 0:
            e = l // 2
            w_p, wg, bg = _prep_even(ev_w_in[e], ev_gla_wg_f[e], ev_gla_bg_f[e],
                                     ev_gla_wg_b[e], ev_gla_bg_b[e])
            z = _proj(h_mix, w_p, EV_PROJ_COLS)
            o_f, o_b = _even_scan(z, ret_cos, ret_sin, wg, bg, n_lat // SCAN_ROWS, n_seq_blocks)
            x_all = _even_out(o_f, o_b, z, ev_gla_norm[e], _mx(ev_w_out[e]), x_all, mods[l],
                              ln_g[l, 1], ln_b[l, 1], gate_row=5, alpha=alpha,
                              n_lat_tiles=n_lat_tiles)
        else:
            o = l // 2
            w_in_p, w_uqt, w_uk, w_uvt = _prep_mla(od_w_in[o], od_w_uq[o], od_w_ukv[o])
            qt, k, vt = _mla_proj(h_mix, w_in_p, od_q_norm[o], od_kv_norm[o], w_uqt, w_uk,
                                  w_uvt, ax_tables)
            a = _mla_attention(qt, k, vt, n_lat=n_lat, n_ctx=n_ctx, n_lat_tiles=n_lat_tiles)
            x_all = _outproj_ln(a, _mx(od_w_out[o]), x_all, mods[l], ln_g[l, 1], ln_b[l, 1],
                                gate_row=5, alpha=alpha, n_lat_tiles=n_lat_tiles)
        x_all, _ = _ffn(x_all, mods[l], ww, wi, wo, ln_g[l, 2], ln_b[l, 2], layer=l, half=1,
                        mod_row=6, emit_row=None, alpha=alpha, n_lat_tiles=n_lat_tiles,
                        n_tiles=n_lat_tiles if last else n_tiles)
    return x_all[None]
```
